```python
import math
import jax, jax.numpy as jnp
from jax import lax
import numpy as np

D_MODEL = 1024
BATCH = 8
SEQ = 4096
DEPTH = 4

N_MIXERS = 2
N_LAYERS_A = (DEPTH + 1) // 2
N_LAYERS_B = DEPTH // 2
D_FF = 2752
FFN_HALF = 0.5
ROPE_THETA = 500000.0
ROT_FRAC = 4
Q_BLOCK = 128
EPS = 1e-6
N_MOD = 9

A_HEADS = 16
A_HEAD_DIM = 64
A_KV_DIM = 64
IDX_HEADS = 8
IDX_DIM = 64
TOPK_MAX = 256
A_Q = A_HEADS * A_HEAD_DIM
A_IN = A_Q + 2 * A_KV_DIM + IDX_HEADS * IDX_DIM + IDX_DIM + IDX_HEADS

B_HEADS = 8
B_HEAD_DIM = 64
B_V_DIM = 2 * B_HEAD_DIM
B_QK = 2 * B_HEADS * B_HEAD_DIM
B_IN = 2 * B_QK + B_HEADS * B_V_DIM

kernel_name = "hybrid_dsa_diffattn_macaron_adaln"


def rmsnorm(x, g):
    xf = x.astype(jnp.float32)
    y = xf * lax.rsqrt(jnp.mean(xf * xf, axis=-1, keepdims=True) + EPS)
    return (y * g.astype(jnp.float32)).astype(x.dtype)


def modulate(h, shift, scale):
    return h * (1.0 + scale[:, None, :]) + shift[:, None, :]


def swiglu(h, w_gate, w_up, w_down):
    return (jax.nn.silu(h @ w_gate) * (h @ w_up)) @ w_down


def rope_tables(positions, head_dim):
    rot = head_dim // ROT_FRAC
    inv = ROPE_THETA ** (-jnp.arange(0, rot, 2, dtype=jnp.float32) / rot)
    ang = positions.astype(jnp.float32)[..., None] * inv
    return jnp.cos(ang), jnp.sin(ang)


def apply_rope(x, cos, sin):
    half = cos.shape[-1]
    rot = 2 * half
    shp = cos.shape[:2] + (1,) * (x.ndim - 3) + (half,)
    cos = cos.reshape(shp)
    sin = sin.reshape(shp)
    x1 = x[..., :half].astype(jnp.float32)
    x2 = x[..., half:rot].astype(jnp.float32)
    xr = jnp.concatenate([x1 * cos - x2 * sin, x2 * cos + x1 * sin], axis=-1)
    return jnp.concatenate([xr.astype(x.dtype), x[..., rot:]], axis=-1)


def dsa_mixer(h, w_in, w_out, cos, sin):
    B, S, _ = h.shape
    top_k = min(TOPK_MAX, S // 4)
    proj = h @ w_in
    o1 = A_Q
    o2 = o1 + A_KV_DIM
    o3 = o2 + A_KV_DIM
    o4 = o3 + IDX_HEADS * IDX_DIM
    o5 = o4 + IDX_DIM
    q = apply_rope(proj[..., :o1].reshape(B, S, A_HEADS, A_HEAD_DIM), cos, sin)
    k = apply_rope(proj[..., o1:o2], cos, sin)
    v = proj[..., o2:o3]
    qi = apply_rope(proj[..., o3:o4].reshape(B, S, IDX_HEADS, IDX_DIM), cos, sin)
    ki = apply_rope(proj[..., o4:o5], cos, sin)
    wi = proj[..., o5:].astype(jnp.float32) * (IDX_HEADS ** -0.5 * IDX_DIM ** -0.5)
    ki32 = ki.astype(jnp.float32)
    key_pos = jnp.arange(S)
    scale = A_HEAD_DIM ** -0.5
    gather = jax.vmap(lambda arr, ids: arr[ids])

    def block(i):
        start = i * Q_BLOCK
        t = start + jnp.arange(Q_BLOCK)
        q_b = lax.dynamic_slice_in_dim(q, start, Q_BLOCK, axis=1)
        qi_b = lax.dynamic_slice_in_dim(qi, start, Q_BLOCK, axis=1)
        wi_b = lax.dynamic_slice_in_dim(wi, start, Q_BLOCK, axis=1)
        logits = jnp.einsum('bthd,bsd->bths', qi_b.astype(jnp.float32), ki32)
        score = jnp.einsum('bth,bths->bts', wi_b, jax.nn.relu(logits))
        causal = key_pos[None, :] <= t[:, None]
        score = jnp.where(causal[None], score, -jnp.inf)
        _, idx = lax.top_k(score, top_k)
        k_sel = gather(k, idx).astype(jnp.float32)
        v_sel = gather(v, idx).astype(jnp.float32)
        s = jnp.einsum('bthd,btkd->bhtk', q_b.astype(jnp.float32), k_sel) * scale
        valid = idx <= t[None, :, None]
        s = jnp.where(valid[:, None], s, -jnp.inf)
        p = jax.nn.softmax(s, axis=-1)
        o = jnp.einsum('bhtk,btkd->bthd', p, v_sel)
        return o.reshape(B, Q_BLOCK, A_HEADS * A_KV_DIM).astype(h.dtype)

    out = lax.map(block, jnp.arange(S // Q_BLOCK))
    out = jnp.transpose(out, (1, 0, 2, 3)).reshape(B, S, A_HEADS * A_KV_DIM)
    return out @ w_out


def diff_mixer(h, w_in, w_out, lam, subln_g, cos, sin, layer_idx):
    B, S, _ = h.shape
    lam_init = 0.8 - 0.6 * math.exp(-0.3 * layer_idx)
    proj = h @ w_in
    q = apply_rope(proj[..., :B_QK].reshape(B, S, 2, B_HEADS, B_HEAD_DIM), cos, sin)
    k = apply_rope(proj[..., B_QK:2 * B_QK].reshape(B, S, 2, B_HEADS, B_HEAD_DIM), cos, sin)
    v = proj[..., 2 * B_QK:].reshape(B, S, B_HEADS, B_V_DIM)
    lam32 = lam.astype(jnp.float32)
    lam_val = (jnp.exp(jnp.sum(lam32[0] * lam32[1])) - jnp.exp(jnp.sum(lam32[2] * lam32[3]))
               + lam_init)
    k32 = k.astype(jnp.float32)
    v32 = v.astype(jnp.float32)
    key_pos = jnp.arange(S)
    scale = B_HEAD_DIM ** -0.5

    def block(i):
        start = i * Q_BLOCK
        t = start + jnp.arange(Q_BLOCK)
        q_b = lax.dynamic_slice_in_dim(q, start, Q_BLOCK, axis=1).astype(jnp.float32)
        s = jnp.einsum('btchd,bschd->bchts', q_b, k32) * scale
        causal = key_pos[None, :] <= t[:, None]
        s = jnp.where(causal, s, -jnp.inf)
        p = jax.nn.softmax(s, axis=-1)
        a = p[:, 0] - lam_val * p[:, 1]
        o = jnp.einsum('bhts,bshe->bthe', a, v32)
        o = rmsnorm(o, subln_g) * (1.0 - lam_init)
        return o.reshape(B, Q_BLOCK, B_HEADS * B_V_DIM).astype(h.dtype)

    out = lax.map(block, jnp.arange(S // Q_BLOCK))
    out = jnp.transpose(out, (1, 0, 2, 3)).reshape(B, S, B_HEADS * B_V_DIM)
    return out @ w_out


def setup_inputs(seed: int = 0) -> dict:
    key = jax.random.key(seed)
    ks = jax.random.split(key, 18)
    f32 = jnp.float32

    def nrm(k, shape, fan_in, scale=1.0):
        return jax.random.normal(k, shape, f32) * (scale * fan_in ** -0.5)

    x = jax.random.normal(ks[0], (BATCH, SEQ, D_MODEL), f32)
    c = jax.random.normal(ks[1], (BATCH, D_MODEL), f32)
    offsets = jax.random.randint(ks[2], (BATCH, 1), 0, 1024, dtype=jnp.int32)
    positions = offsets + jnp.arange(SEQ, dtype=jnp.int32)[None, :]
    return {
        "x": x,
        "c": c,
        "positions": positions,
        "ada_w": nrm(ks[3], (DEPTH, D_MODEL, N_MOD * D_MODEL), D_MODEL, 0.5),
        "ada_b": 0.02 * jax.random.normal(ks[4], (DEPTH, N_MOD * D_MODEL), f32),
        "pre_norm": 1.0 + 0.05 * jax.random.normal(ks[5], (DEPTH, 3, D_MODEL), f32),
        "post_norm": 1.0 + 0.05 * jax.random.normal(ks[6], (DEPTH, 3, D_MODEL), f32),
        "ffn_w_gate": nrm(ks[7], (DEPTH, 2, D_MODEL, D_FF), D_MODEL),
        "ffn_w_up": nrm(ks[8], (DEPTH, 2, D_MODEL, D_FF), D_MODEL),
        "ffn_w_down": nrm(ks[9], (DEPTH, 2, D_FF, D_MODEL), D_FF),
        "dsa_w_in": nrm(ks[10], (N_LAYERS_A, D_MODEL, A_IN), D_MODEL),
        "dsa_w_out": nrm(ks[11], (N_LAYERS_A, A_HEADS * A_KV_DIM, D_MODEL), A_HEADS * A_KV_DIM),
        "diff_w_in": nrm(ks[12], (N_LAYERS_B, D_MODEL, B_IN), D_MODEL),
        "diff_w_out": nrm(ks[13], (N_LAYERS_B, B_HEADS * B_V_DIM, D_MODEL), B_HEADS * B_V_DIM),
        "diff_lambda": 0.1 * jax.random.normal(ks[14], (N_LAYERS_B, 4, B_HEAD_DIM), f32),
        "diff_subln": 1.0 + 0.05 * jax.random.normal(ks[15], (N_LAYERS_B, B_V_DIM), f32),
    }


def reference(x, c, positions, ada_w, ada_b, pre_norm, post_norm, ffn_w_gate, ffn_w_up,
              ffn_w_down, dsa_w_in, dsa_w_out, diff_w_in, diff_w_out, diff_lambda, diff_subln):
    cos, sin = rope_tables(positions, A_HEAD_DIM)
    cond = jax.nn.silu(c)
    for i in range(DEPTH):
        mod = cond @ ada_w[i] + ada_b[i]
        sh0, sc0, g0, sh1, sc1, g1, sh2, sc2, g2 = jnp.split(mod, N_MOD, axis=-1)
        h = modulate(rmsnorm(x, pre_norm[i, 0]), sh0, sc0)
        y = swiglu(h, ffn_w_gate[i, 0], ffn_w_up[i, 0], ffn_w_down[i, 0])
        x = x + FFN_HALF * g0[:, None, :] * rmsnorm(y, post_norm[i, 0])
        h = modulate(rmsnorm(x, pre_norm[i, 1]), sh1, sc1)
        j = i // N_MIXERS
        if i % N_MIXERS == 0:
            y = dsa_mixer(h, dsa_w_in[j], dsa_w_out[j], cos, sin)
        else:
            y = diff_mixer(h, diff_w_in[j], diff_w_out[j], diff_lambda[j], diff_subln[j],
                           cos, sin, i)
        x = x + g1[:, None, :] * rmsnorm(y, post_norm[i, 1])
        h = modulate(rmsnorm(x, pre_norm[i, 2]), sh2, sc2)
        y = swiglu(h, ffn_w_gate[i, 1], ffn_w_up[i, 1], ffn_w_down[i, 1])
        x = x + FFN_HALF * g2[:, None, :] * rmsnorm(y, post_norm[i, 2])
    return x
```

```python
import functools
import math

import jax
import jax.numpy as jnp
from jax import lax
from jax.experimental import pallas as pl
from jax.experimental.pallas import tpu as pltpu

F32 = jnp.float32
BF16 = jnp.bfloat16

EPS = 1e-6
ROPE_THETA = 500000.0
HEAD_DIM = 64
ROT_DIM = HEAD_DIM // 4
ROT_HALF = ROT_DIM // 2
TOPK_MAX = 256
FFN_HALF = 0.5
N_MOD = 9

LANES = 128
MXU_DIM = 256
VMEM_BYTES_V7X = 64 * 1024 * 1024

MASKED = -1e30
INT_MIN = -(2 ** 31)
KEY_NEG_FLT_MAX = -2139095040


def _vmem_limit(nbytes):
    return int(min(nbytes, VMEM_BYTES_V7X - 6 * 1024 * 1024))


def _dot(a, b):
    return jnp.dot(a, b, preferred_element_type=F32)


def _dot_nt(a, b):
    return lax.dot_general(a, b, (((1,), (1,)), ((), ())), preferred_element_type=F32)


def _rms(y):
    return y * lax.rsqrt(jnp.mean(y * y, axis=-1, keepdims=True) + EPS)


def _prenorm_mod(x, gain, shift, scale):
    return (_rms(x) * gain) * (1.0 + scale) + shift


def _silu(g):
    return g / (1.0 + jnp.exp(-g))


def _mod_kernel(c_ref, w_ref, b_ref, o_ref):
    cond = _silu(c_ref[...])
    w = w_ref[0]
    c_hi = cond.astype(BF16)
    c_lo = (cond - c_hi.astype(F32)).astype(BF16)
    w_hi = w.astype(BF16)
    w_lo = (w - w_hi.astype(F32)).astype(BF16)
    o_ref[0] = _dot(c_hi, w_hi) + _dot(c_lo, w_hi) + _dot(c_hi, w_lo) + b_ref[0]


def _mod_call(c, ada_w, ada_b):
    depth, d, n = ada_w.shape
    b = c.shape[0]
    tn = 1024 if n % 1024 == 0 else n
    return pl.pallas_call(
        _mod_kernel,
        grid=(depth, n // tn),
        in_specs=[
            pl.BlockSpec((b, d), lambda l, j: (0, 0)),
            pl.BlockSpec((1, d, tn), lambda l, j: (l, 0, j)),
            pl.BlockSpec((1, 1, tn), lambda l, j: (l, 0, j)),
        ],
        out_specs=pl.BlockSpec((1, b, tn), lambda l, j: (l, 0, j)),
        out_shape=jax.ShapeDtypeStruct((depth, b, n), F32),
        compiler_params=pltpu.CompilerParams(
            dimension_semantics=("arbitrary", "arbitrary"),
            vmem_limit_bytes=_vmem_limit(40 * 1024 * 1024)),
        name="adaln_mod",
    )(c, ada_w, ada_b.reshape(depth, 1, n))


def _rope_kernel(ang_ref, cos_ref, sin_ref):
    a = ang_ref[0]
    lane = lax.broadcasted_iota(jnp.int32, a.shape, 1) & (HEAD_DIM - 1)
    rot = lane < ROT_DIM
    cos_ref[0] = jnp.where(rot, jnp.cos(a), 1.0)
    sin_ref[0] = jnp.where(rot, jnp.sin(a), 0.0)


def _rope_tables(positions):
    b, s = positions.shape
    inv = ROPE_THETA ** (-jnp.arange(0, ROT_DIM, 2, dtype=F32) / ROT_DIM)
    head = jnp.concatenate([-inv, inv, jnp.zeros((HEAD_DIM - ROT_DIM,), F32)])
    inv_lanes = jnp.tile(head, LANES // HEAD_DIM)
    ang = positions.astype(F32)[..., None] * inv_lanes
    tm = 512 if s % 512 == 0 else s
    spec = pl.BlockSpec((1, tm, LANES), lambda i, j: (i, j, 0))
    return pl.pallas_call(
        _rope_kernel,
        grid=(b, s // tm),
        in_specs=[spec],
        out_specs=[spec, spec],
        out_shape=[jax.ShapeDtypeStruct((b, s, LANES), F32)] * 2,
        compiler_params=pltpu.CompilerParams(dimension_semantics=("arbitrary", "arbitrary")),
        name="rope_tables",
    )(ang)


def _apply_rope(y, cos_f, sin_s, first_half):
    partner = jnp.where(first_half, pltpu.roll(y, LANES - ROT_HALF, 1), pltpu.roll(y, ROT_HALF, 1))
    return y * cos_f + partner * sin_s


def _ffn_kernel(x_ref, mod_ref, pre_ref, post_ref, wg_ref, wu_ref, wd_ref, o_ref, *, sub, f_chunks):
    x = x_ref[0]
    shift = mod_ref[0, 0, 3 * sub:3 * sub + 1, :]
    scale = mod_ref[0, 0, 3 * sub + 1:3 * sub + 2, :]
    gate = mod_ref[0, 0, 3 * sub + 2:3 * sub + 3, :]
    h = _prenorm_mod(x, pre_ref[0, sub:sub + 1, :], shift, scale).astype(BF16)
    y = None
    for f0, f1 in f_chunks:
        g = _dot(h, wg_ref[0, 0, :, f0:f1])
        u = _dot(h, wu_ref[0, 0, :, f0:f1])
        a = (_silu(g) * u).astype(BF16)
        part = _dot(a, wd_ref[0, 0, f0:f1, :])
        y = part if y is None else y + part
    o_ref[0] = x + (FFN_HALF * gate) * (_rms(y) * post_ref[0, sub:sub + 1, :])


def _ffn_call(x, mods, pre_norm, post_norm, wg, wu, wd, layer, sub, which):
    b, s, d = x.shape
    f = wg.shape[-1]
    tm = 512 if s % 512 == 0 else s
    step = 4 * MXU_DIM
    f_chunks = tuple((f0, min(f0 + step, f)) for f0 in range(0, f, step))
    kern = functools.partial(_ffn_kernel, sub=sub, f_chunks=f_chunks)
    once = pl.Buffered(1)
    return pl.pallas_call(
        kern,
        grid=(b, s // tm),
        in_specs=[
            pl.BlockSpec((1, tm, d), lambda i, j: (i, j, 0)),
            pl.BlockSpec((1, 1, N_MOD, d), lambda i, j: (layer, i, 0, 0)),
            pl.BlockSpec((1, 3, d), lambda i, j: (layer, 0, 0)),
            pl.BlockSpec((1, 3, d), lambda i, j: (layer, 0, 0)),
            pl.BlockSpec((1, 1, d, f), lambda i, j: (layer, which, 0, 0), pipeline_mode=once),
            pl.BlockSpec((1, 1, d, f), lambda i, j: (layer, which, 0, 0), pipeline_mode=once),
            pl.BlockSpec((1, 1, f, d), lambda i, j: (layer, which, 0, 0), pipeline_mode=once),
        ],
        out_specs=pl.BlockSpec((1, tm, d), lambda i, j: (i, j, 0)),
        out_shape=jax.ShapeDtypeStruct((b, s, d), F32),
        compiler_params=pltpu.CompilerParams(
            dimension_semantics=("arbitrary", "arbitrary"),
            vmem_limit_bytes=_vmem_limit(3 * d * f * 2 + 4 * tm * d * 4 + 6 * tm * step * 4
                                         + 8 * 1024 * 1024)),
        name="swiglu_half_step",
    )(x, mods, pre_norm, post_norm, wg, wu, wd)


def _proj_kernel(x_ref, mod_ref, pre_ref, cos_ref, sin_ref, w_ref, *out_refs, rope_cols, out_scales):
    x = x_ref[0]
    shift = mod_ref[0, 0, 3:4, :]
    scale = mod_ref[0, 0, 4:5, :]
    h = _prenorm_mod(x, pre_ref[0, 1:2, :], shift, scale).astype(BF16)
    cos_f = cos_ref[0]
    sin_s = sin_ref[0]
    lane = lax.broadcasted_iota(jnp.int32, cos_f.shape, 1) & (HEAD_DIM - 1)
    first_half = lane < ROT_HALF
    col = 0
    for out_ref, out_scale in zip(out_refs, out_scales):
        ncols = out_ref.shape[-1]
        y = _dot(h, w_ref[0, :, col:col + ncols])
        for j in range(ncols // LANES):
            blk = y[:, j * LANES:(j + 1) * LANES]
            if col + j * LANES < rope_cols:
                blk = _apply_rope(blk, cos_f, sin_s, first_half)
            if out_scale is not None:
                blk = blk * out_scale
            out_ref[0, :, j * LANES:(j + 1) * LANES] = blk.astype(out_ref.dtype)
        col += ncols


def _proj_call(x, mods, pre_norm, cos_f, sin_s, w, layer, wl, out_cols, out_dtypes, out_scales,
               rope_cols):
    b, s, d = x.shape
    n = w.shape[-1]
    tm = 512 if s % 512 == 0 else s
    kern = functools.partial(_proj_kernel, rope_cols=rope_cols, out_scales=out_scales)
    row = lambda i, j: (i, j, 0)
    return pl.pallas_call(
        kern,
        grid=(b, s // tm),
        in_specs=[
            pl.BlockSpec((1, tm, d), row),
            pl.BlockSpec((1, 1, N_MOD, d), lambda i, j: (layer, i, 0, 0)),
            pl.BlockSpec((1, 3, d), lambda i, j: (layer, 0, 0)),
            pl.BlockSpec((1, tm, LANES), row),
            pl.BlockSpec((1, tm, LANES), row),
            pl.BlockSpec((1, d, n), lambda i, j: (wl, 0, 0), pipeline_mode=pl.Buffered(1)),
        ],
        out_specs=[pl.BlockSpec((1, tm, nc), row) for nc in out_cols],
        out_shape=[jax.ShapeDtypeStruct((b, s, nc), dt) for nc, dt in zip(out_cols, out_dtypes)],
        compiler_params=pltpu.CompilerParams(
            dimension_semantics=("arbitrary", "arbitrary"),
            vmem_limit_bytes=_vmem_limit(d * n * 2 + 2 * tm * d * 4 + 6 * tm * n * 2
                                         + 6 * tm * 1024 * 4 + 8 * 1024 * 1024)),
        name="mixer_in_proj",
    )(x, mods, pre_norm, cos_f, sin_s, w)


def _outproj_kernel(o_ref, x_ref, mod_ref, post_ref, w_ref, out_ref):
    y = _dot(o_ref[0], w_ref[0])
    gate = mod_ref[0, 0, 5:6, :]
    out_ref[0] = x_ref[0] + gate * (_rms(y) * post_ref[0, 1:2, :])


def _outproj_call(o, x, mods, post_norm, w, layer, wl):
    b, s, d = x.shape
    k = o.shape[-1]
    tm = 512 if s % 512 == 0 else s
    row = lambda i, j: (i, j, 0)
    return pl.pallas_call(
        _outproj_kernel,
        grid=(b, s // tm),
        in_specs=[
            pl.BlockSpec((1, tm, k), row),
            pl.BlockSpec((1, tm, d), row),
            pl.BlockSpec((1, 1, N_MOD, d), lambda i, j: (layer, i, 0, 0)),
            pl.BlockSpec((1, 3, d), lambda i, j: (layer, 0, 0)),
            pl.BlockSpec((1, k, d), lambda i, j: (wl, 0, 0), pipeline_mode=pl.Buffered(1)),
        ],
        out_specs=pl.BlockSpec((1, tm, d), row),
        out_shape=jax.ShapeDtypeStruct((b, s, d), F32),
        compiler_params=pltpu.CompilerParams(
            dimension_semantics=("arbitrary", "arbitrary"),
            vmem_limit_bytes=_vmem_limit(k * d * 2 + 8 * tm * d * 4 + 8 * 1024 * 1024)),
        name="mixer_out_proj",
    )(o, x, mods, post_norm, w)


def _diff_attn_kernel(lam_ref, sub_ref, q1_ref, q2_ref, k1_ref, k2_ref, v_ref, o_ref,
                      m_ref, l_ref, acc_ref, *, tq, tk, lam_init):
    qi = pl.program_id(2)
    kj = pl.program_id(3)

    @pl.when(kj == 0)
    def _():
        m_ref[...] = jnp.full(m_ref.shape, MASKED, F32)
        l_ref[...] = jnp.zeros(l_ref.shape, F32)
        acc_ref[...] = jnp.zeros(acc_ref.shape, F32)

    @pl.when(kj * tk <= qi * tq + tq - 1)
    def _():
        lane = lax.broadcasted_iota(jnp.int32, (tq, LANES), 1)
        low = lane < HEAD_DIM
        row = qi * tq + lax.broadcasted_iota(jnp.int32, (tq, tk), 0)
        colk = kj * tk + lax.broadcasted_iota(jnp.int32, (tq, tk), 1)
        causal = colk <= row
        v = v_ref[0]
        for comp, (q_r, k_r) in enumerate(((q1_ref, k1_ref), (q2_ref, k2_ref))):
            q = q_r[0]
            k = k_r[0]
            for half in range(2):
                slot = 2 * comp + half
                qm = jnp.where(low if half == 0 else jnp.logical_not(low), q, jnp.zeros_like(q))
                s = jnp.where(causal, _dot_nt(qm, k), MASKED)
                m_prev = m_ref[slot]
                m_new = jnp.maximum(m_prev, jnp.max(s, axis=1, keepdims=True))
                alpha = jnp.exp(m_prev - m_new)
                p = jnp.exp(s - m_new)
                l_ref[slot] = alpha * l_ref[slot] + jnp.sum(p, axis=1, keepdims=True)
                pv = _dot(p.astype(BF16), v[:, half * 2 * HEAD_DIM:(half + 1) * 2 * HEAD_DIM])
                acc_ref[slot] = alpha * acc_ref[slot] + pv
                m_ref[slot] = m_new

    @pl.when(kj == pl.num_programs(3) - 1)
    def _():
        lam = lam_ref[0]
        lam_val = (jnp.exp(jnp.sum(lam[0:1] * lam[1:2], axis=1, keepdims=True))
                   - jnp.exp(jnp.sum(lam[2:3] * lam[3:4], axis=1, keepdims=True)) + lam_init)
        for half in range(2):
            o1 = acc_ref[half] / l_ref[half]
            o2 = acc_ref[2 + half] / l_ref[2 + half]
            o = _rms(o1 - lam_val * o2) * sub_ref[0] * (1.0 - lam_init)
            o_ref[0, :, half * 2 * HEAD_DIM:(half + 1) * 2 * HEAD_DIM] = o.astype(o_ref.dtype)


def _diff_attn_call(q, k, v, lam, subln, wl, lam_init):
    b, s, qk = q.shape
    pairs = qk // (2 * LANES)
    tq = 512 if s % 512 == 0 else s
    tk = tq
    v_dim = 2 * HEAD_DIM
    kern = functools.partial(_diff_attn_kernel, tq=tq, tk=tk, lam_init=lam_init)

    def kblock(bi, p, i, j):
        return jnp.minimum(j, (i * tq + tq - 1) // tk)

    return pl.pallas_call(
        kern,
        grid=(b, pairs, s // tq, s // tk),
        in_specs=[
            pl.BlockSpec((1, 4, HEAD_DIM), lambda bi, p, i, j: (wl, 0, 0)),
            pl.BlockSpec((1, 1, v_dim), lambda bi, p, i, j: (wl, 0, 0)),
            pl.BlockSpec((1, tq, LANES), lambda bi, p, i, j: (bi, i, p)),
            pl.BlockSpec((1, tq, LANES), lambda bi, p, i, j: (bi, i, pairs + p)),
            pl.BlockSpec((1, tk, LANES), lambda bi, p, i, j: (bi, kblock(bi, p, i, j), p)),
            pl.BlockSpec((1, tk, LANES), lambda bi, p, i, j: (bi, kblock(bi, p, i, j), pairs + p)),
            pl.BlockSpec((1, tk, 2 * v_dim), lambda bi, p, i, j: (bi, kblock(bi, p, i, j), p)),
        ],
        out_specs=pl.BlockSpec((1, tq, 2 * v_dim), lambda bi, p, i, j: (bi, i, p)),
        out_shape=jax.ShapeDtypeStruct((b, s, pairs * 2 * v_dim), BF16),
        scratch_shapes=[
            pltpu.VMEM((4, tq, 1), F32),
            pltpu.VMEM((4, tq, 1), F32),
            pltpu.VMEM((4, tq, v_dim), F32),
        ],
        compiler_params=pltpu.CompilerParams(
            dimension_semantics=("arbitrary", "arbitrary", "arbitrary", "arbitrary"),
            vmem_limit_bytes=_vmem_limit(40 * 1024 * 1024)),
        name="diff_attention",
    )(lam, subln.reshape(subln.shape[0], 1, v_dim), q, q, k, k, v)


def _dsa_attn_kernel(q_ref, qi_ref, wi_ref, kd_ref, kid_ref, vlo_ref, vhi_ref, o_ref,
                     sc_ref, m_ref, l_ref, acc_ref, *, tq, tkc, top_k, n_heads, n_idx_heads,
                     idx_bits):
    i = pl.program_id(1)
    nk = (i * tq + tq + tkc - 1) // tkc
    lane = lax.broadcasted_iota(jnp.int32, (tq, LANES), 1)
    low = lane < HEAD_DIM
    high = jnp.logical_not(low)
    row = i * tq + lax.broadcasted_iota(jnp.int32, (tq, tkc), 0)
    col0 = lax.broadcasted_iota(jnp.int32, (tq, tkc), 1)
    wi = wi_ref[0]
    k_f = float(top_k)

    def chunk(ref, kc):
        return ref[0, pl.ds(pl.multiple_of(kc * tkc, tkc), tkc), :]

    def score_body(kc, carry):
        kik = chunk(kid_ref, kc)
        score = jnp.zeros((tq, tkc), F32)
        for h in range(n_idx_heads):
            qij = qi_ref[0, :, (h // 2) * LANES:(h // 2 + 1) * LANES]
            qm = jnp.where(low if h % 2 == 0 else high, qij, jnp.zeros_like(qij))
            score = score + wi[:, h:h + 1] * jnp.maximum(_dot_nt(qm, kik), 0.0)
        sc_ref[kc] = jnp.where(kc * tkc + col0 <= row, score, -jnp.inf)
        return carry

    lax.fori_loop(0, nk, score_body, 0)

    def key_to_float(u):
        key = u ^ jnp.int32(INT_MIN)
        bits = jnp.where(key >= 0, key, key ^ jnp.int32(0x7FFFFFFF))
        return lax.bitcast_convert_type(bits, F32)

    def count(pred):
        def body(kc, cnt):
            hit = jnp.where(pred(sc_ref[kc], kc), 1.0, 0.0)
            return cnt + jnp.sum(hit, axis=1, keepdims=True)
        return lax.fori_loop(0, nk, body, jnp.zeros((tq, 1), F32))

    def bit_body(step, prefix):
        cand_u = prefix | lax.shift_left(jnp.int32(1), 31 - step)
        cand = key_to_float(cand_u)
        cnt = count(lambda sc, kc: sc >= cand)
        return jnp.where(cnt >= k_f, cand_u, prefix)

    prefix = lax.fori_loop(0, 32, bit_body, jnp.zeros((tq, 1), jnp.int32))
    thr_key = jnp.maximum(prefix ^ jnp.int32(INT_MIN), jnp.int32(KEY_NEG_FLT_MAX))
    thr = lax.bitcast_convert_type(
        jnp.where(thr_key >= 0, thr_key, thr_key ^ jnp.int32(0x7FFFFFFF)), F32)

    n_ge = count(lambda sc, kc: sc >= thr)

    @pl.when(jnp.max(n_ge) > k_f)
    def _():
        room = k_f - count(lambda sc, kc: sc > thr)

        def idx_body(step, bound):
            cand = bound | lax.shift_left(jnp.int32(1), idx_bits - 1 - step)
            cnt = count(lambda sc, kc: jnp.logical_and(sc == thr, kc * tkc + col0 < cand))
            return jnp.where(cnt <= room, cand, bound)

        bound = lax.fori_loop(0, idx_bits, idx_body, jnp.zeros((tq, 1), jnp.int32))

        def drop_body(kc, carry):
            sc = sc_ref[kc]
            surplus = jnp.logical_and(sc == thr, kc * tkc + col0 >= bound)
            sc_ref[kc] = jnp.where(surplus, -jnp.inf, sc)
            return carry

        lax.fori_loop(0, nk, drop_body, 0)

    m_ref[...] = jnp.full(m_ref.shape, MASKED, F32)
    l_ref[...] = jnp.zeros(l_ref.shape, F32)
    acc_ref[...] = jnp.zeros(acc_ref.shape, F32)

    def attn_body(kc, carry):
        kd = chunk(kd_ref, kc)
        v_halves = (chunk(vlo_ref, kc), chunk(vhi_ref, kc))
        bias = jnp.where(sc_ref[kc] >= thr, 0.0, MASKED)
        for j in range(n_heads // 2):
            qj = q_ref[0, :, j * LANES:(j + 1) * LANES]
            alphas = []
            pv = None
            for half in range(2):
                h = 2 * j + half
                qm = jnp.where(low if half == 0 else high, qj, jnp.zeros_like(qj))
                s = _dot_nt(qm, kd) + bias
                m_prev = m_ref[h]
                m_new = jnp.maximum(m_prev, jnp.max(s, axis=1, keepdims=True))
                alpha = jnp.exp(m_prev - m_new)
                p = jnp.exp(s - m_new)
                l_ref[h] = alpha * l_ref[h] + jnp.sum(p, axis=1, keepdims=True)
                m_ref[h] = m_new
                part = _dot(p.astype(BF16), v_halves[half])
                pv = part if pv is None else pv + part
                alphas.append(alpha)
            acc_ref[j] = acc_ref[j] * jnp.where(low, alphas[0], alphas[1]) + pv
        return carry

    lax.fori_loop(0, nk, attn_body, 0)

    for j in range(n_heads // 2):
        inv_l = jnp.where(low, 1.0 / l_ref[2 * j], 1.0 / l_ref[2 * j + 1])
        o_ref[0, :, j * LANES:(j + 1) * LANES] = (acc_ref[j] * inv_l).astype(o_ref.dtype)


def _dsa_attn_call(q, qi, wi, kd, kid, vlo, vhi, n_idx_heads):
    b, s, a_q = q.shape
    n_heads = a_q // HEAD_DIM
    top_k = min(TOPK_MAX, s // 4)
    tq = 256 if s % 256 == 0 else s
    tkc = 512 if s % 512 == 0 else s
    kern = functools.partial(_dsa_attn_kernel, tq=tq, tkc=tkc, top_k=top_k, n_heads=n_heads,
                             n_idx_heads=n_idx_heads, idx_bits=max(1, (s - 1).bit_length()) + 1)
    rowq = lambda bi, i: (bi, i, 0)
    allk = lambda bi, i: (bi, 0, 0)
    return pl.pallas_call(
        kern,
        grid=(b, s // tq),
        in_specs=[
            pl.BlockSpec((1, tq, a_q), rowq),
            pl.BlockSpec((1, tq, qi.shape[-1]), rowq),
            pl.BlockSpec((1, tq, LANES), rowq),
            pl.BlockSpec((1, s, LANES), allk),
            pl.BlockSpec((1, s, LANES), allk),
            pl.BlockSpec((1, s, LANES), allk),
            pl.BlockSpec((1, s, LANES), allk),
        ],
        out_specs=pl.BlockSpec((1, tq, a_q), rowq),
        out_shape=jax.ShapeDtypeStruct((b, s, a_q), BF16),
        scratch_shapes=[
            pltpu.VMEM((s // tkc, tq, tkc), F32),
            pltpu.VMEM((n_heads, tq, 1), F32),
            pltpu.VMEM((n_heads, tq, 1), F32),
            pltpu.VMEM((n_heads // 2, tq, LANES), F32),
        ],
        compiler_params=pltpu.CompilerParams(
            dimension_semantics=("arbitrary", "arbitrary"),
            vmem_limit_bytes=_vmem_limit(8 * s * LANES * 2 + tq * s * 4 + 2 * n_heads * tq * 512
                                         + 24 * 1024 * 1024)),
        name="dsa_attention",
    )(q, qi, wi, kd, kid, vlo, vhi)


def _pad_to(w, axis, mult):
    n = w.shape[axis]
    pad = (-n) % mult
    if pad == 0:
        return w
    widths = [(0, 0)] * w.ndim
    widths[axis] = (0, pad)
    return jnp.pad(w, widths)


def _prep_dsa_w_in(w, a_q, n_idx_heads):
    o1 = a_q
    o2 = o1 + HEAD_DIM
    o3 = o2 + HEAD_DIM
    o4 = o3 + n_idx_heads * HEAD_DIM
    o5 = o4 + HEAD_DIM
    wq = w[..., :o1] * (HEAD_DIM ** -0.5)
    wk, wv, wqi, wki, wwi = w[..., o1:o2], w[..., o2:o3], w[..., o3:o4], w[..., o4:o5], w[..., o5:]
    z = jnp.zeros_like(wv)
    cols = [wq, wqi, wk, wk, wki, wki, wv, z, z, wv, _pad_to(wwi, 2, LANES)]
    return jnp.concatenate(cols, axis=-1).astype(BF16)


def kernel(x, c, positions, ada_w, ada_b, pre_norm, post_norm, ffn_w_gate, ffn_w_up, ffn_w_down,
           dsa_w_in, dsa_w_out, diff_w_in, diff_w_out, diff_lambda, diff_subln):
    b, s, d = x.shape
    depth = ada_w.shape[0]
    assert HEAD_DIM ** -0.5 == 0.125

    mods = _mod_call(c, ada_w, ada_b).reshape(depth, b, N_MOD, d)
    cos_f, sin_s = _rope_tables(positions)

    wg = _pad_to(ffn_w_gate, 3, MXU_DIM).astype(BF16)
    wu = _pad_to(ffn_w_up, 3, MXU_DIM).astype(BF16)
    wd = _pad_to(ffn_w_down, 2, MXU_DIM).astype(BF16)

    a_q = dsa_w_out.shape[1]
    n_idx_heads = (dsa_w_in.shape[2] - a_q - 3 * HEAD_DIM) // (HEAD_DIM + 1)
    dsa_in = _prep_dsa_w_in(dsa_w_in, a_q, n_idx_heads)
    dsa_out = dsa_w_out.astype(BF16)
    idx_q = n_idx_heads * HEAD_DIM
    wi_scale = n_idx_heads ** -0.5 * HEAD_DIM ** -0.5

    b_out = diff_w_out.shape[1]
    b_qk = (diff_w_in.shape[2] - b_out) // 2
    diff_in = jnp.concatenate(
        [diff_w_in[..., :b_qk] * (HEAD_DIM ** -0.5), diff_w_in[..., b_qk:]], axis=-1).astype(BF16)
    diff_out = diff_w_out.astype(BF16)

    for i in range(depth):
        x = _ffn_call(x, mods, pre_norm, post_norm, wg, wu, wd, i, 0, 0)
        j = i // 2
        if i % 2 == 0:
            q, qi, kd, kid, vlo, vhi, wi = _proj_call(
                x, mods, pre_norm, cos_f, sin_s, dsa_in, i, j,
                out_cols=(a_q, idx_q, LANES, LANES, LANES, LANES, LANES),
                out_dtypes=(BF16,) * 6 + (F32,),
                out_scales=(None,) * 6 + (wi_scale,),
                rope_cols=a_q + idx_q + 2 * LANES)
            o = _dsa_attn_call(q, qi, wi, kd, kid, vlo, vhi, n_idx_heads)
            x = _outproj_call(o, x, mods, post_norm, dsa_out, i, j)
        else:
            lam_init = 0.8 - 0.6 * math.exp(-0.3 * i)
            q, k, v = _proj_call(
                x, mods, pre_norm, cos_f, sin_s, diff_in, i, j,
                out_cols=(b_qk, b_qk, b_out), out_dtypes=(BF16,) * 3, out_scales=(None,) * 3,
                rope_cols=2 * b_qk)
            o = _diff_attn_call(q, k, v, diff_lambda, diff_subln, j, lam_init)
            x = _outproj_call(o, x, mods, post_norm, diff_out, i, j)
        x = _ffn_call(x, mods, pre_norm, post_norm, wg, wu, wd, i, 2, 1)
    return x
```

```python
import functools
import math

import jax
import jax.numpy as jnp
from jax import lax
from jax.experimental import pallas as pl
from jax.experimental.pallas import tpu as pltpu

F32 = jnp.float32
BF16 = jnp.bfloat16

EPS = 1e-6
ROPE_THETA = 500000.0
HEAD_DIM = 64
ROT_DIM = HEAD_DIM // 4
ROT_HALF = ROT_DIM // 2
TOPK_MAX = 256
FFN_HALF = 0.5
N_MOD = 9

LANES = 128
SUBLANES = 8
MXU_DIM = 256
VMEM_BYTES_V7X = 64 * 1024 * 1024

ROW_TILE = 512
SCORE_PIECE = 128
SOFTMAX_STRIP = 64
COUNT_ROWS = 32

LOG2E = math.log2(math.e)
ATTN_Q_SCALE = HEAD_DIM ** -0.5 * LOG2E

MASKED = -1e30
INT_MIN = -(2 ** 31)
KEY_NEG_FLT_MAX = -2139095040


def _vmem_limit(nbytes):
    return int(min(nbytes, VMEM_BYTES_V7X - 6 * 1024 * 1024))


def _dot(a, b):
    return jnp.dot(a, b, preferred_element_type=F32)


def _dot_nt(a, b):
    return lax.dot_general(a, b, (((1,), (1,)), ((), ())), preferred_element_type=F32)


def _rms(y):
    return y * lax.rsqrt(jnp.mean(y * y, axis=-1, keepdims=True) + EPS)


def _prenorm_mod(x, gain, shift, scale):
    return (_rms(x) * gain) * (1.0 + scale) + shift


def _silu(g):
    return g / (1.0 + jnp.exp(-g))


def _fold_rows(x, op):
    return op(x.reshape(x.shape[0] // SUBLANES, SUBLANES, x.shape[1]), axis=0)


def _mod_kernel(c_ref, w_ref, b_ref, o_ref):
    cond = _silu(c_ref[...])
    w = w_ref[0]
    c_hi = cond.astype(BF16)
    c_lo = (cond - c_hi.astype(F32)).astype(BF16)
    w_hi = w.astype(BF16)
    w_lo = (w - w_hi.astype(F32)).astype(BF16)
    o_ref[0] = _dot(c_hi, w_hi) + _dot(c_lo, w_hi) + _dot(c_hi, w_lo) + b_ref[0]


def _mod_call(c, ada_w, ada_b):
    depth, d, n = ada_w.shape
    b = c.shape[0]
    tn = 1024 if n % 1024 == 0 else n
    return pl.pallas_call(
        _mod_kernel,
        grid=(depth, n // tn),
        in_specs=[
            pl.BlockSpec((b, d), lambda l, j: (0, 0)),
            pl.BlockSpec((1, d, tn), lambda l, j: (l, 0, j)),
            pl.BlockSpec((1, 1, tn), lambda l, j: (l, 0, j)),
        ],
        out_specs=pl.BlockSpec((1, b, tn), lambda l, j: (l, 0, j)),
        out_shape=jax.ShapeDtypeStruct((depth, b, n), F32),
        compiler_params=pltpu.CompilerParams(
            dimension_semantics=("arbitrary", "arbitrary"),
            vmem_limit_bytes=_vmem_limit(40 * 1024 * 1024)),
        name="adaln_mod",
    )(c, ada_w, ada_b.reshape(depth, 1, n))


def _rope_kernel(ang_ref, cos_ref, sin_ref):
    a = ang_ref[0]
    lane = lax.broadcasted_iota(jnp.int32, a.shape, 1) & (HEAD_DIM - 1)
    rot = lane < ROT_DIM
    cos_ref[0] = jnp.where(rot, jnp.cos(a), 1.0)
    sin_ref[0] = jnp.where(rot, jnp.sin(a), 0.0)


def _rope_tables(positions):
    b, s = positions.shape
    inv = ROPE_THETA ** (-jnp.arange(0, ROT_DIM, 2, dtype=F32) / ROT_DIM)
    head = jnp.concatenate([-inv, inv, jnp.zeros((HEAD_DIM - ROT_DIM,), F32)])
    inv_lanes = jnp.tile(head, LANES // HEAD_DIM)
    ang = positions.astype(F32)[..., None] * inv_lanes
    tm = ROW_TILE if s % ROW_TILE == 0 else s
    spec = pl.BlockSpec((1, tm, LANES), lambda i, j: (i, j, 0))
    return pl.pallas_call(
        _rope_kernel,
        grid=(b, s // tm),
        in_specs=[spec],
        out_specs=[spec, spec],
        out_shape=[jax.ShapeDtypeStruct((b, s, LANES), F32)] * 2,
        compiler_params=pltpu.CompilerParams(dimension_semantics=("arbitrary", "arbitrary")),
        name="rope_tables",
    )(ang)


def _apply_rope(y, cos_f, sin_s, first_half):
    partner = jnp.where(first_half, pltpu.roll(y, LANES - ROT_HALF, 1), pltpu.roll(y, ROT_HALF, 1))
    return y * cos_f + partner * sin_s


def _ffn_kernel(x_ref, mod_ref, pre_ref, post_ref, wg_ref, wu_ref, wd_ref, o_ref, *, sub, f_chunks):
    x = x_ref[0]
    shift = mod_ref[0, 0, 3 * sub:3 * sub + 1, :]
    scale = mod_ref[0, 0, 3 * sub + 1:3 * sub + 2, :]
    gate = mod_ref[0, 0, 3 * sub + 2:3 * sub + 3, :]
    h = _prenorm_mod(x, pre_ref[0, sub:sub + 1, :], shift, scale).astype(BF16)
    y = None
    for f0, f1 in f_chunks:
        g = _dot(h, wg_ref[0, 0, :, f0:f1])
        u = _dot(h, wu_ref[0, 0, :, f0:f1])
        a = (_silu(g) * u).astype(BF16)
        part = _dot(a, wd_ref[0, 0, f0:f1, :])
        y = part if y is None else y + part
    o_ref[0] = x + (FFN_HALF * gate) * (_rms(y) * post_ref[0, sub:sub + 1, :])


def _ffn_call(x, mods, pre_norm, post_norm, wg, wu, wd, layer, sub, which):
    b, s, d = x.shape
    f = wg.shape[-1]
    tm = ROW_TILE if s % ROW_TILE == 0 else s
    step = 4 * MXU_DIM
    f_chunks = tuple((f0, min(f0 + step, f)) for f0 in range(0, f, step))
    kern = functools.partial(_ffn_kernel, sub=sub, f_chunks=f_chunks)
    once = pl.Buffered(1)
    return pl.pallas_call(
        kern,
        grid=(b, s // tm),
        in_specs=[
            pl.BlockSpec((1, tm, d), lambda i, j: (i, j, 0)),
            pl.BlockSpec((1, 1, N_MOD, d), lambda i, j: (layer, i, 0, 0)),
            pl.BlockSpec((1, 3, d), lambda i, j: (layer, 0, 0)),
            pl.BlockSpec((1, 3, d), lambda i, j: (layer, 0, 0)),
            pl.BlockSpec((1, 1, d, f), lambda i, j: (layer, which, 0, 0), pipeline_mode=once),
            pl.BlockSpec((1, 1, d, f), lambda i, j: (layer, which, 0, 0), pipeline_mode=once),
            pl.BlockSpec((1, 1, f, d), lambda i, j: (layer, which, 0, 0), pipeline_mode=once),
        ],
        out_specs=pl.BlockSpec((1, tm, d), lambda i, j: (i, j, 0)),
        out_shape=jax.ShapeDtypeStruct((b, s, d), F32),
        compiler_params=pltpu.CompilerParams(
            dimension_semantics=("arbitrary", "arbitrary"),
            vmem_limit_bytes=_vmem_limit(3 * d * f * 2 + 4 * tm * d * 4 + 6 * tm * step * 4
                                         + 8 * 1024 * 1024)),
        name="swiglu_half_step",
    )(x, mods, pre_norm, post_norm, wg, wu, wd)


def _proj_kernel(x_ref, mod_ref, pre_ref, cos_ref, sin_ref, w_ref, wt_ref, *out_refs,
                 n_row_outs, t_scales):
    x = x_ref[0]
    shift = mod_ref[0, 0, 3:4, :]
    scale = mod_ref[0, 0, 4:5, :]
    h = _prenorm_mod(x, pre_ref[0, 1:2, :], shift, scale).astype(BF16)
    cos_f = cos_ref[0]
    sin_s = sin_ref[0]
    lane = lax.broadcasted_iota(jnp.int32, cos_f.shape, 1) & (HEAD_DIM - 1)
    first_half = lane < ROT_HALF
    col = 0
    for out_ref in out_refs[:n_row_outs]:
        ncols = out_ref.shape[-1]
        y = _dot(h, w_ref[0, :, col:col + ncols])
        for j in range(ncols // LANES):
            blk = _apply_rope(y[:, j * LANES:(j + 1) * LANES], cos_f, sin_s, first_half)
            out_ref[0, :, j * LANES:(j + 1) * LANES] = blk.astype(out_ref.dtype)
        col += ncols
    r = 0
    for out_ref, t_scale in zip(out_refs[n_row_outs:], t_scales):
        nrows = out_ref.shape[-2]
        yt = _dot_nt(wt_ref[0, r:r + nrows, :], h)
        if t_scale is not None:
            yt = yt * t_scale
        out_ref[0, 0] = yt.astype(out_ref.dtype)
        r += nrows


def _proj_call(x, mods, pre_norm, cos_f, sin_s, w, wt, layer, wl, row_cols, t_rows, t_dtypes,
               t_scales):
    b, s, d = x.shape
    n = w.shape[-1]
    nt = wt.shape[-2]
    tm = ROW_TILE if s % ROW_TILE == 0 else s
    kern = functools.partial(_proj_kernel, n_row_outs=len(row_cols), t_scales=t_scales)
    row = lambda i, j: (i, j, 0)
    once = pl.Buffered(1)
    return pl.pallas_call(
        kern,
        grid=(b, s // tm),
        in_specs=[
            pl.BlockSpec((1, tm, d), row),
            pl.BlockSpec((1, 1, N_MOD, d), lambda i, j: (layer, i, 0, 0)),
            pl.BlockSpec((1, 3, d), lambda i, j: (layer, 0, 0)),
            pl.BlockSpec((1, tm, LANES), row),
            pl.BlockSpec((1, tm, LANES), row),
            pl.BlockSpec((1, d, n), lambda i, j: (wl, 0, 0), pipeline_mode=once),
            pl.BlockSpec((1, nt, d), lambda i, j: (wl, 0, 0), pipeline_mode=once),
        ],
        out_specs=([pl.BlockSpec((1, tm, nc), row) for nc in row_cols]
                   + [pl.BlockSpec((1, 1, nr, tm), lambda i, j: (i, j, 0, 0)) for nr in t_rows]),
        out_shape=([jax.ShapeDtypeStruct((b, s, nc), BF16) for nc in row_cols]
                   + [jax.ShapeDtypeStruct((b, s // tm, nr, tm), dt)
                      for nr, dt in zip(t_rows, t_dtypes)]),
        compiler_params=pltpu.CompilerParams(
            dimension_semantics=("arbitrary", "arbitrary"),
            vmem_limit_bytes=_vmem_limit(d * (n + nt) * 2 + 2 * tm * d * 4 + 6 * tm * (n + nt) * 2
                                         + 6 * tm * 1024 * 4 + 8 * 1024 * 1024)),
        name="mixer_in_proj",
    )(x, mods, pre_norm, cos_f, sin_s, w, wt)


def _outproj_kernel(o_ref, x_ref, mod_ref, post_ref, w_ref, out_ref):
    y = _dot(o_ref[0], w_ref[0])
    gate = mod_ref[0, 0, 5:6, :]
    out_ref[0] = x_ref[0] + gate * (_rms(y) * post_ref[0, 1:2, :])


def _outproj_call(o, x, mods, post_norm, w, layer, wl):
    b, s, d = x.shape
    k = o.shape[-1]
    tm = ROW_TILE if s % ROW_TILE == 0 else s
    row = lambda i, j: (i, j, 0)
    return pl.pallas_call(
        _outproj_kernel,
        grid=(b, s // tm),
        in_specs=[
            pl.BlockSpec((1, tm, k), row),
            pl.BlockSpec((1, tm, d), row),
            pl.BlockSpec((1, 1, N_MOD, d), lambda i, j: (layer, i, 0, 0)),
            pl.BlockSpec((1, 3, d), lambda i, j: (layer, 0, 0)),
            pl.BlockSpec((1, k, d), lambda i, j: (wl, 0, 0), pipeline_mode=pl.Buffered(1)),
        ],
        out_specs=pl.BlockSpec((1, tm, d), row),
        out_shape=jax.ShapeDtypeStruct((b, s, d), F32),
        compiler_params=pltpu.CompilerParams(
            dimension_semantics=("arbitrary", "arbitrary"),
            vmem_limit_bytes=_vmem_limit(k * d * 2 + 8 * tm * d * 4 + 8 * 1024 * 1024)),
        name="mixer_out_proj",
    )(o, x, mods, post_norm, w)


def _online_softmax_step(s_ref, p_ref, vt, m_ref, l_ref, acc_ref, slot, interleave=()):
    tk = s_ref.shape[0]
    strips = list(range(0, tk, SOFTMAX_STRIP))
    due = {(j * 2 * len(strips)) // len(interleave): run for j, run in enumerate(interleave)}
    m8 = None
    for n, r in enumerate(strips):
        if n in due:
            due[n]()
        f = _fold_rows(s_ref[r:r + SOFTMAX_STRIP, :], jnp.max)
        m8 = f if m8 is None else jnp.maximum(m8, f)
    m_prev = m_ref[slot]
    m_new = jnp.maximum(m_prev, jnp.max(m8, axis=0, keepdims=True))
    alpha = jnp.exp2(m_prev - m_new)
    l8 = None
    for n, r in enumerate(strips):
        if len(strips) + n in due:
            due[len(strips) + n]()
        p = jnp.exp2(s_ref[r:r + SOFTMAX_STRIP, :] - m_new)
        p_ref[r:r + SOFTMAX_STRIP, :] = p.astype(BF16)
        f = _fold_rows(p, jnp.sum)
        l8 = f if l8 is None else l8 + f
    l_ref[slot] = alpha * l_ref[slot] + l8
    acc_ref[slot] = alpha * acc_ref[slot] + _dot(vt, p_ref[...])
    m_ref[slot] = m_new


def _diff_attn_kernel(lam_ref, sub_ref, q1_ref, q2_ref, k1_ref, k2_ref, vt_ref, o_ref,
                      m_ref, l_ref, acc_ref, s0_ref, s1_ref, p0_ref, p1_ref, *, tq, tk, lam_init):
    s_bufs = (s0_ref, s1_ref)
    p_bufs = (p0_ref, p1_ref)
    i = pl.program_id(2)
    lane = lax.broadcasted_iota(jnp.int32, (tq, LANES), 1)
    low = lane < HEAD_DIM
    qms = []
    for q_r in (q1_ref, q2_ref):
        q = q_r[0]
        qms.append(jnp.where(low, q, jnp.zeros_like(q)))
        qms.append(jnp.where(low, jnp.zeros_like(q), q))
    k_refs = (k1_ref, k2_ref)
    v_dim = 2 * HEAD_DIM

    m_ref[...] = jnp.full(m_ref.shape, MASKED, F32)
    l_ref[...] = jnp.zeros(l_ref.shape, F32)
    acc_ref[...] = jnp.zeros(acc_ref.shape, F32)

    def step(kc, on_diagonal):
        off = pl.multiple_of(kc * tk, tk)
        ks = [k_r[0, pl.ds(off, tk), :] for k_r in k_refs]
        vt = vt_ref[0, kc]
        if on_diagonal:
            key_idx = kc * tk + lax.broadcasted_iota(jnp.int32, (tk, tq), 0)
            qry_idx = i * tq + lax.broadcasted_iota(jnp.int32, (tk, tq), 1)
            keep = key_idx <= qry_idx
        def score_pieces(slot):
            buf = s_bufs[slot % 2]

            def piece(r):
                def run():
                    s = _dot_nt(ks[slot // 2][r:r + SCORE_PIECE, :], qms[slot])
                    if on_diagonal:
                        s = jnp.where(keep[r:r + SCORE_PIECE, :], s, MASKED)
                    buf[r:r + SCORE_PIECE, :] = s
                return run

            return [piece(r) for r in range(0, tk, SCORE_PIECE)]

        for run in score_pieces(0):
            run()
        for slot in range(4):
            nxt = score_pieces(slot + 1) if slot + 1 < 4 else ()
            half = slot % 2
            _online_softmax_step(s_bufs[slot % 2], p_bufs[slot % 2],
                                 vt[half * v_dim:(half + 1) * v_dim, :], m_ref, l_ref, acc_ref, slot,
                                 nxt)

    n_full = (i * tq) // tk

    def full_body(kc, carry):
        step(kc, False)
        return carry

    lax.fori_loop(0, n_full, full_body, 0)
    for d in range(tq // tk):
        step(n_full + d, True)

    lam = lam_ref[0]
    lam_val = (jnp.exp(jnp.sum(lam[0:1] * lam[1:2], axis=1, keepdims=True))
               - jnp.exp(jnp.sum(lam[2:3] * lam[3:4], axis=1, keepdims=True)) + lam_init)
    for half in range(2):
        o1 = acc_ref[half] / jnp.sum(l_ref[half], axis=0, keepdims=True)
        o2 = acc_ref[2 + half] / jnp.sum(l_ref[2 + half], axis=0, keepdims=True)
        o = (o1 - lam_val * o2).T
        o = _rms(o) * sub_ref[0] * (1.0 - lam_init)
        o_ref[0, :, half * v_dim:(half + 1) * v_dim] = o.astype(o_ref.dtype)


def _diff_attn_call(q, k, vt, lam, subln, wl, lam_init):
    b, s, qk = q.shape
    pairs = qk // (2 * LANES)
    nk, _, tk = vt.shape[1:]
    tq = tk
    v_dim = 2 * HEAD_DIM
    kern = functools.partial(_diff_attn_kernel, tq=tq, tk=tk, lam_init=lam_init)
    return pl.pallas_call(
        kern,
        grid=(b, pairs, s // tq),
        in_specs=[
            pl.BlockSpec((1, 4, HEAD_DIM), lambda bi, p, i: (wl, 0, 0)),
            pl.BlockSpec((1, 1, v_dim), lambda bi, p, i: (wl, 0, 0)),
            pl.BlockSpec((1, tq, LANES), lambda bi, p, i: (bi, i, p)),
            pl.BlockSpec((1, tq, LANES), lambda bi, p, i: (bi, i, pairs + p)),
            pl.BlockSpec((1, s, LANES), lambda bi, p, i: (bi, 0, p)),
            pl.BlockSpec((1, s, LANES), lambda bi, p, i: (bi, 0, pairs + p)),
            pl.BlockSpec((1, nk, 2 * v_dim, tk), lambda bi, p, i: (bi, 0, p, 0)),
        ],
        out_specs=pl.BlockSpec((1, tq, 2 * v_dim), lambda bi, p, i: (bi, i, p)),
        out_shape=jax.ShapeDtypeStruct((b, s, pairs * 2 * v_dim), BF16),
        scratch_shapes=[
            pltpu.VMEM((4, 1, tq), F32),
            pltpu.VMEM((4, SUBLANES, tq), F32),
            pltpu.VMEM((4, v_dim, tq), F32),
            pltpu.VMEM((tk, tq), F32),
            pltpu.VMEM((tk, tq), F32),
            pltpu.VMEM((tk, tq), BF16),
            pltpu.VMEM((tk, tq), BF16),
        ],
        compiler_params=pltpu.CompilerParams(
            dimension_semantics=("arbitrary", "arbitrary", "arbitrary"),
            vmem_limit_bytes=_vmem_limit(40 * 1024 * 1024)),
        name="diff_attention",
    )(lam, subln.reshape(subln.shape[0], 1, v_dim), q, q, k, k, vt)


def _dsa_attn_kernel(q_ref, qi_ref, wi_ref, kd_ref, kid_ref, vt_ref, o_ref,
                     sc_ref, m_ref, l_ref, acc_ref, bias_ref, s0_ref, s1_ref, p0_ref, p1_ref, *,
                     tq, tkc, top_k, n_heads, n_idx_heads, idx_bits):
    s_bufs = (s0_ref, s1_ref)
    p_bufs = (p0_ref, p1_ref)
    i = pl.program_id(1)
    nk = (i * tq + tq + tkc - 1) // tkc
    lane = lax.broadcasted_iota(jnp.int32, (tq, LANES), 1)
    low = lane < HEAD_DIM
    key0 = lax.broadcasted_iota(jnp.int32, (tkc, tq), 0)
    qry = i * tq + lax.broadcasted_iota(jnp.int32, (tkc, tq), 1)
    k_f = float(top_k)

    def head_operand(ref, h):
        pair = ref[0, :, (h // 2) * LANES:(h // 2 + 1) * LANES]
        zero = jnp.zeros_like(pair)
        return jnp.where(low, pair, zero) if h % 2 == 0 else jnp.where(low, zero, pair)

    def key_chunk(ref, kc):
        return ref[0, pl.ds(pl.multiple_of(kc * tkc, tkc), tkc), :]

    wt = wi_ref[0, 0]
    qims = [head_operand(qi_ref, h) for h in range(n_idx_heads)]

    def score_body(kc, carry):
        kik = key_chunk(kid_ref, kc)
        score = jnp.zeros((tkc, tq), F32)
        for h in range(n_idx_heads):
            score = score + wt[h:h + 1, :] * jnp.maximum(_dot_nt(kik, qims[h]), 0.0)
        sc_ref[kc] = jnp.where(kc * tkc + key0 <= qry, score, -jnp.inf)
        return carry

    lax.fori_loop(0, nk, score_body, 0)

    def key_to_float(u):
        key = u ^ jnp.int32(INT_MIN)
        bits = jnp.where(key >= 0, key, key ^ jnp.int32(0x7FFFFFFF))
        return lax.bitcast_convert_type(bits, F32)

    def count(pred):
        def body(kc, cnt):
            hit = jnp.where(pred(sc_ref[kc], kc), 1.0, 0.0)
            return cnt + jnp.sum(hit.reshape(tkc // COUNT_ROWS, COUNT_ROWS, tq), axis=0)
        cnt = lax.fori_loop(0, nk, body, jnp.zeros((COUNT_ROWS, tq), F32))
        return jnp.sum(cnt, axis=0, keepdims=True)

    few_keys = i * tq + lax.broadcasted_iota(jnp.int32, (1, tq), 1) < top_k - 1

    def bit_cond(state):
        step, _, settled = state
        return jnp.logical_and(step < 32, jnp.min(settled) < 0.5)

    def bit_body(state):
        step, prefix, settled = state
        cand_u = prefix | lax.shift_left(jnp.int32(1), 31 - step)
        cnt = count(lambda sc, kc: sc >= key_to_float(cand_u))
        settled = jnp.maximum(settled, jnp.where(cnt == k_f, 1.0, 0.0))
        return step + 1, jnp.where(cnt >= k_f, cand_u, prefix), settled

    steps, prefix, _ = lax.while_loop(
        bit_cond, bit_body,
        (jnp.int32(0), jnp.zeros((1, tq), jnp.int32), jnp.where(few_keys, 1.0, 0.0)))
    thr_key = jnp.maximum(prefix ^ jnp.int32(INT_MIN), jnp.int32(KEY_NEG_FLT_MAX))
    thr = lax.bitcast_convert_type(
        jnp.where(thr_key >= 0, thr_key, thr_key ^ jnp.int32(0x7FFFFFFF)), F32)

    @pl.when(steps == 32)
    def _():
        room = k_f - count(lambda sc, kc: sc > thr)

        def idx_body(step, bound):
            cand = bound | lax.shift_left(jnp.int32(1), idx_bits - 1 - step)
            cnt = count(lambda sc, kc: jnp.logical_and(sc == thr, kc * tkc + key0 < cand))
            return jnp.where(cnt <= room, cand, bound)

        bound = lax.fori_loop(0, idx_bits, idx_body, jnp.zeros((1, tq), jnp.int32))

        def drop_body(kc, carry):
            sc = sc_ref[kc]
            surplus = jnp.logical_and(sc == thr, kc * tkc + key0 >= bound)
            sc_ref[kc] = jnp.where(surplus, -jnp.inf, sc)
            return carry

        lax.fori_loop(0, nk, drop_body, 0)

    m_ref[...] = jnp.full(m_ref.shape, MASKED, F32)
    l_ref[...] = jnp.zeros(l_ref.shape, F32)
    acc_ref[...] = jnp.zeros(acc_ref.shape, F32)

    def attn_body(kc, carry):
        kd = key_chunk(kd_ref, kc)
        vt = vt_ref[0, kc]
        bias_ref[...] = jnp.where(sc_ref[kc] >= thr, 0.0, MASKED)

        def score_pieces(h):
            qm = head_operand(q_ref, h)
            buf = s_bufs[h % 2]

            def piece(r):
                def run():
                    buf[r:r + SCORE_PIECE, :] = (bias_ref[r:r + SCORE_PIECE, :]
                                                 + _dot_nt(kd[r:r + SCORE_PIECE, :], qm))
                return run

            return [piece(r) for r in range(0, tkc, SCORE_PIECE)]

        for run in score_pieces(0):
            run()
        for h in range(n_heads):
            nxt = score_pieces(h + 1) if h + 1 < n_heads else ()
            _online_softmax_step(s_bufs[h % 2], p_bufs[h % 2], vt, m_ref, l_ref, acc_ref, h, nxt)
        return carry

    lax.fori_loop(0, nk, attn_body, 0)

    for j in range(n_heads // 2):
        halves = [acc_ref[h] / jnp.sum(l_ref[h], axis=0, keepdims=True) for h in (2 * j, 2 * j + 1)]
        o_ref[0, :, j * LANES:(j + 1) * LANES] = jnp.concatenate(halves, axis=0).T.astype(o_ref.dtype)


def _dsa_attn_call(q, qi, wit, kd, kid, vt, n_idx_heads):
    b, s, a_q = q.shape
    n_heads = a_q // HEAD_DIM
    top_k = min(TOPK_MAX, s // 4)
    nkc, _, tkc = vt.shape[1:]
    tq = 256 if tkc % 256 == 0 else tkc
    per = tkc // tq
    kern = functools.partial(_dsa_attn_kernel, tq=tq, tkc=tkc, top_k=top_k, n_heads=n_heads,
                             n_idx_heads=n_idx_heads, idx_bits=max(1, (s - 1).bit_length()) + 1)
    rowq = lambda bi, i: (bi, i, 0)
    allk = lambda bi, i: (bi, 0, 0)
    return pl.pallas_call(
        kern,
        grid=(b, s // tq),
        in_specs=[
            pl.BlockSpec((1, tq, a_q), rowq),
            pl.BlockSpec((1, tq, qi.shape[-1]), rowq),
            pl.BlockSpec((1, 1, wit.shape[2], tq), lambda bi, i: (bi, i // per, 0, i % per)),
            pl.BlockSpec((1, s, LANES), allk),
            pl.BlockSpec((1, s, LANES), allk),
            pl.BlockSpec((1, nkc, HEAD_DIM, tkc), lambda bi, i: (bi, 0, 0, 0)),
        ],
        out_specs=pl.BlockSpec((1, tq, a_q), rowq),
        out_shape=jax.ShapeDtypeStruct((b, s, a_q), BF16),
        scratch_shapes=[
            pltpu.VMEM((nkc, tkc, tq), F32),
            pltpu.VMEM((n_heads, 1, tq), F32),
            pltpu.VMEM((n_heads, SUBLANES, tq), F32),
            pltpu.VMEM((n_heads, HEAD_DIM, tq), F32),
            pltpu.VMEM((tkc, tq), F32),
            pltpu.VMEM((tkc, tq), F32),
            pltpu.VMEM((tkc, tq), F32),
            pltpu.VMEM((tkc, tq), BF16),
            pltpu.VMEM((tkc, tq), BF16),
        ],
        compiler_params=pltpu.CompilerParams(
            dimension_semantics=("arbitrary", "arbitrary"),
            vmem_limit_bytes=_vmem_limit(6 * s * LANES * 2 + tq * s * 4 + 24 * 1024 * 1024)),
        name="dsa_attention",
    )(q, qi, wit, kd, kid, vt)


def _pad_to(w, axis, mult):
    n = w.shape[axis]
    pad = (-n) % mult
    if pad == 0:
        return w
    widths = [(0, 0)] * w.ndim
    widths[axis] = (0, pad)
    return jnp.pad(w, widths)


def _prep_dsa_w_in(w, a_q, n_idx_heads):
    o1 = a_q
    o2 = o1 + HEAD_DIM
    o3 = o2 + HEAD_DIM
    o4 = o3 + n_idx_heads * HEAD_DIM
    o5 = o4 + HEAD_DIM
    wq = w[..., :o1] * ATTN_Q_SCALE
    wk, wv, wqi, wki, wwi = w[..., o1:o2], w[..., o2:o3], w[..., o3:o4], w[..., o4:o5], w[..., o5:]
    rows = jnp.concatenate([wq, wqi, wk, wk, wki, wki], axis=-1).astype(BF16)
    cols = jnp.concatenate([wv, _pad_to(wwi, 2, 2 * SUBLANES)], axis=-1)
    return rows, jnp.swapaxes(cols, 1, 2).astype(BF16)


def kernel(x, c, positions, ada_w, ada_b, pre_norm, post_norm, ffn_w_gate, ffn_w_up, ffn_w_down,
           dsa_w_in, dsa_w_out, diff_w_in, diff_w_out, diff_lambda, diff_subln):
    b, s, d = x.shape
    depth = ada_w.shape[0]

    mods = _mod_call(c, ada_w, ada_b).reshape(depth, b, N_MOD, d)
    cos_f, sin_s = _rope_tables(positions)

    wg = _pad_to(ffn_w_gate, 3, MXU_DIM).astype(BF16)
    wu = _pad_to(ffn_w_up, 3, MXU_DIM).astype(BF16)
    wd = _pad_to(ffn_w_down, 2, MXU_DIM).astype(BF16)

    a_q = dsa_w_out.shape[1]
    n_idx_heads = (dsa_w_in.shape[2] - a_q - 3 * HEAD_DIM) // (HEAD_DIM + 1)
    dsa_in, dsa_in_t = _prep_dsa_w_in(dsa_w_in, a_q, n_idx_heads)
    dsa_out = dsa_w_out.astype(BF16)
    idx_q = n_idx_heads * HEAD_DIM
    wi_scale = n_idx_heads ** -0.5 * HEAD_DIM ** -0.5

    b_out = diff_w_out.shape[1]
    b_qk = (diff_w_in.shape[2] - b_out) // 2
    diff_in = jnp.concatenate(
        [diff_w_in[..., :b_qk] * ATTN_Q_SCALE, diff_w_in[..., b_qk:2 * b_qk]],
        axis=-1).astype(BF16)
    diff_in_t = jnp.swapaxes(diff_w_in[..., 2 * b_qk:], 1, 2).astype(BF16)
    diff_out = diff_w_out.astype(BF16)

    for i in range(depth):
        x = _ffn_call(x, mods, pre_norm, post_norm, wg, wu, wd, i, 0, 0)
        j = i // 2
        if i % 2 == 0:
            q, qi, kd, kid, vt, wit = _proj_call(
                x, mods, pre_norm, cos_f, sin_s, dsa_in, dsa_in_t, i, j,
                row_cols=(a_q, idx_q, LANES, LANES),
                t_rows=(HEAD_DIM, 2 * SUBLANES), t_dtypes=(BF16, F32), t_scales=(None, wi_scale))
            o = _dsa_attn_call(q, qi, wit, kd, kid, vt, n_idx_heads)
            x = _outproj_call(o, x, mods, post_norm, dsa_out, i, j)
        else:
            lam_init = 0.8 - 0.6 * math.exp(-0.3 * i)
            q, k, vt = _proj_call(
                x, mods, pre_norm, cos_f, sin_s, diff_in, diff_in_t, i, j,
                row_cols=(b_qk, b_qk), t_rows=(b_out,), t_dtypes=(BF16,), t_scales=(None,))
            o = _diff_attn_call(q, k, vt, diff_lambda, diff_subln, j, lam_init)
            x = _outproj_call(o, x, mods, post_norm, diff_out, i, j)
        x = _ffn_call(x, mods, pre_norm, post_norm, wg, wu, wd, i, 2, 1)
    return x
```

```python
import functools
import math

import jax
import jax.numpy as jnp
from jax import lax
from jax.experimental import pallas as pl
from jax.experimental.pallas import tpu as pltpu

F32 = jnp.float32
BF16 = jnp.bfloat16

EPS = 1e-6
ROPE_THETA = 500000.0
HEAD_DIM = 64
ROT_DIM = HEAD_DIM // 4
ROT_HALF = ROT_DIM // 2
TOPK_MAX = 256
FFN_HALF = 0.5
N_MOD = 9

LANES = 128
SUBLANES = 8
MXU_DIM = 256
VMEM_BYTES_V7X = 64 * 1024 * 1024

ROW_TILE = 512
DENOM_ROWS = 16
SCORE_PIECE = 128
SOFTMAX_STRIP = 128
COUNT_ROWS = 32

LOG2E = math.log2(math.e)
ATTN_Q_SCALE = HEAD_DIM ** -0.5 * LOG2E

MASKED = -1e30
INT_MIN = -(2 ** 31)
KEY_NEG_FLT_MAX = -2139095040


def _vmem_limit(nbytes):
    return int(min(nbytes, VMEM_BYTES_V7X - 6 * 1024 * 1024))


def _dot(a, b):
    return jnp.dot(a, b, preferred_element_type=F32)


def _dot_nt(a, b):
    return lax.dot_general(a, b, (((1,), (1,)), ((), ())), preferred_element_type=F32)


def _rms(y):
    return y * lax.rsqrt(jnp.mean(y * y, axis=-1, keepdims=True) + EPS)


def _prenorm_mod(x, gain, shift, scale):
    return (_rms(x) * gain) * (1.0 + scale) + shift


def _silu(g):
    return g / (1.0 + jnp.exp(-g))


def _fold_rows(x, op):
    return op(x.reshape(x.shape[0] // SUBLANES, SUBLANES, x.shape[1]), axis=0)


def _mod_kernel(c_ref, w_ref, b_ref, o_ref):
    cond = _silu(c_ref[...])
    w = w_ref[0]
    c_hi = cond.astype(BF16)
    c_lo = (cond - c_hi.astype(F32)).astype(BF16)
    w_hi = w.astype(BF16)
    w_lo = (w - w_hi.astype(F32)).astype(BF16)
    o_ref[0] = _dot(c_hi, w_hi) + _dot(c_lo, w_hi) + _dot(c_hi, w_lo) + b_ref[0]


def _mod_call(c, ada_w, ada_b):
    depth, d, n = ada_w.shape
    b = c.shape[0]
    tn = 1024 if n % 1024 == 0 else n
    return pl.pallas_call(
        _mod_kernel,
        grid=(depth, n // tn),
        in_specs=[
            pl.BlockSpec((b, d), lambda l, j: (0, 0)),
            pl.BlockSpec((1, d, tn), lambda l, j: (l, 0, j)),
            pl.BlockSpec((1, 1, tn), lambda l, j: (l, 0, j)),
        ],
        out_specs=pl.BlockSpec((1, b, tn), lambda l, j: (l, 0, j)),
        out_shape=jax.ShapeDtypeStruct((depth, b, n), F32),
        compiler_params=pltpu.CompilerParams(
            dimension_semantics=("arbitrary", "arbitrary"),
            vmem_limit_bytes=_vmem_limit(40 * 1024 * 1024)),
        name="adaln_mod",
    )(c, ada_w, ada_b.reshape(depth, 1, n))


def _rope_kernel(ang_ref, cos_ref, sin_ref):
    a = ang_ref[0]
    lane = lax.broadcasted_iota(jnp.int32, a.shape, 1) & (HEAD_DIM - 1)
    rot = lane < ROT_DIM
    cos_ref[0] = jnp.where(rot, jnp.cos(a), 1.0)
    sin_ref[0] = jnp.where(rot, jnp.sin(a), 0.0)


def _rope_tables(positions):
    b, s = positions.shape
    inv = ROPE_THETA ** (-jnp.arange(0, ROT_DIM, 2, dtype=F32) / ROT_DIM)
    head = jnp.concatenate([-inv, inv, jnp.zeros((HEAD_DIM - ROT_DIM,), F32)])
    inv_lanes = jnp.tile(head, LANES // HEAD_DIM)
    ang = positions.astype(F32)[..., None] * inv_lanes
    tm = ROW_TILE if s % ROW_TILE == 0 else s
    spec = pl.BlockSpec((1, tm, LANES), lambda i, j: (i, j, 0))
    return pl.pallas_call(
        _rope_kernel,
        grid=(b, s // tm),
        in_specs=[spec],
        out_specs=[spec, spec],
        out_shape=[jax.ShapeDtypeStruct((b, s, LANES), F32)] * 2,
        compiler_params=pltpu.CompilerParams(dimension_semantics=("arbitrary", "arbitrary")),
        name="rope_tables",
    )(ang)


def _apply_rope(y, cos_f, sin_s, first_half):
    partner = jnp.where(first_half, pltpu.roll(y, LANES - ROT_HALF, 1), pltpu.roll(y, ROT_HALF, 1))
    return y * cos_f + partner * sin_s


def _ffn_kernel(x_ref, mod_ref, pre_ref, post_ref, wg_ref, wu_ref, wd_ref, o_ref, *, sub, f_chunks):
    x = x_ref[0]
    shift = mod_ref[0, 0, 3 * sub:3 * sub + 1, :]
    scale = mod_ref[0, 0, 3 * sub + 1:3 * sub + 2, :]
    gate = mod_ref[0, 0, 3 * sub + 2:3 * sub + 3, :]
    h = _prenorm_mod(x, pre_ref[0, sub:sub + 1, :], shift, scale).astype(BF16)
    y = None
    for f0, f1 in f_chunks:
        g = _dot(h, wg_ref[0, 0, :, f0:f1])
        u = _dot(h, wu_ref[0, 0, :, f0:f1])
        a = (_silu(g) * u).astype(BF16)
        part = _dot(a, wd_ref[0, 0, f0:f1, :])
        y = part if y is None else y + part
    o_ref[0] = x + (FFN_HALF * gate) * (_rms(y) * post_ref[0, sub:sub + 1, :])


def _ffn_call(x, mods, pre_norm, post_norm, wg, wu, wd, layer, sub, which):
    b, s, d = x.shape
    f = wg.shape[-1]
    tm = ROW_TILE if s % ROW_TILE == 0 else s
    step = 4 * MXU_DIM
    f_chunks = tuple((f0, min(f0 + step, f)) for f0 in range(0, f, step))
    kern = functools.partial(_ffn_kernel, sub=sub, f_chunks=f_chunks)
    once = pl.Buffered(1)
    return pl.pallas_call(
        kern,
        grid=(b, s // tm),
        in_specs=[
            pl.BlockSpec((1, tm, d), lambda i, j: (i, j, 0)),
            pl.BlockSpec((1, 1, N_MOD, d), lambda i, j: (layer, i, 0, 0)),
            pl.BlockSpec((1, 3, d), lambda i, j: (layer, 0, 0)),
            pl.BlockSpec((1, 3, d), lambda i, j: (layer, 0, 0)),
            pl.BlockSpec((1, 1, d, f), lambda i, j: (layer, which, 0, 0), pipeline_mode=once),
            pl.BlockSpec((1, 1, d, f), lambda i, j: (layer, which, 0, 0), pipeline_mode=once),
            pl.BlockSpec((1, 1, f, d), lambda i, j: (layer, which, 0, 0), pipeline_mode=once),
        ],
        out_specs=pl.BlockSpec((1, tm, d), lambda i, j: (i, j, 0)),
        out_shape=jax.ShapeDtypeStruct((b, s, d), F32),
        compiler_params=pltpu.CompilerParams(
            dimension_semantics=("arbitrary", "arbitrary"),
            vmem_limit_bytes=_vmem_limit(3 * d * f * 2 + 4 * tm * d * 4 + 6 * tm * step * 4
                                         + 8 * 1024 * 1024)),
        name="swiglu_half_step",
    )(x, mods, pre_norm, post_norm, wg, wu, wd)


def _proj_kernel(x_ref, mod_ref, pre_ref, cos_ref, sin_ref, w_ref, wt_ref, *out_refs,
                 n_row_outs, t_scales):
    x = x_ref[0]
    shift = mod_ref[0, 0, 3:4, :]
    scale = mod_ref[0, 0, 4:5, :]
    h = _prenorm_mod(x, pre_ref[0, 1:2, :], shift, scale).astype(BF16)
    cos_f = cos_ref[0]
    sin_s = sin_ref[0]
    lane = lax.broadcasted_iota(jnp.int32, cos_f.shape, 1) & (HEAD_DIM - 1)
    first_half = lane < ROT_HALF
    col = 0
    for out_ref in out_refs[:n_row_outs]:
        ncols = out_ref.shape[-1]
        y = _dot(h, w_ref[0, :, col:col + ncols])
        for j in range(ncols // LANES):
            blk = _apply_rope(y[:, j * LANES:(j + 1) * LANES], cos_f, sin_s, first_half)
            out_ref[0, :, j * LANES:(j + 1) * LANES] = blk.astype(out_ref.dtype)
        col += ncols
    r = 0
    for out_ref, t_scale in zip(out_refs[n_row_outs:], t_scales):
        nrows = out_ref.shape[-2]
        yt = _dot_nt(wt_ref[0, r:r + nrows, :], h)
        if t_scale is not None:
            yt = yt * t_scale
        out_ref[0, 0] = yt.astype(out_ref.dtype)
        r += nrows


def _proj_call(x, mods, pre_norm, cos_f, sin_s, w, wt, layer, wl, row_cols, t_rows, t_dtypes,
               t_scales):
    b, s, d = x.shape
    n = w.shape[-1]
    nt = wt.shape[-2]
    tm = ROW_TILE if s % ROW_TILE == 0 else s
    kern = functools.partial(_proj_kernel, n_row_outs=len(row_cols), t_scales=t_scales)
    row = lambda i, j: (i, j, 0)
    once = pl.Buffered(1)
    return pl.pallas_call(
        kern,
        grid=(b, s // tm),
        in_specs=[
            pl.BlockSpec((1, tm, d), row),
            pl.BlockSpec((1, 1, N_MOD, d), lambda i, j: (layer, i, 0, 0)),
            pl.BlockSpec((1, 3, d), lambda i, j: (layer, 0, 0)),
            pl.BlockSpec((1, tm, LANES), row),
            pl.BlockSpec((1, tm, LANES), row),
            pl.BlockSpec((1, d, n), lambda i, j: (wl, 0, 0), pipeline_mode=once),
            pl.BlockSpec((1, nt, d), lambda i, j: (wl, 0, 0), pipeline_mode=once),
        ],
        out_specs=([pl.BlockSpec((1, tm, nc), row) for nc in row_cols]
                   + [pl.BlockSpec((1, 1, nr, tm), lambda i, j: (i, j, 0, 0)) for nr in t_rows]),
        out_shape=([jax.ShapeDtypeStruct((b, s, nc), BF16) for nc in row_cols]
                   + [jax.ShapeDtypeStruct((b, s // tm, nr, tm), dt)
                      for nr, dt in zip(t_rows, t_dtypes)]),
        compiler_params=pltpu.CompilerParams(
            dimension_semantics=("arbitrary", "arbitrary"),
            vmem_limit_bytes=_vmem_limit(d * (n + nt) * 2 + 2 * tm * d * 4 + 6 * tm * (n + nt) * 2
                                         + 6 * tm * 1024 * 4 + 8 * 1024 * 1024)),
        name="mixer_in_proj",
    )(x, mods, pre_norm, cos_f, sin_s, w, wt)


def _outproj_kernel(o_ref, x_ref, mod_ref, post_ref, w_ref, out_ref):
    y = _dot(o_ref[0], w_ref[0])
    gate = mod_ref[0, 0, 5:6, :]
    out_ref[0] = x_ref[0] + gate * (_rms(y) * post_ref[0, 1:2, :])


def _outproj_call(o, x, mods, post_norm, w, layer, wl):
    b, s, d = x.shape
    k = o.shape[-1]
    tm = ROW_TILE if s % ROW_TILE == 0 else s
    row = lambda i, j: (i, j, 0)
    return pl.pallas_call(
        _outproj_kernel,
        grid=(b, s // tm),
        in_specs=[
            pl.BlockSpec((1, tm, k), row),
            pl.BlockSpec((1, tm, d), row),
            pl.BlockSpec((1, 1, N_MOD, d), lambda i, j: (layer, i, 0, 0)),
            pl.BlockSpec((1, 3, d), lambda i, j: (layer, 0, 0)),
            pl.BlockSpec((1, k, d), lambda i, j: (wl, 0, 0), pipeline_mode=pl.Buffered(1)),
        ],
        out_specs=pl.BlockSpec((1, tm, d), row),
        out_shape=jax.ShapeDtypeStruct((b, s, d), F32),
        compiler_params=pltpu.CompilerParams(
            dimension_semantics=("arbitrary", "arbitrary"),
            vmem_limit_bytes=_vmem_limit(k * d * 2 + 8 * tm * d * 4 + 8 * 1024 * 1024)),
        name="mixer_out_proj",
    )(o, x, mods, post_norm, w)


def _online_softmax_step(s_ref, p_ref, vt, m_ref, l_ref, acc_ref, slot, interleave=()):
    tk = s_ref.shape[0]
    strips = list(range(0, tk, SOFTMAX_STRIP))
    due = {(j * 2 * len(strips)) // len(interleave): run for j, run in enumerate(interleave)}
    m8 = None
    for n, r in enumerate(strips):
        if n in due:
            due[n]()
        f = _fold_rows(s_ref[r:r + SOFTMAX_STRIP, :], jnp.max)
        m8 = f if m8 is None else jnp.maximum(m8, f)
    m_prev = m_ref[slot]
    m_new = jnp.maximum(m_prev, jnp.max(m8, axis=0, keepdims=True))
    alpha = jnp.exp2(m_prev - m_new)
    l8 = None
    for n, r in enumerate(strips):
        if len(strips) + n in due:
            due[len(strips) + n]()
        p = jnp.exp2(s_ref[r:r + SOFTMAX_STRIP, :] - m_new)
        p_ref[r:r + SOFTMAX_STRIP, :] = p.astype(BF16)
        if l_ref is not None:
            f = _fold_rows(p, jnp.sum)
            l8 = f if l8 is None else l8 + f
    if l_ref is not None:
        l_ref[slot] = alpha * l_ref[slot] + l8
    acc_ref[slot] = alpha * acc_ref[slot] + _dot(vt, p_ref[...])
    m_ref[slot] = m_new


def _diff_attn_kernel(lam_ref, sub_ref, q1_ref, q2_ref, k1_ref, k2_ref, vt_ref, o_ref,
                      m_ref, l_ref, acc_ref, s0_ref, s1_ref, p0_ref, p1_ref, *, tq, tk, lam_init,
                      pairs):
    s_bufs = (s0_ref, s1_ref)
    p_bufs = (p0_ref, p1_ref)
    i = pl.program_id(2)
    lane = lax.broadcasted_iota(jnp.int32, (tq, LANES), 1)
    low = lane < HEAD_DIM
    v_dim = 2 * HEAD_DIM
    n_slots = 4 * pairs

    def slot_parts(slot):
        return slot // 4, (slot % 4) // 2, slot % 2

    qms = []
    for slot in range(n_slots):
        pair, comp, half = slot_parts(slot)
        q = (q1_ref, q2_ref)[comp][0, :, pair * LANES:(pair + 1) * LANES]
        zero = jnp.zeros_like(q)
        qms.append(jnp.where(low, q, zero) if half == 0 else jnp.where(low, zero, q))
    k_refs = (k1_ref, k2_ref)

    m_ref[...] = jnp.full(m_ref.shape, MASKED, F32)
    l_ref[...] = jnp.zeros(l_ref.shape, F32)
    acc_ref[...] = jnp.zeros(acc_ref.shape, F32)

    def step(kc, on_diagonal):
        off = pl.multiple_of(kc * tk, tk)
        ks = [k_r[0, pl.ds(off, tk), :] for k_r in k_refs]
        vt = vt_ref[0, kc]
        if on_diagonal:
            key_idx = kc * tk + lax.broadcasted_iota(jnp.int32, (tk, tq), 0)
            qry_idx = i * tq + lax.broadcasted_iota(jnp.int32, (tk, tq), 1)
            keep = key_idx <= qry_idx
        def score_pieces(slot):
            pair, comp, _ = slot_parts(slot)
            buf = s_bufs[slot % 2]

            def piece(r):
                def run():
                    k = ks[comp][r:r + SCORE_PIECE, pair * LANES:(pair + 1) * LANES]
                    s = _dot_nt(k, qms[slot])
                    if on_diagonal:
                        s = jnp.where(keep[r:r + SCORE_PIECE, :], s, MASKED)
                    buf[r:r + SCORE_PIECE, :] = s
                return run

            return [piece(r) for r in range(0, tk, SCORE_PIECE)]

        for run in score_pieces(0):
            run()
        for slot in range(n_slots):
            nxt = score_pieces(slot + 1) if slot + 1 < n_slots else ()
            pair, _, half = slot_parts(slot)
            head = 2 * pair + half
            _online_softmax_step(s_bufs[slot % 2], p_bufs[slot % 2],
                                 vt[head * v_dim:(head + 1) * v_dim, :], m_ref, l_ref, acc_ref, slot,
                                 nxt)

    n_full = (i * tq) // tk

    def full_body(kc, carry):
        step(kc, False)
        return carry

    lax.fori_loop(0, n_full, full_body, 0)
    for d in range(tq // tk):
        step(n_full + d, True)

    lam = lam_ref[0]
    lam_val = (jnp.exp(jnp.sum(lam[0:1] * lam[1:2], axis=1, keepdims=True))
               - jnp.exp(jnp.sum(lam[2:3] * lam[3:4], axis=1, keepdims=True)) + lam_init)
    for pair in range(pairs):
        for half in range(2):
            s1 = 4 * pair + half
            s2 = s1 + 2
            o1 = acc_ref[s1] / jnp.sum(l_ref[s1], axis=0, keepdims=True)
            o2 = acc_ref[s2] / jnp.sum(l_ref[s2], axis=0, keepdims=True)
            o = (o1 - lam_val * o2).T
            o = _rms(o) * sub_ref[0] * (1.0 - lam_init)
            head = 2 * pair + half
            o_ref[0, :, head * v_dim:(head + 1) * v_dim] = o.astype(o_ref.dtype)


def _diff_attn_call(q, k, vt, lam, subln, wl, lam_init):
    b, s, qk = q.shape
    n_pairs = qk // (2 * LANES)
    pairs = 2 if n_pairs % 2 == 0 else 1
    groups = n_pairs // pairs
    nk, _, tk = vt.shape[1:]
    tq = tk
    v_dim = 2 * HEAD_DIM
    n_slots = 4 * pairs
    kern = functools.partial(_diff_attn_kernel, tq=tq, tk=tk, lam_init=lam_init, pairs=pairs)
    return pl.pallas_call(
        kern,
        grid=(b, groups, s // tq),
        in_specs=[
            pl.BlockSpec((1, 4, HEAD_DIM), lambda bi, g, i: (wl, 0, 0)),
            pl.BlockSpec((1, 1, v_dim), lambda bi, g, i: (wl, 0, 0)),
            pl.BlockSpec((1, tq, pairs * LANES), lambda bi, g, i: (bi, i, g)),
            pl.BlockSpec((1, tq, pairs * LANES), lambda bi, g, i: (bi, i, groups + g)),
            pl.BlockSpec((1, s, pairs * LANES), lambda bi, g, i: (bi, 0, g)),
            pl.BlockSpec((1, s, pairs * LANES), lambda bi, g, i: (bi, 0, groups + g)),
            pl.BlockSpec((1, nk, 2 * pairs * v_dim, tk), lambda bi, g, i: (bi, 0, g, 0)),
        ],
        out_specs=pl.BlockSpec((1, tq, 2 * pairs * v_dim), lambda bi, g, i: (bi, i, g)),
        out_shape=jax.ShapeDtypeStruct((b, s, n_pairs * 2 * v_dim), BF16),
        scratch_shapes=[
            pltpu.VMEM((n_slots, 1, tq), F32),
            pltpu.VMEM((n_slots, SUBLANES, tq), F32),
            pltpu.VMEM((n_slots, v_dim, tq), F32),
            pltpu.VMEM((tk, tq), F32),
            pltpu.VMEM((tk, tq), F32),
            pltpu.VMEM((tk, tq), BF16),
            pltpu.VMEM((tk, tq), BF16),
        ],
        compiler_params=pltpu.CompilerParams(
            dimension_semantics=("arbitrary", "arbitrary", "arbitrary"),
            vmem_limit_bytes=_vmem_limit(40 * 1024 * 1024)),
        name="diff_attention",
    )(lam, subln.reshape(subln.shape[0], 1, v_dim), q, q, k, k, vt)


def _dsa_attn_kernel(q_ref, qi_ref, wi_ref, kd_ref, kid_ref, vt_ref, o_ref,
                     sc_ref, m_ref, acc_ref, bias_ref, s0_ref, s1_ref, p0_ref, p1_ref, *,
                     tq, tkc, top_k, n_heads, n_idx_heads, idx_bits):
    s_bufs = (s0_ref, s1_ref)
    p_bufs = (p0_ref, p1_ref)
    i = pl.program_id(1)
    nk = (i * tq + tq + tkc - 1) // tkc
    lane = lax.broadcasted_iota(jnp.int32, (tq, LANES), 1)
    low = lane < HEAD_DIM
    key0 = lax.broadcasted_iota(jnp.int32, (tkc, tq), 0)
    qry = i * tq + lax.broadcasted_iota(jnp.int32, (tkc, tq), 1)
    k_f = float(top_k)

    def head_operand(ref, h):
        pair = ref[0, :, (h // 2) * LANES:(h // 2 + 1) * LANES]
        zero = jnp.zeros_like(pair)
        return jnp.where(low, pair, zero) if h % 2 == 0 else jnp.where(low, zero, pair)

    def key_chunk(ref, kc):
        return ref[0, pl.ds(pl.multiple_of(kc * tkc, tkc), tkc), :]

    wt = wi_ref[0, 0]
    qims = [head_operand(qi_ref, h) for h in range(n_idx_heads)]

    def score_body(kc, carry):
        kik = key_chunk(kid_ref, kc)
        score = jnp.zeros((tkc, tq), F32)
        for h in range(n_idx_heads):
            score = score + wt[h:h + 1, :] * jnp.maximum(_dot_nt(kik, qims[h]), 0.0)
        sc_ref[kc] = jnp.where(kc * tkc + key0 <= qry, score, -jnp.inf)
        return carry

    lax.fori_loop(0, nk, score_body, 0)

    def key_to_float(u):
        key = u ^ jnp.int32(INT_MIN)
        bits = jnp.where(key >= 0, key, key ^ jnp.int32(0x7FFFFFFF))
        return lax.bitcast_convert_type(bits, F32)

    def count(pred):
        def body(kc, cnt):
            hit = jnp.where(pred(sc_ref[kc], kc), 1.0, 0.0)
            return cnt + jnp.sum(hit.reshape(tkc // COUNT_ROWS, COUNT_ROWS, tq), axis=0)
        cnt = lax.fori_loop(0, nk, body, jnp.zeros((COUNT_ROWS, tq), F32))
        return jnp.sum(cnt, axis=0, keepdims=True)

    few_keys = i * tq + lax.broadcasted_iota(jnp.int32, (1, tq), 1) < top_k - 1

    def bit_cond(state):
        step, _, settled = state
        return jnp.logical_and(step < 32, jnp.min(settled) < 0.5)

    def bit_body(state):
        step, prefix, settled = state
        cand_u = prefix | lax.shift_left(jnp.int32(1), 31 - step)
        cnt = count(lambda sc, kc: sc >= key_to_float(cand_u))
        settled = jnp.maximum(settled, jnp.where(cnt == k_f, 1.0, 0.0))
        return step + 1, jnp.where(cnt >= k_f, cand_u, prefix), settled

    steps, prefix, _ = lax.while_loop(
        bit_cond, bit_body,
        (jnp.int32(0), jnp.zeros((1, tq), jnp.int32), jnp.where(few_keys, 1.0, 0.0)))
    thr_key = jnp.maximum(prefix ^ jnp.int32(INT_MIN), jnp.int32(KEY_NEG_FLT_MAX))
    thr = lax.bitcast_convert_type(
        jnp.where(thr_key >= 0, thr_key, thr_key ^ jnp.int32(0x7FFFFFFF)), F32)

    @pl.when(steps == 32)
    def _():
        room = k_f - count(lambda sc, kc: sc > thr)

        def idx_body(step, bound):
            cand = bound | lax.shift_left(jnp.int32(1), idx_bits - 1 - step)
            cnt = count(lambda sc, kc: jnp.logical_and(sc == thr, kc * tkc + key0 < cand))
            return jnp.where(cnt <= room, cand, bound)

        bound = lax.fori_loop(0, idx_bits, idx_body, jnp.zeros((1, tq), jnp.int32))

        def drop_body(kc, carry):
            sc = sc_ref[kc]
            surplus = jnp.logical_and(sc == thr, kc * tkc + key0 >= bound)
            sc_ref[kc] = jnp.where(surplus, -jnp.inf, sc)
            return carry

        lax.fori_loop(0, nk, drop_body, 0)

    m_ref[...] = jnp.full(m_ref.shape, MASKED, F32)
    acc_ref[...] = jnp.zeros(acc_ref.shape, F32)
    ones_rows = jnp.ones((DENOM_ROWS, tkc), BF16)

    def attn_body(kc, carry):
        kd = key_chunk(kd_ref, kc)
        vt = jnp.concatenate([vt_ref[0, kc], ones_rows], axis=0)
        bias_ref[...] = jnp.where(sc_ref[kc] >= thr, 0.0, MASKED)

        def score_pieces(h):
            qm = head_operand(q_ref, h)
            buf = s_bufs[h % 2]

            def piece(r):
                def run():
                    buf[r:r + SCORE_PIECE, :] = (bias_ref[r:r + SCORE_PIECE, :]
                                                 + _dot_nt(kd[r:r + SCORE_PIECE, :], qm))
                return run

            return [piece(r) for r in range(0, tkc, SCORE_PIECE)]

        for run in score_pieces(0):
            run()
        for h in range(n_heads):
            nxt = score_pieces(h + 1) if h + 1 < n_heads else ()
            _online_softmax_step(s_bufs[h % 2], p_bufs[h % 2], vt, m_ref, None, acc_ref, h, nxt)
        return carry

    lax.fori_loop(0, nk, attn_body, 0)

    for j in range(n_heads // 2):
        halves = [acc_ref[h, :HEAD_DIM, :] / acc_ref[h, HEAD_DIM:HEAD_DIM + 1, :]
                  for h in (2 * j, 2 * j + 1)]
        o_ref[0, :, j * LANES:(j + 1) * LANES] = jnp.concatenate(halves, axis=0).T.astype(o_ref.dtype)


def _dsa_attn_call(q, qi, wit, kd, kid, vt, n_idx_heads):
    b, s, a_q = q.shape
    n_heads = a_q // HEAD_DIM
    top_k = min(TOPK_MAX, s // 4)
    nkc, _, tkc = vt.shape[1:]
    tq = tkc
    per = tkc // tq
    kern = functools.partial(_dsa_attn_kernel, tq=tq, tkc=tkc, top_k=top_k, n_heads=n_heads,
                             n_idx_heads=n_idx_heads, idx_bits=max(1, (s - 1).bit_length()) + 1)
    rowq = lambda bi, i: (bi, i, 0)
    allk = lambda bi, i: (bi, 0, 0)
    return pl.pallas_call(
        kern,
        grid=(b, s // tq),
        in_specs=[
            pl.BlockSpec((1, tq, a_q), rowq),
            pl.BlockSpec((1, tq, qi.shape[-1]), rowq),
            pl.BlockSpec((1, 1, wit.shape[2], tq), lambda bi, i: (bi, i // per, 0, i % per)),
            pl.BlockSpec((1, s, LANES), allk),
            pl.BlockSpec((1, s, LANES), allk),
            pl.BlockSpec((1, nkc, HEAD_DIM, tkc), lambda bi, i: (bi, 0, 0, 0)),
        ],
        out_specs=pl.BlockSpec((1, tq, a_q), rowq),
        out_shape=jax.ShapeDtypeStruct((b, s, a_q), BF16),
        scratch_shapes=[
            pltpu.VMEM((nkc, tkc, tq), F32),
            pltpu.VMEM((n_heads, 1, tq), F32),
            pltpu.VMEM((n_heads, HEAD_DIM + DENOM_ROWS, tq), F32),
            pltpu.VMEM((tkc, tq), F32),
            pltpu.VMEM((tkc, tq), F32),
            pltpu.VMEM((tkc, tq), F32),
            pltpu.VMEM((tkc, tq), BF16),
            pltpu.VMEM((tkc, tq), BF16),
        ],
        compiler_params=pltpu.CompilerParams(
            dimension_semantics=("arbitrary", "arbitrary"),
            vmem_limit_bytes=_vmem_limit(6 * s * LANES * 2 + tq * s * 4 + 24 * 1024 * 1024)),
        name="dsa_attention",
    )(q, qi, wit, kd, kid, vt)


def _pad_to(w, axis, mult):
    n = w.shape[axis]
    pad = (-n) % mult
    if pad == 0:
        return w
    widths = [(0, 0)] * w.ndim
    widths[axis] = (0, pad)
    return jnp.pad(w, widths)


def _prep_dsa_w_in(w, a_q, n_idx_heads):
    o1 = a_q
    o2 = o1 + HEAD_DIM
    o3 = o2 + HEAD_DIM
    o4 = o3 + n_idx_heads * HEAD_DIM
    o5 = o4 + HEAD_DIM
    wq = w[..., :o1] * ATTN_Q_SCALE
    wk, wv, wqi, wki, wwi = w[..., o1:o2], w[..., o2:o3], w[..., o3:o4], w[..., o4:o5], w[..., o5:]
    rows = jnp.concatenate([wq, wqi, wk, wk, wki, wki], axis=-1).astype(BF16)
    cols = jnp.concatenate([wv, _pad_to(wwi, 2, 2 * SUBLANES)], axis=-1)
    return rows, jnp.swapaxes(cols, 1, 2).astype(BF16)


def kernel(x, c, positions, ada_w, ada_b, pre_norm, post_norm, ffn_w_gate, ffn_w_up, ffn_w_down,
           dsa_w_in, dsa_w_out, diff_w_in, diff_w_out, diff_lambda, diff_subln):
    b, s, d = x.shape
    depth = ada_w.shape[0]

    mods = _mod_call(c, ada_w, ada_b).reshape(depth, b, N_MOD, d)
    cos_f, sin_s = _rope_tables(positions)

    wg = _pad_to(ffn_w_gate, 3, MXU_DIM).astype(BF16)
    wu = _pad_to(ffn_w_up, 3, MXU_DIM).astype(BF16)
    wd = _pad_to(ffn_w_down, 2, MXU_DIM).astype(BF16)

    a_q = dsa_w_out.shape[1]
    n_idx_heads = (dsa_w_in.shape[2] - a_q - 3 * HEAD_DIM) // (HEAD_DIM + 1)
    dsa_in, dsa_in_t = _prep_dsa_w_in(dsa_w_in, a_q, n_idx_heads)
    dsa_out = dsa_w_out.astype(BF16)
    idx_q = n_idx_heads * HEAD_DIM
    wi_scale = n_idx_heads ** -0.5 * HEAD_DIM ** -0.5

    b_out = diff_w_out.shape[1]
    b_qk = (diff_w_in.shape[2] - b_out) // 2
    diff_in = jnp.concatenate(
        [diff_w_in[..., :b_qk] * ATTN_Q_SCALE, diff_w_in[..., b_qk:2 * b_qk]],
        axis=-1).astype(BF16)
    diff_in_t = jnp.swapaxes(diff_w_in[..., 2 * b_qk:], 1, 2).astype(BF16)
    diff_out = diff_w_out.astype(BF16)

    for i in range(depth):
        x = _ffn_call(x, mods, pre_norm, post_norm, wg, wu, wd, i, 0, 0)
        j = i // 2
        if i % 2 == 0:
            q, qi, kd, kid, vt, wit = _proj_call(
                x, mods, pre_norm, cos_f, sin_s, dsa_in, dsa_in_t, i, j,
                row_cols=(a_q, idx_q, LANES, LANES),
                t_rows=(HEAD_DIM, 2 * SUBLANES), t_dtypes=(BF16, F32), t_scales=(None, wi_scale))
            o = _dsa_attn_call(q, qi, wit, kd, kid, vt, n_idx_heads)
            x = _outproj_call(o, x, mods, post_norm, dsa_out, i, j)
        else:
            lam_init = 0.8 - 0.6 * math.exp(-0.3 * i)
            q, k, vt = _proj_call(
                x, mods, pre_norm, cos_f, sin_s, diff_in, diff_in_t, i, j,
                row_cols=(b_qk, b_qk), t_rows=(b_out,), t_dtypes=(BF16,), t_scales=(None,))
            o = _diff_attn_call(q, k, vt, diff_lambda, diff_subln, j, lam_init)
            x = _outproj_call(o, x, mods, post_norm, diff_out, i, j)
        x = _ffn_call(x, mods, pre_norm, post_norm, wg, wu, wd, i, 2, 1)
    return x
```

```python
import functools
import math

import jax
import jax.numpy as jnp
from jax import lax
from jax.experimental import pallas as pl
from jax.experimental.pallas import tpu as pltpu

F32 = jnp.float32
BF16 = jnp.bfloat16

EPS = 1e-6
ROPE_THETA = 500000.0
HEAD_DIM = 64
ROT_DIM = HEAD_DIM // 4
ROT_HALF = ROT_DIM // 2
TOPK_MAX = 256
FFN_HALF = 0.5
N_MOD = 9

LANES = 128
SUBLANES = 8
MXU_DIM = 256
VMEM_BYTES_V7X = 64 * 1024 * 1024

ROW_TILE = 512
DENOM_ROWS = 16
SCORE_PIECE = 128
SOFTMAX_STRIP = 128
COUNT_ROWS = 32

LOG2E = math.log2(math.e)
ATTN_Q_SCALE = HEAD_DIM ** -0.5 * LOG2E

MASKED = -1e30
INT_MIN = -(2 ** 31)
KEY_NEG_FLT_MAX = -2139095040


def _vmem_limit(nbytes):
    return int(min(nbytes, VMEM_BYTES_V7X - 6 * 1024 * 1024))


def _dot(a, b):
    return jnp.dot(a, b, preferred_element_type=F32)


def _dot_nt(a, b):
    return lax.dot_general(a, b, (((1,), (1,)), ((), ())), preferred_element_type=F32)


def _rms(y):
    return y * lax.rsqrt(jnp.mean(y * y, axis=-1, keepdims=True) + EPS)


def _prenorm_mod(x, gain, shift, scale):
    return (_rms(x) * gain) * (1.0 + scale) + shift


def _silu(g):
    return g / (1.0 + jnp.exp(-g))


def _fold_rows(x, op):
    return op(x.reshape(x.shape[0] // SUBLANES, SUBLANES, x.shape[1]), axis=0)


def _mod_kernel(c_ref, w_ref, b_ref, o_ref):
    cond = _silu(c_ref[...])
    w = w_ref[0]
    c_hi = cond.astype(BF16)
    c_lo = (cond - c_hi.astype(F32)).astype(BF16)
    w_hi = w.astype(BF16)
    w_lo = (w - w_hi.astype(F32)).astype(BF16)
    o_ref[0] = _dot(c_hi, w_hi) + _dot(c_lo, w_hi) + _dot(c_hi, w_lo) + b_ref[0]


def _mod_call(c, ada_w, ada_b):
    depth, d, n = ada_w.shape
    b = c.shape[0]
    tn = 1024 if n % 1024 == 0 else n
    return pl.pallas_call(
        _mod_kernel,
        grid=(depth, n // tn),
        in_specs=[
            pl.BlockSpec((b, d), lambda l, j: (0, 0)),
            pl.BlockSpec((1, d, tn), lambda l, j: (l, 0, j)),
            pl.BlockSpec((1, 1, tn), lambda l, j: (l, 0, j)),
        ],
        out_specs=pl.BlockSpec((1, b, tn), lambda l, j: (l, 0, j)),
        out_shape=jax.ShapeDtypeStruct((depth, b, n), F32),
        compiler_params=pltpu.CompilerParams(
            dimension_semantics=("arbitrary", "arbitrary"),
            vmem_limit_bytes=_vmem_limit(40 * 1024 * 1024)),
        name="adaln_mod",
    )(c, ada_w, ada_b.reshape(depth, 1, n))


def _rope_kernel(ang_ref, cos_ref, sin_ref):
    a = ang_ref[0]
    lane = lax.broadcasted_iota(jnp.int32, a.shape, 1) & (HEAD_DIM - 1)
    rot = lane < ROT_DIM
    cos_ref[0] = jnp.where(rot, jnp.cos(a), 1.0)
    sin_ref[0] = jnp.where(rot, jnp.sin(a), 0.0)


def _rope_tables(positions):
    b, s = positions.shape
    inv = ROPE_THETA ** (-jnp.arange(0, ROT_DIM, 2, dtype=F32) / ROT_DIM)
    head = jnp.concatenate([-inv, inv, jnp.zeros((HEAD_DIM - ROT_DIM,), F32)])
    inv_lanes = jnp.tile(head, LANES // HEAD_DIM)
    ang = positions.astype(F32)[..., None] * inv_lanes
    tm = ROW_TILE if s % ROW_TILE == 0 else s
    spec = pl.BlockSpec((1, tm, LANES), lambda i, j: (i, j, 0))
    return pl.pallas_call(
        _rope_kernel,
        grid=(b, s // tm),
        in_specs=[spec],
        out_specs=[spec, spec],
        out_shape=[jax.ShapeDtypeStruct((b, s, LANES), F32)] * 2,
        compiler_params=pltpu.CompilerParams(dimension_semantics=("arbitrary", "arbitrary")),
        name="rope_tables",
    )(ang)


def _apply_rope(y, cos_f, sin_s, first_half):
    partner = jnp.where(first_half, pltpu.roll(y, LANES - ROT_HALF, 1), pltpu.roll(y, ROT_HALF, 1))
    return y * cos_f + partner * sin_s


def _ffn_kernel(x_ref, mod_ref, pre_ref, post_ref, wg_ref, wu_ref, wd_ref, o_ref, *, sub, f_chunks):
    x = x_ref[0]
    shift = mod_ref[0, 0, 3 * sub:3 * sub + 1, :]
    scale = mod_ref[0, 0, 3 * sub + 1:3 * sub + 2, :]
    gate = mod_ref[0, 0, 3 * sub + 2:3 * sub + 3, :]
    h = _prenorm_mod(x, pre_ref[0, sub:sub + 1, :], shift, scale).astype(BF16)
    y = None
    for f0, f1 in f_chunks:
        g = _dot(h, wg_ref[0, 0, :, f0:f1])
        u = _dot(h, wu_ref[0, 0, :, f0:f1])
        a = (_silu(g) * u).astype(BF16)
        part = _dot(a, wd_ref[0, 0, f0:f1, :])
        y = part if y is None else y + part
    o_ref[0] = x + (FFN_HALF * gate) * (_rms(y) * post_ref[0, sub:sub + 1, :])


def _ffn_call(x, mods, pre_norm, post_norm, wg, wu, wd, layer, sub, which):
    b, s, d = x.shape
    f = wg.shape[-1]
    tm = ROW_TILE if s % ROW_TILE == 0 else s
    step = 4 * MXU_DIM
    f_chunks = tuple((f0, min(f0 + step, f)) for f0 in range(0, f, step))
    kern = functools.partial(_ffn_kernel, sub=sub, f_chunks=f_chunks)
    once = pl.Buffered(1)
    return pl.pallas_call(
        kern,
        grid=(b, s // tm),
        in_specs=[
            pl.BlockSpec((1, tm, d), lambda i, j: (i, j, 0)),
            pl.BlockSpec((1, 1, N_MOD, d), lambda i, j: (layer, i, 0, 0)),
            pl.BlockSpec((1, 3, d), lambda i, j: (layer, 0, 0)),
            pl.BlockSpec((1, 3, d), lambda i, j: (layer, 0, 0)),
            pl.BlockSpec((1, 1, d, f), lambda i, j: (layer, which, 0, 0), pipeline_mode=once),
            pl.BlockSpec((1, 1, d, f), lambda i, j: (layer, which, 0, 0), pipeline_mode=once),
            pl.BlockSpec((1, 1, f, d), lambda i, j: (layer, which, 0, 0), pipeline_mode=once),
        ],
        out_specs=pl.BlockSpec((1, tm, d), lambda i, j: (i, j, 0)),
        out_shape=jax.ShapeDtypeStruct((b, s, d), F32),
        compiler_params=pltpu.CompilerParams(
            dimension_semantics=("arbitrary", "arbitrary"),
            vmem_limit_bytes=_vmem_limit(3 * d * f * 2 + 4 * tm * d * 4 + 6 * tm * step * 4
                                         + 8 * 1024 * 1024)),
        name="swiglu_half_step",
    )(x, mods, pre_norm, post_norm, wg, wu, wd)


def _proj_kernel(x_ref, mod_ref, pre_ref, cos_ref, sin_ref, w_ref, wt_ref, *out_refs,
                 n_row_outs, t_scales):
    x = x_ref[0]
    shift = mod_ref[0, 0, 3:4, :]
    scale = mod_ref[0, 0, 4:5, :]
    h = _prenorm_mod(x, pre_ref[0, 1:2, :], shift, scale).astype(BF16)
    cos_f = cos_ref[0]
    sin_s = sin_ref[0]
    lane = lax.broadcasted_iota(jnp.int32, cos_f.shape, 1) & (HEAD_DIM - 1)
    first_half = lane < ROT_HALF
    col = 0
    for out_ref in out_refs[:n_row_outs]:
        ncols = out_ref.shape[-1]
        y = _dot(h, w_ref[0, :, col:col + ncols])
        for j in range(ncols // LANES):
            blk = _apply_rope(y[:, j * LANES:(j + 1) * LANES], cos_f, sin_s, first_half)
            out_ref[0, :, j * LANES:(j + 1) * LANES] = blk.astype(out_ref.dtype)
        col += ncols
    r = 0
    for out_ref, t_scale in zip(out_refs[n_row_outs:], t_scales):
        nrows = out_ref.shape[-2]
        yt = _dot_nt(wt_ref[0, r:r + nrows, :], h)
        if t_scale is not None:
            yt = yt * t_scale
        out_ref[0, 0] = yt.astype(out_ref.dtype)
        r += nrows


def _proj_call(x, mods, pre_norm, cos_f, sin_s, w, wt, layer, wl, row_cols, t_rows, t_dtypes,
               t_scales):
    b, s, d = x.shape
    n = w.shape[-1]
    nt = wt.shape[-2]
    tm = ROW_TILE if s % ROW_TILE == 0 else s
    kern = functools.partial(_proj_kernel, n_row_outs=len(row_cols), t_scales=t_scales)
    row = lambda i, j: (i, j, 0)
    once = pl.Buffered(1)
    return pl.pallas_call(
        kern,
        grid=(b, s // tm),
        in_specs=[
            pl.BlockSpec((1, tm, d), row),
            pl.BlockSpec((1, 1, N_MOD, d), lambda i, j: (layer, i, 0, 0)),
            pl.BlockSpec((1, 3, d), lambda i, j: (layer, 0, 0)),
            pl.BlockSpec((1, tm, LANES), row),
            pl.BlockSpec((1, tm, LANES), row),
            pl.BlockSpec((1, d, n), lambda i, j: (wl, 0, 0), pipeline_mode=once),
            pl.BlockSpec((1, nt, d), lambda i, j: (wl, 0, 0), pipeline_mode=once),
        ],
        out_specs=([pl.BlockSpec((1, tm, nc), row) for nc in row_cols]
                   + [pl.BlockSpec((1, 1, nr, tm), lambda i, j: (i, j, 0, 0)) for nr in t_rows]),
        out_shape=([jax.ShapeDtypeStruct((b, s, nc), BF16) for nc in row_cols]
                   + [jax.ShapeDtypeStruct((b, s // tm, nr, tm), dt)
                      for nr, dt in zip(t_rows, t_dtypes)]),
        compiler_params=pltpu.CompilerParams(
            dimension_semantics=("arbitrary", "arbitrary"),
            vmem_limit_bytes=_vmem_limit(d * (n + nt) * 2 + 2 * tm * d * 4 + 6 * tm * (n + nt) * 2
                                         + 6 * tm * 1024 * 4 + 8 * 1024 * 1024)),
        name="mixer_in_proj",
    )(x, mods, pre_norm, cos_f, sin_s, w, wt)


def _outproj_kernel(o_ref, x_ref, mod_ref, post_ref, w_ref, out_ref):
    y = _dot(o_ref[0], w_ref[0])
    gate = mod_ref[0, 0, 5:6, :]
    out_ref[0] = x_ref[0] + gate * (_rms(y) * post_ref[0, 1:2, :])


def _outproj_call(o, x, mods, post_norm, w, layer, wl):
    b, s, d = x.shape
    k = o.shape[-1]
    tm = ROW_TILE if s % ROW_TILE == 0 else s
    row = lambda i, j: (i, j, 0)
    return pl.pallas_call(
        _outproj_kernel,
        grid=(b, s // tm),
        in_specs=[
            pl.BlockSpec((1, tm, k), row),
            pl.BlockSpec((1, tm, d), row),
            pl.BlockSpec((1, 1, N_MOD, d), lambda i, j: (layer, i, 0, 0)),
            pl.BlockSpec((1, 3, d), lambda i, j: (layer, 0, 0)),
            pl.BlockSpec((1, k, d), lambda i, j: (wl, 0, 0), pipeline_mode=pl.Buffered(1)),
        ],
        out_specs=pl.BlockSpec((1, tm, d), row),
        out_shape=jax.ShapeDtypeStruct((b, s, d), F32),
        compiler_params=pltpu.CompilerParams(
            dimension_semantics=("arbitrary", "arbitrary"),
            vmem_limit_bytes=_vmem_limit(k * d * 2 + 8 * tm * d * 4 + 8 * 1024 * 1024)),
        name="mixer_out_proj",
    )(o, x, mods, post_norm, w)


def _online_softmax_step(s_ref, p_ref, tile_max, vt, m_ref, l_ref, acc_ref, slot, interleave=()):
    tk = s_ref.shape[0]
    strips = list(range(0, tk, SOFTMAX_STRIP))
    due = {}
    for j, run in enumerate(interleave):
        due.setdefault((j * len(strips)) // len(interleave), []).append(run)
    m8 = functools.reduce(jnp.maximum, tile_max)
    m_prev = m_ref[slot]
    m_new = jnp.maximum(m_prev, jnp.max(m8, axis=0, keepdims=True))
    alpha = jnp.exp2(m_prev - m_new)
    l8 = None
    for n, r in enumerate(strips):
        for run in due.get(n, ()):
            run()
        p = jnp.exp2(s_ref[r:r + SOFTMAX_STRIP, :] - m_new)
        p_ref[r:r + SOFTMAX_STRIP, :] = p.astype(BF16)
        if l_ref is not None:
            f = _fold_rows(p, jnp.sum)
            l8 = f if l8 is None else l8 + f
    if l_ref is not None:
        l_ref[slot] = alpha * l_ref[slot] + l8
    acc_ref[slot] = alpha * acc_ref[slot] + _dot(vt, p_ref[...])
    m_ref[slot] = m_new


def _diff_attn_kernel(lam_ref, sub_ref, q1_ref, q2_ref, k1_ref, k2_ref, vt_ref, o_ref,
                      m_ref, l_ref, acc_ref, s0_ref, s1_ref, p0_ref, p1_ref, *, tq, tk, lam_init,
                      pairs):
    s_bufs = (s0_ref, s1_ref)
    p_bufs = (p0_ref, p1_ref)
    i = pl.program_id(2)
    lane = lax.broadcasted_iota(jnp.int32, (tq, LANES), 1)
    low = lane < HEAD_DIM
    v_dim = 2 * HEAD_DIM
    n_slots = 4 * pairs

    def slot_parts(slot):
        return slot // 4, (slot % 4) // 2, slot % 2

    qms = []
    for slot in range(n_slots):
        pair, comp, half = slot_parts(slot)
        q = (q1_ref, q2_ref)[comp][0, :, pair * LANES:(pair + 1) * LANES]
        zero = jnp.zeros_like(q)
        qms.append(jnp.where(low, q, zero) if half == 0 else jnp.where(low, zero, q))
    k_refs = (k1_ref, k2_ref)

    m_ref[...] = jnp.full(m_ref.shape, MASKED, F32)
    l_ref[...] = jnp.zeros(l_ref.shape, F32)
    acc_ref[...] = jnp.zeros(acc_ref.shape, F32)

    def step(kc, on_diagonal):
        off = pl.multiple_of(kc * tk, tk)
        ks = [k_r[0, pl.ds(off, tk), :] for k_r in k_refs]
        vt = vt_ref[0, kc]
        if on_diagonal:
            key_idx = kc * tk + lax.broadcasted_iota(jnp.int32, (tk, tq), 0)
            qry_idx = i * tq + lax.broadcasted_iota(jnp.int32, (tk, tq), 1)
            keep = key_idx <= qry_idx
        def score_pieces(slot):
            pair, comp, _ = slot_parts(slot)
            buf = s_bufs[slot % 2]
            maxes = []

            def piece(r):
                def run():
                    k = ks[comp][r:r + SCORE_PIECE, pair * LANES:(pair + 1) * LANES]
                    s = _dot_nt(k, qms[slot])
                    if on_diagonal:
                        s = jnp.where(keep[r:r + SCORE_PIECE, :], s, MASKED)
                    buf[r:r + SCORE_PIECE, :] = s
                    maxes.append(_fold_rows(s, jnp.max))
                return run

            return [piece(r) for r in range(0, tk, SCORE_PIECE)], maxes

        pieces, maxes = score_pieces(0)
        for run in pieces:
            run()
        for slot in range(n_slots):
            nxt, nxt_maxes = score_pieces(slot + 1) if slot + 1 < n_slots else ((), None)
            pair, _, half = slot_parts(slot)
            head = 2 * pair + half
            _online_softmax_step(s_bufs[slot % 2], p_bufs[slot % 2], maxes,
                                 vt[head * v_dim:(head + 1) * v_dim, :], m_ref, l_ref, acc_ref, slot,
                                 nxt)
            maxes = nxt_maxes

    n_full = (i * tq) // tk

    def full_body(kc, carry):
        step(kc, False)
        return carry

    lax.fori_loop(0, n_full, full_body, 0)
    for d in range(tq // tk):
        step(n_full + d, True)

    lam = lam_ref[0]
    lam_val = (jnp.exp(jnp.sum(lam[0:1] * lam[1:2], axis=1, keepdims=True))
               - jnp.exp(jnp.sum(lam[2:3] * lam[3:4], axis=1, keepdims=True)) + lam_init)
    for pair in range(pairs):
        for half in range(2):
            s1 = 4 * pair + half
            s2 = s1 + 2
            o1 = acc_ref[s1] / jnp.sum(l_ref[s1], axis=0, keepdims=True)
            o2 = acc_ref[s2] / jnp.sum(l_ref[s2], axis=0, keepdims=True)
            o = (o1 - lam_val * o2).T
            o = _rms(o) * sub_ref[0] * (1.0 - lam_init)
            head = 2 * pair + half
            o_ref[0, :, head * v_dim:(head + 1) * v_dim] = o.astype(o_ref.dtype)


def _diff_attn_call(q, k, vt, lam, subln, wl, lam_init):
    b, s, qk = q.shape
    n_pairs = qk // (2 * LANES)
    pairs = 2 if n_pairs % 2 == 0 else 1
    groups = n_pairs // pairs
    nk, _, tk = vt.shape[1:]
    tq = tk
    v_dim = 2 * HEAD_DIM
    n_slots = 4 * pairs
    kern = functools.partial(_diff_attn_kernel, tq=tq, tk=tk, lam_init=lam_init, pairs=pairs)
    return pl.pallas_call(
        kern,
        grid=(b, groups, s // tq),
        in_specs=[
            pl.BlockSpec((1, 4, HEAD_DIM), lambda bi, g, i: (wl, 0, 0)),
            pl.BlockSpec((1, 1, v_dim), lambda bi, g, i: (wl, 0, 0)),
            pl.BlockSpec((1, tq, pairs * LANES), lambda bi, g, i: (bi, i, g)),
            pl.BlockSpec((1, tq, pairs * LANES), lambda bi, g, i: (bi, i, groups + g)),
            pl.BlockSpec((1, s, pairs * LANES), lambda bi, g, i: (bi, 0, g)),
            pl.BlockSpec((1, s, pairs * LANES), lambda bi, g, i: (bi, 0, groups + g)),
            pl.BlockSpec((1, nk, 2 * pairs * v_dim, tk), lambda bi, g, i: (bi, 0, g, 0)),
        ],
        out_specs=pl.BlockSpec((1, tq, 2 * pairs * v_dim), lambda bi, g, i: (bi, i, g)),
        out_shape=jax.ShapeDtypeStruct((b, s, n_pairs * 2 * v_dim), BF16),
        scratch_shapes=[
            pltpu.VMEM((n_slots, 1, tq), F32),
            pltpu.VMEM((n_slots, SUBLANES, tq), F32),
            pltpu.VMEM((n_slots, v_dim, tq), F32),
            pltpu.VMEM((tk, tq), F32),
            pltpu.VMEM((tk, tq), F32),
            pltpu.VMEM((tk, tq), BF16),
            pltpu.VMEM((tk, tq), BF16),
        ],
        compiler_params=pltpu.CompilerParams(
            dimension_semantics=("arbitrary", "arbitrary", "arbitrary"),
            vmem_limit_bytes=_vmem_limit(40 * 1024 * 1024)),
        name="diff_attention",
    )(lam, subln.reshape(subln.shape[0], 1, v_dim), q, q, k, k, vt)


def _dsa_attn_kernel(q_ref, qi_ref, wi_ref, kd_ref, kid_ref, vt_ref, o_ref,
                     sc_ref, m_ref, acc_ref, bias_ref, s0_ref, s1_ref, p0_ref, p1_ref, *,
                     tq, tkc, top_k, n_heads, n_idx_heads):
    s_bufs = (s0_ref, s1_ref)
    p_bufs = (p0_ref, p1_ref)
    i = pl.program_id(1)
    nk = (i * tq + tq + tkc - 1) // tkc
    lane = lax.broadcasted_iota(jnp.int32, (tq, LANES), 1)
    low = lane < HEAD_DIM
    key0 = lax.broadcasted_iota(jnp.int32, (tkc, tq), 0)
    qry = i * tq + lax.broadcasted_iota(jnp.int32, (tkc, tq), 1)
    k_f = float(top_k)

    def head_operand(ref, h):
        pair = ref[0, :, (h // 2) * LANES:(h // 2 + 1) * LANES]
        zero = jnp.zeros_like(pair)
        return jnp.where(low, pair, zero) if h % 2 == 0 else jnp.where(low, zero, pair)

    def key_chunk(ref, kc):
        return ref[0, pl.ds(pl.multiple_of(kc * tkc, tkc), tkc), :]

    wt = wi_ref[0, 0]
    qims = [head_operand(qi_ref, h) for h in range(n_idx_heads)]

    def score_body(kc, carry):
        kik = key_chunk(kid_ref, kc)
        score = jnp.zeros((tkc, tq), F32)
        for h in range(n_idx_heads):
            score = score + wt[h:h + 1, :] * jnp.maximum(_dot_nt(kik, qims[h]), 0.0)
        sc_ref[kc] = jnp.where(kc * tkc + key0 <= qry, score, -jnp.inf)
        return carry

    lax.fori_loop(0, nk, score_body, 0)

    def key_to_float(u):
        key = u ^ jnp.int32(INT_MIN)
        bits = jnp.where(key >= 0, key, key ^ jnp.int32(0x7FFFFFFF))
        return lax.bitcast_convert_type(bits, F32)

    def counts(*preds):
        def body(kc, cnts):
            sc = sc_ref[kc]
            return tuple(
                c + jnp.sum(jnp.where(pred(sc), 1.0, 0.0).reshape(tkc // COUNT_ROWS, COUNT_ROWS, tq),
                            axis=0)
                for c, pred in zip(cnts, preds))
        cnts = lax.fori_loop(0, nk, body, (jnp.zeros((COUNT_ROWS, tq), F32),) * len(preds))
        return [jnp.sum(c, axis=0, keepdims=True) for c in cnts]

    few_keys = i * tq + lax.broadcasted_iota(jnp.int32, (1, tq), 1) < top_k - 1
    n_ge0, n_gt0 = counts(lambda sc: sc >= 0.0, lambda sc: sc > 0.0)
    zero_thr = jnp.logical_and(n_gt0 < k_f, n_ge0 >= k_f)
    zero_tied = jnp.logical_and(zero_thr, n_ge0 > k_f)

    def bit_cond(state):
        step, _, settled = state
        return jnp.logical_and(step < 32, jnp.min(settled) < 0.5)

    def bit_body(state):
        step, prefix, settled = state
        cand_u = prefix | lax.shift_left(jnp.int32(1), 31 - step)
        cand = key_to_float(cand_u)
        (cnt,) = counts(lambda sc: sc >= cand)
        settled = jnp.maximum(settled, jnp.where(cnt == k_f, 1.0, 0.0))
        return step + 1, jnp.where(cnt >= k_f, cand_u, prefix), settled

    steps, prefix, settled = lax.while_loop(
        bit_cond, bit_body,
        (jnp.int32(1), jnp.where(n_ge0 >= k_f, jnp.int32(INT_MIN), jnp.int32(0)),
         jnp.where(jnp.logical_or(few_keys, zero_thr), 1.0, 0.0)))
    thr_key = jnp.maximum(prefix ^ jnp.int32(INT_MIN), jnp.int32(KEY_NEG_FLT_MAX))
    thr = lax.bitcast_convert_type(
        jnp.where(thr_key >= 0, thr_key, thr_key ^ jnp.int32(0x7FFFFFFF)), F32)

    any_tie = jnp.logical_or(jnp.max(jnp.where(zero_tied, 1.0, 0.0)) > 0.5,
                             jnp.min(settled) < 0.5)

    @pl.when(any_tie)
    def _():
        (n_gt,) = counts(lambda sc: sc > thr)
        room = k_f - n_gt
        r_i = lax.broadcasted_iota(jnp.int32, (tkc, tkc), 0)
        c_i = lax.broadcasted_iota(jnp.int32, (tkc, tkc), 1)
        prefix_op = jnp.where(r_i >= c_i, 1.0, 0.0).astype(BF16)

        def drop_body(kc, seen):
            sc = sc_ref[kc]
            tied = sc == thr
            rank = seen + _dot(prefix_op, jnp.where(tied, 1.0, 0.0).astype(BF16))
            sc_ref[kc] = jnp.where(jnp.logical_and(tied, rank > room), -jnp.inf, sc)
            return rank[tkc - 1:tkc, :]

        lax.fori_loop(0, nk, drop_body, jnp.zeros((1, tq), F32))

    m_ref[...] = jnp.full(m_ref.shape, MASKED, F32)
    acc_ref[...] = jnp.zeros(acc_ref.shape, F32)
    ones_rows = jnp.ones((DENOM_ROWS, tkc), BF16)

    def attn_body(kc, carry):
        kd = key_chunk(kd_ref, kc)
        vt = jnp.concatenate([vt_ref[0, kc], ones_rows], axis=0)
        bias_ref[...] = jnp.where(sc_ref[kc] >= thr, 0.0, MASKED)

        def score_pieces(h):
            qm = head_operand(q_ref, h)
            buf = s_bufs[h % 2]
            maxes = []

            def piece(r):
                def run():
                    s = bias_ref[r:r + SCORE_PIECE, :] + _dot_nt(kd[r:r + SCORE_PIECE, :], qm)
                    buf[r:r + SCORE_PIECE, :] = s
                    maxes.append(_fold_rows(s, jnp.max))
                return run

            return [piece(r) for r in range(0, tkc, SCORE_PIECE)], maxes

        pieces, maxes = score_pieces(0)
        for run in pieces:
            run()
        for h in range(n_heads):
            nxt, nxt_maxes = score_pieces(h + 1) if h + 1 < n_heads else ((), None)
            _online_softmax_step(s_bufs[h % 2], p_bufs[h % 2], maxes, vt, m_ref, None, acc_ref, h,
                                 nxt)
            maxes = nxt_maxes
        return carry

    lax.fori_loop(0, nk, attn_body, 0)

    for j in range(n_heads // 2):
        halves = [acc_ref[h, :HEAD_DIM, :] / acc_ref[h, HEAD_DIM:HEAD_DIM + 1, :]
                  for h in (2 * j, 2 * j + 1)]
        o_ref[0, :, j * LANES:(j + 1) * LANES] = jnp.concatenate(halves, axis=0).T.astype(o_ref.dtype)


def _dsa_attn_call(q, qi, wit, kd, kid, vt, n_idx_heads):
    b, s, a_q = q.shape
    n_heads = a_q // HEAD_DIM
    top_k = min(TOPK_MAX, s // 4)
    nkc, _, tkc = vt.shape[1:]
    tq = tkc
    per = tkc // tq
    kern = functools.partial(_dsa_attn_kernel, tq=tq, tkc=tkc, top_k=top_k, n_heads=n_heads,
                             n_idx_heads=n_idx_heads)
    rowq = lambda bi, i: (bi, i, 0)
    allk = lambda bi, i: (bi, 0, 0)
    return pl.pallas_call(
        kern,
        grid=(b, s // tq),
        in_specs=[
            pl.BlockSpec((1, tq, a_q), rowq),
            pl.BlockSpec((1, tq, qi.shape[-1]), rowq),
            pl.BlockSpec((1, 1, wit.shape[2], tq), lambda bi, i: (bi, i // per, 0, i % per)),
            pl.BlockSpec((1, s, LANES), allk),
            pl.BlockSpec((1, s, LANES), allk),
            pl.BlockSpec((1, nkc, HEAD_DIM, tkc), lambda bi, i: (bi, 0, 0, 0)),
        ],
        out_specs=pl.BlockSpec((1, tq, a_q), rowq),
        out_shape=jax.ShapeDtypeStruct((b, s, a_q), BF16),
        scratch_shapes=[
            pltpu.VMEM((nkc, tkc, tq), F32),
            pltpu.VMEM((n_heads, 1, tq), F32),
            pltpu.VMEM((n_heads, HEAD_DIM + DENOM_ROWS, tq), F32),
            pltpu.VMEM((tkc, tq), F32),
            pltpu.VMEM((tkc, tq), F32),
            pltpu.VMEM((tkc, tq), F32),
            pltpu.VMEM((tkc, tq), BF16),
            pltpu.VMEM((tkc, tq), BF16),
        ],
        compiler_params=pltpu.CompilerParams(
            dimension_semantics=("arbitrary", "arbitrary"),
            vmem_limit_bytes=_vmem_limit(6 * s * LANES * 2 + tq * s * 4 + 24 * 1024 * 1024)),
        name="dsa_attention",
    )(q, qi, wit, kd, kid, vt)


def _pad_to(w, axis, mult):
    n = w.shape[axis]
    pad = (-n) % mult
    if pad == 0:
        return w
    widths = [(0, 0)] * w.ndim
    widths[axis] = (0, pad)
    return jnp.pad(w, widths)


def _prep_dsa_w_in(w, a_q, n_idx_heads):
    o1 = a_q
    o2 = o1 + HEAD_DIM
    o3 = o2 + HEAD_DIM
    o4 = o3 + n_idx_heads * HEAD_DIM
    o5 = o4 + HEAD_DIM
    wq = w[..., :o1] * ATTN_Q_SCALE
    wk, wv, wqi, wki, wwi = w[..., o1:o2], w[..., o2:o3], w[..., o3:o4], w[..., o4:o5], w[..., o5:]
    rows = jnp.concatenate([wq, wqi, wk, wk, wki, wki], axis=-1).astype(BF16)
    cols = jnp.concatenate([wv, _pad_to(wwi, 2, 2 * SUBLANES)], axis=-1)
    return rows, jnp.swapaxes(cols, 1, 2).astype(BF16)


def kernel(x, c, positions, ada_w, ada_b, pre_norm, post_norm, ffn_w_gate, ffn_w_up, ffn_w_down,
           dsa_w_in, dsa_w_out, diff_w_in, diff_w_out, diff_lambda, diff_subln):
    b, s, d = x.shape
    depth = ada_w.shape[0]

    mods = _mod_call(c, ada_w, ada_b).reshape(depth, b, N_MOD, d)
    cos_f, sin_s = _rope_tables(positions)

    wg = _pad_to(ffn_w_gate, 3, MXU_DIM).astype(BF16)
    wu = _pad_to(ffn_w_up, 3, MXU_DIM).astype(BF16)
    wd = _pad_to(ffn_w_down, 2, MXU_DIM).astype(BF16)

    a_q = dsa_w_out.shape[1]
    n_idx_heads = (dsa_w_in.shape[2] - a_q - 3 * HEAD_DIM) // (HEAD_DIM + 1)
    dsa_in, dsa_in_t = _prep_dsa_w_in(dsa_w_in, a_q, n_idx_heads)
    dsa_out = dsa_w_out.astype(BF16)
    idx_q = n_idx_heads * HEAD_DIM
    wi_scale = n_idx_heads ** -0.5 * HEAD_DIM ** -0.5

    b_out = diff_w_out.shape[1]
    b_qk = (diff_w_in.shape[2] - b_out) // 2
    diff_in = jnp.concatenate(
        [diff_w_in[..., :b_qk] * ATTN_Q_SCALE, diff_w_in[..., b_qk:2 * b_qk]],
        axis=-1).astype(BF16)
    diff_in_t = jnp.swapaxes(diff_w_in[..., 2 * b_qk:], 1, 2).astype(BF16)
    diff_out = diff_w_out.astype(BF16)

    for i in range(depth):
        x = _ffn_call(x, mods, pre_norm, post_norm, wg, wu, wd, i, 0, 0)
        j = i // 2
        if i % 2 == 0:
            q, qi, kd, kid, vt, wit = _proj_call(
                x, mods, pre_norm, cos_f, sin_s, dsa_in, dsa_in_t, i, j,
                row_cols=(a_q, idx_q, LANES, LANES),
                t_rows=(HEAD_DIM, 2 * SUBLANES), t_dtypes=(BF16, F32), t_scales=(None, wi_scale))
            o = _dsa_attn_call(q, qi, wit, kd, kid, vt, n_idx_heads)
            x = _outproj_call(o, x, mods, post_norm, dsa_out, i, j)
        else:
            lam_init = 0.8 - 0.6 * math.exp(-0.3 * i)
            q, k, vt = _proj_call(
                x, mods, pre_norm, cos_f, sin_s, diff_in, diff_in_t, i, j,
                row_cols=(b_qk, b_qk), t_rows=(b_out,), t_dtypes=(BF16,), t_scales=(None,))
            o = _diff_attn_call(q, k, vt, diff_lambda, diff_subln, j, lam_init)
            x = _outproj_call(o, x, mods, post_norm, diff_out, i, j)
        x = _ffn_call(x, mods, pre_norm, post_norm, wg, wu, wd, i, 2, 1)
    return x
```

```python
import functools
import math

import jax
import jax.numpy as jnp
from jax import lax
from jax.experimental import pallas as pl
from jax.experimental.pallas import tpu as pltpu

F32 = jnp.float32
BF16 = jnp.bfloat16

EPS = 1e-6
ROPE_THETA = 500000.0
HEAD_DIM = 64
ROT_DIM = HEAD_DIM // 4
ROT_HALF = ROT_DIM // 2
TOPK_MAX = 256
FFN_HALF = 0.5
N_MOD = 9

LANES = 128
SUBLANES = 8
MXU_DIM = 256
VMEM_BYTES_V7X = 64 * 1024 * 1024

ROW_TILE = 512
DENOM_ROWS = 16
SCORE_PIECE = 128
SOFTMAX_STRIP = 128
COUNT_ROWS = 32

LOG2E = math.log2(math.e)
ATTN_Q_SCALE = HEAD_DIM ** -0.5 * LOG2E

MASKED = -1e30
BOUND_SAFETY = 1.03
DENOM_FLOOR = 2.0 ** -40
INT_MIN = -(2 ** 31)
KEY_NEG_FLT_MAX = -2139095040


def _vmem_limit(nbytes):
    return int(min(nbytes, VMEM_BYTES_V7X - 6 * 1024 * 1024))


def _dot(a, b):
    return jnp.dot(a, b, preferred_element_type=F32)


def _dot_nt(a, b):
    return lax.dot_general(a, b, (((1,), (1,)), ((), ())), preferred_element_type=F32)


def _rms(y):
    return y * lax.rsqrt(jnp.mean(y * y, axis=-1, keepdims=True) + EPS)


def _prenorm_mod(x, gain, shift, scale):
    return (_rms(x) * gain) * (1.0 + scale) + shift


def _silu(g):
    return g / (1.0 + jnp.exp(-g))


def _fold_rows(x, op):
    return op(x.reshape(x.shape[0] // SUBLANES, SUBLANES, x.shape[1]), axis=0)


def _mod_kernel(c_ref, w_ref, b_ref, o_ref):
    cond = _silu(c_ref[...])
    w = w_ref[0]
    c_hi = cond.astype(BF16)
    c_lo = (cond - c_hi.astype(F32)).astype(BF16)
    w_hi = w.astype(BF16)
    w_lo = (w - w_hi.astype(F32)).astype(BF16)
    o_ref[0] = _dot(c_hi, w_hi) + _dot(c_lo, w_hi) + _dot(c_hi, w_lo) + b_ref[0]


def _mod_call(c, ada_w, ada_b):
    depth, d, n = ada_w.shape
    b = c.shape[0]
    tn = 1024 if n % 1024 == 0 else n
    return pl.pallas_call(
        _mod_kernel,
        grid=(depth, n // tn),
        in_specs=[
            pl.BlockSpec((b, d), lambda l, j: (0, 0)),
            pl.BlockSpec((1, d, tn), lambda l, j: (l, 0, j)),
            pl.BlockSpec((1, 1, tn), lambda l, j: (l, 0, j)),
        ],
        out_specs=pl.BlockSpec((1, b, tn), lambda l, j: (l, 0, j)),
        out_shape=jax.ShapeDtypeStruct((depth, b, n), F32),
        compiler_params=pltpu.CompilerParams(
            dimension_semantics=("arbitrary", "arbitrary"),
            vmem_limit_bytes=_vmem_limit(40 * 1024 * 1024)),
        name="adaln_mod",
    )(c, ada_w, ada_b.reshape(depth, 1, n))


def _rope_kernel(ang_ref, cos_ref, sin_ref):
    a = ang_ref[0]
    lane = lax.broadcasted_iota(jnp.int32, a.shape, 1) & (HEAD_DIM - 1)
    rot = lane < ROT_DIM
    cos_ref[0] = jnp.where(rot, jnp.cos(a), 1.0)
    sin_ref[0] = jnp.where(rot, jnp.sin(a), 0.0)


def _rope_tables(positions):
    b, s = positions.shape
    inv = ROPE_THETA ** (-jnp.arange(0, ROT_DIM, 2, dtype=F32) / ROT_DIM)
    head = jnp.concatenate([-inv, inv, jnp.zeros((HEAD_DIM - ROT_DIM,), F32)])
    inv_lanes = jnp.tile(head, LANES // HEAD_DIM)
    ang = positions.astype(F32)[..., None] * inv_lanes
    tm = ROW_TILE if s % ROW_TILE == 0 else s
    spec = pl.BlockSpec((1, tm, LANES), lambda i, j: (i, j, 0))
    return pl.pallas_call(
        _rope_kernel,
        grid=(b, s // tm),
        in_specs=[spec],
        out_specs=[spec, spec],
        out_shape=[jax.ShapeDtypeStruct((b, s, LANES), F32)] * 2,
        compiler_params=pltpu.CompilerParams(dimension_semantics=("arbitrary", "arbitrary")),
        name="rope_tables",
    )(ang)


def _apply_rope(y, cos_f, sin_s, first_half):
    partner = jnp.where(first_half, pltpu.roll(y, LANES - ROT_HALF, 1), pltpu.roll(y, ROT_HALF, 1))
    return y * cos_f + partner * sin_s


def _ffn_kernel(x_ref, mod_ref, pre_ref, post_ref, wg_ref, wu_ref, wd_ref, o_ref, *, sub, f_chunks):
    x = x_ref[0]
    shift = mod_ref[0, 0, 3 * sub:3 * sub + 1, :]
    scale = mod_ref[0, 0, 3 * sub + 1:3 * sub + 2, :]
    gate = mod_ref[0, 0, 3 * sub + 2:3 * sub + 3, :]
    h = _prenorm_mod(x, pre_ref[0, sub:sub + 1, :], shift, scale).astype(BF16)
    y = None
    for f0, f1 in f_chunks:
        g = _dot(h, wg_ref[0, 0, :, f0:f1])
        u = _dot(h, wu_ref[0, 0, :, f0:f1])
        a = (_silu(g) * u).astype(BF16)
        part = _dot(a, wd_ref[0, 0, f0:f1, :])
        y = part if y is None else y + part
    o_ref[0] = x + (FFN_HALF * gate) * (_rms(y) * post_ref[0, sub:sub + 1, :])


def _ffn_call(x, mods, pre_norm, post_norm, wg, wu, wd, layer, sub, which):
    b, s, d = x.shape
    f = wg.shape[-1]
    tm = ROW_TILE if s % ROW_TILE == 0 else s
    step = 4 * MXU_DIM
    f_chunks = tuple((f0, min(f0 + step, f)) for f0 in range(0, f, step))
    kern = functools.partial(_ffn_kernel, sub=sub, f_chunks=f_chunks)
    once = pl.Buffered(1)
    return pl.pallas_call(
        kern,
        grid=(b, s // tm),
        in_specs=[
            pl.BlockSpec((1, tm, d), lambda i, j: (i, j, 0)),
            pl.BlockSpec((1, 1, N_MOD, d), lambda i, j: (layer, i, 0, 0)),
            pl.BlockSpec((1, 3, d), lambda i, j: (layer, 0, 0)),
            pl.BlockSpec((1, 3, d), lambda i, j: (layer, 0, 0)),
            pl.BlockSpec((1, 1, d, f), lambda i, j: (layer, which, 0, 0), pipeline_mode=once),
            pl.BlockSpec((1, 1, d, f), lambda i, j: (layer, which, 0, 0), pipeline_mode=once),
            pl.BlockSpec((1, 1, f, d), lambda i, j: (layer, which, 0, 0), pipeline_mode=once),
        ],
        out_specs=pl.BlockSpec((1, tm, d), lambda i, j: (i, j, 0)),
        out_shape=jax.ShapeDtypeStruct((b, s, d), F32),
        compiler_params=pltpu.CompilerParams(
            dimension_semantics=("arbitrary", "arbitrary"),
            vmem_limit_bytes=_vmem_limit(3 * d * f * 2 + 4 * tm * d * 4 + 6 * tm * step * 4
                                         + 8 * 1024 * 1024)),
        name="swiglu_half_step",
    )(x, mods, pre_norm, post_norm, wg, wu, wd)


def _proj_kernel(x_ref, mod_ref, pre_ref, cos_ref, sin_ref, w_ref, wt_ref, *out_refs,
                 n_row_outs, norm_of, t_scales):
    x = x_ref[0]
    shift = mod_ref[0, 0, 3:4, :]
    scale = mod_ref[0, 0, 4:5, :]
    h = _prenorm_mod(x, pre_ref[0, 1:2, :], shift, scale).astype(BF16)
    cos_f = cos_ref[0]
    sin_s = sin_ref[0]
    lane = lax.broadcasted_iota(jnp.int32, cos_f.shape, 1) & (HEAD_DIM - 1)
    first_half = lane < ROT_HALF
    norm_refs = out_refs[len(out_refs) - len(norm_of):]
    col = 0
    for n, out_ref in enumerate(out_refs[:n_row_outs]):
        ncols = out_ref.shape[-1]
        y = _dot(h, w_ref[0, :, col:col + ncols])
        squares = []
        for j in range(ncols // LANES):
            blk = _apply_rope(y[:, j * LANES:(j + 1) * LANES], cos_f, sin_s, first_half)
            out_ref[0, :, j * LANES:(j + 1) * LANES] = blk.astype(out_ref.dtype)
            if n in norm_of:
                squares.append((blk * blk).astype(BF16))
        if n in norm_of:
            n_ref = norm_refs[norm_of.index(n)]
            rows = n_ref.shape[-2]
            head_of_col = lax.broadcasted_iota(jnp.int32, (rows, ncols), 1) // HEAD_DIM
            sel = jnp.where(head_of_col == lax.broadcasted_iota(jnp.int32, (rows, ncols), 0),
                            1.0, 0.0).astype(BF16)
            n_ref[0, 0] = _dot_nt(sel, jnp.concatenate(squares, axis=1))
        col += ncols
    r = 0
    for out_ref, t_scale in zip(out_refs[n_row_outs:len(out_refs) - len(norm_of)], t_scales):
        nrows = out_ref.shape[-2]
        yt = _dot_nt(wt_ref[0, r:r + nrows, :], h)
        if t_scale is not None:
            yt = yt * t_scale
        out_ref[0, 0] = yt.astype(out_ref.dtype)
        r += nrows


def _proj_call(x, mods, pre_norm, cos_f, sin_s, w, wt, layer, wl, row_cols, t_rows, t_dtypes,
               t_scales, norm_of):
    b, s, d = x.shape
    n = w.shape[-1]
    nt = wt.shape[-2]
    tm = ROW_TILE if s % ROW_TILE == 0 else s
    kern = functools.partial(_proj_kernel, n_row_outs=len(row_cols), norm_of=norm_of,
                             t_scales=t_scales)
    row = lambda i, j: (i, j, 0)
    once = pl.Buffered(1)
    t_rows = tuple(t_rows) + tuple(max(SUBLANES, row_cols[n] // HEAD_DIM) for n in norm_of)
    t_dtypes = tuple(t_dtypes) + (F32,) * len(norm_of)
    return pl.pallas_call(
        kern,
        grid=(b, s // tm),
        in_specs=[
            pl.BlockSpec((1, tm, d), row),
            pl.BlockSpec((1, 1, N_MOD, d), lambda i, j: (layer, i, 0, 0)),
            pl.BlockSpec((1, 3, d), lambda i, j: (layer, 0, 0)),
            pl.BlockSpec((1, tm, LANES), row),
            pl.BlockSpec((1, tm, LANES), row),
            pl.BlockSpec((1, d, n), lambda i, j: (wl, 0, 0), pipeline_mode=once),
            pl.BlockSpec((1, nt, d), lambda i, j: (wl, 0, 0), pipeline_mode=once),
        ],
        out_specs=([pl.BlockSpec((1, tm, nc), row) for nc in row_cols]
                   + [pl.BlockSpec((1, 1, nr, tm), lambda i, j: (i, j, 0, 0)) for nr in t_rows]),
        out_shape=([jax.ShapeDtypeStruct((b, s, nc), BF16) for nc in row_cols]
                   + [jax.ShapeDtypeStruct((b, s // tm, nr, tm), dt)
                      for nr, dt in zip(t_rows, t_dtypes)]),
        compiler_params=pltpu.CompilerParams(
            dimension_semantics=("arbitrary", "arbitrary"),
            vmem_limit_bytes=_vmem_limit(d * (n + nt) * 2 + 2 * tm * d * 4 + 6 * tm * (n + nt) * 2
                                         + 6 * tm * 1024 * 4 + 8 * 1024 * 1024)),
        name="mixer_in_proj",
    )(x, mods, pre_norm, cos_f, sin_s, w, wt)


def _outproj_kernel(o_ref, x_ref, mod_ref, post_ref, w_ref, out_ref):
    y = _dot(o_ref[0], w_ref[0])
    gate = mod_ref[0, 0, 5:6, :]
    out_ref[0] = x_ref[0] + gate * (_rms(y) * post_ref[0, 1:2, :])


def _outproj_call(o, x, mods, post_norm, w, layer, wl):
    b, s, d = x.shape
    k = o.shape[-1]
    tm = ROW_TILE if s % ROW_TILE == 0 else s
    row = lambda i, j: (i, j, 0)
    return pl.pallas_call(
        _outproj_kernel,
        grid=(b, s // tm),
        in_specs=[
            pl.BlockSpec((1, tm, k), row),
            pl.BlockSpec((1, tm, d), row),
            pl.BlockSpec((1, 1, N_MOD, d), lambda i, j: (layer, i, 0, 0)),
            pl.BlockSpec((1, 3, d), lambda i, j: (layer, 0, 0)),
            pl.BlockSpec((1, k, d), lambda i, j: (wl, 0, 0), pipeline_mode=pl.Buffered(1)),
        ],
        out_specs=pl.BlockSpec((1, tm, d), row),
        out_shape=jax.ShapeDtypeStruct((b, s, d), F32),
        compiler_params=pltpu.CompilerParams(
            dimension_semantics=("arbitrary", "arbitrary"),
            vmem_limit_bytes=_vmem_limit(k * d * 2 + 8 * tm * d * 4 + 8 * 1024 * 1024)),
        name="mixer_out_proj",
    )(o, x, mods, post_norm, w)


def _online_softmax_step(s_ref, p_ref, tile_max, vt, m_ref, l_ref, acc_ref, slot, interleave=()):
    tk = s_ref.shape[0]
    strips = list(range(0, tk, SOFTMAX_STRIP))
    due = {}
    for j, run in enumerate(interleave):
        due.setdefault((j * len(strips)) // len(interleave), []).append(run)
    m8 = functools.reduce(jnp.maximum, tile_max)
    m_prev = m_ref[slot]
    m_new = jnp.maximum(m_prev, jnp.max(m8, axis=0, keepdims=True))
    alpha = jnp.exp2(m_prev - m_new)
    l8 = None
    for n, r in enumerate(strips):
        for run in due.get(n, ()):
            run()
        p = jnp.exp2(s_ref[r:r + SOFTMAX_STRIP, :] - m_new)
        p_ref[r:r + SOFTMAX_STRIP, :] = p.astype(BF16)
        if l_ref is not None:
            f = _fold_rows(p, jnp.sum)
            l8 = f if l8 is None else l8 + f
    if l_ref is not None:
        l_ref[slot] = alpha * l_ref[slot] + l8
    acc_ref[slot] = alpha * acc_ref[slot] + _dot(vt, p_ref[...])
    m_ref[slot] = m_new


def _diff_attn_kernel(lam_ref, sub_ref, q1_ref, q2_ref, qn1_ref, qn2_ref, k1_ref, k2_ref, kn1_ref,
                      kn2_ref, vt_ref, o_ref, m_ref, acc_ref, s0_ref, s1_ref, p0_ref, p1_ref, *,
                      tq, tk, lam_init, pairs):
    s_bufs = (s0_ref, s1_ref)
    p_bufs = (p0_ref, p1_ref)
    i = pl.program_id(2)
    lane = lax.broadcasted_iota(jnp.int32, (tq, LANES), 1)
    low = lane < HEAD_DIM
    v_dim = 2 * HEAD_DIM
    n_slots = 4 * pairs

    def slot_parts(slot):
        return slot // 4, (slot % 4) // 2, slot % 2

    qms = []
    bounds = []
    for slot in range(n_slots):
        pair, comp, half = slot_parts(slot)
        q = (q1_ref, q2_ref)[comp][0, :, pair * LANES:(pair + 1) * LANES]
        zero = jnp.zeros_like(q)
        qms.append(jnp.where(low, q, zero) if half == 0 else jnp.where(low, zero, q))
        j = 2 * pair + half
        k2_max = jnp.max((kn1_ref, kn2_ref)[comp][0, :, 0, j:j + 1, :])
        q2 = (qn1_ref, qn2_ref)[comp][0, 0, 0, j:j + 1, :]
        bounds.append(jnp.sqrt(q2 * k2_max) * BOUND_SAFETY)
    k_refs = (k1_ref, k2_ref)
    ones_rows = jnp.ones((DENOM_ROWS, tk), BF16)

    def chunk_operands(kc):
        off = pl.multiple_of(kc * tk, tk)
        ks = [k_r[0, pl.ds(off, tk), :] for k_r in k_refs]
        vt = vt_ref[0, kc]
        vts = [jnp.concatenate([vt[h * v_dim:(h + 1) * v_dim, :], ones_rows], axis=0)
               for h in range(2 * pairs)]
        return ks, vts

    def masked_scores(ks, slot, r, keep):
        pair, comp, _ = slot_parts(slot)
        k = ks[comp][r:r + SCORE_PIECE, pair * LANES:(pair + 1) * LANES]
        s = _dot_nt(k, qms[slot])
        if keep is not None:
            s = jnp.where(keep[r:r + SCORE_PIECE, :], s, MASKED)
        return s

    def causal_keep(kc):
        key_idx = kc * tk + lax.broadcasted_iota(jnp.int32, (tk, tq), 0)
        qry_idx = i * tq + lax.broadcasted_iota(jnp.int32, (tk, tq), 1)
        return key_idx <= qry_idx

    def sweep(step):
        n_full = (i * tq) // tk

        def full_body(kc, carry):
            step(kc, None)
            return carry

        lax.fori_loop(0, n_full, full_body, 0)
        for d in range(tq // tk):
            step(n_full + d, causal_keep(n_full + d))

    def fast_step(kc, keep):
        ks, vts = chunk_operands(kc)

        def prob_pieces(slot):
            buf = p_bufs[slot % 2]

            def piece(r):
                def run():
                    s = masked_scores(ks, slot, r, keep)
                    buf[r:r + SCORE_PIECE, :] = jnp.exp2(s - bounds[slot]).astype(BF16)
                return run

            return [piece(r) for r in range(0, tk, SCORE_PIECE)]

        for run in prob_pieces(0):
            run()
        for slot in range(n_slots):
            nxt = prob_pieces(slot + 1) if slot + 1 < n_slots else []
            for run in nxt[:1]:
                run()
            pair, _, half = slot_parts(slot)
            acc_ref[slot] = acc_ref[slot] + _dot(vts[2 * pair + half], p_bufs[slot % 2][...])
            for run in nxt[1:]:
                run()

    acc_ref[...] = jnp.zeros(acc_ref.shape, F32)
    sweep(fast_step)
    denom_min = functools.reduce(
        jnp.minimum, [acc_ref[slot, v_dim:v_dim + 1, :] for slot in range(n_slots)])

    @pl.when(jnp.logical_not(jnp.min(denom_min) > DENOM_FLOOR))
    def _():
        def exact_step(kc, keep):
            ks, vts = chunk_operands(kc)

            def score_pieces(slot):
                buf = s_bufs[slot % 2]
                maxes = []

                def piece(r):
                    def run():
                        s = masked_scores(ks, slot, r, keep)
                        buf[r:r + SCORE_PIECE, :] = s
                        maxes.append(_fold_rows(s, jnp.max))
                    return run

                return [piece(r) for r in range(0, tk, SCORE_PIECE)], maxes

            pieces, maxes = score_pieces(0)
            for run in pieces:
                run()
            for slot in range(n_slots):
                nxt, nxt_maxes = score_pieces(slot + 1) if slot + 1 < n_slots else ((), None)
                pair, _, half = slot_parts(slot)
                _online_softmax_step(s_bufs[slot % 2], p_bufs[slot % 2], maxes, vts[2 * pair + half],
                                     m_ref, None, acc_ref, slot, nxt)
                maxes = nxt_maxes

        m_ref[...] = jnp.full(m_ref.shape, MASKED, F32)
        acc_ref[...] = jnp.zeros(acc_ref.shape, F32)
        sweep(exact_step)

    lam = lam_ref[0]
    lam_val = (jnp.exp(jnp.sum(lam[0:1] * lam[1:2], axis=1, keepdims=True))
               - jnp.exp(jnp.sum(lam[2:3] * lam[3:4], axis=1, keepdims=True)) + lam_init)
    for pair in range(pairs):
        for half in range(2):
            s1 = 4 * pair + half
            s2 = s1 + 2
            o1 = acc_ref[s1, :v_dim, :] / acc_ref[s1, v_dim:v_dim + 1, :]
            o2 = acc_ref[s2, :v_dim, :] / acc_ref[s2, v_dim:v_dim + 1, :]
            o = (o1 - lam_val * o2).T
            o = _rms(o) * sub_ref[0] * (1.0 - lam_init)
            head = 2 * pair + half
            o_ref[0, :, head * v_dim:(head + 1) * v_dim] = o.astype(o_ref.dtype)


def _diff_attn_call(q, k, vt, qn2, kn2, lam, subln, wl, lam_init):
    b, s, qk = q.shape
    n_pairs = qk // (2 * LANES)
    pairs = 2 if n_pairs % 2 == 0 else 1
    groups = n_pairs // pairs
    nk, _, tk = vt.shape[1:]
    tq = tk
    v_dim = 2 * HEAD_DIM
    n_slots = 4 * pairs
    hps = 2 * pairs
    qn2 = qn2.reshape(b, s // tq, 2 * groups, hps, tq)
    kn2 = kn2.reshape(b, nk, 2 * groups, hps, tk)
    kern = functools.partial(_diff_attn_kernel, tq=tq, tk=tk, lam_init=lam_init, pairs=pairs)
    return pl.pallas_call(
        kern,
        grid=(b, groups, s // tq),
        in_specs=[
            pl.BlockSpec((1, 4, HEAD_DIM), lambda bi, g, i: (wl, 0, 0)),
            pl.BlockSpec((1, 1, v_dim), lambda bi, g, i: (wl, 0, 0)),
            pl.BlockSpec((1, tq, pairs * LANES), lambda bi, g, i: (bi, i, g)),
            pl.BlockSpec((1, tq, pairs * LANES), lambda bi, g, i: (bi, i, groups + g)),
            pl.BlockSpec((1, 1, 1, hps, tq), lambda bi, g, i: (bi, i, g, 0, 0)),
            pl.BlockSpec((1, 1, 1, hps, tq), lambda bi, g, i: (bi, i, groups + g, 0, 0)),
            pl.BlockSpec((1, s, pairs * LANES), lambda bi, g, i: (bi, 0, g)),
            pl.BlockSpec((1, s, pairs * LANES), lambda bi, g, i: (bi, 0, groups + g)),
            pl.BlockSpec((1, nk, 1, hps, tk), lambda bi, g, i: (bi, 0, g, 0, 0)),
            pl.BlockSpec((1, nk, 1, hps, tk), lambda bi, g, i: (bi, 0, groups + g, 0, 0)),
            pl.BlockSpec((1, nk, 2 * pairs * v_dim, tk), lambda bi, g, i: (bi, 0, g, 0)),
        ],
        out_specs=pl.BlockSpec((1, tq, 2 * pairs * v_dim), lambda bi, g, i: (bi, i, g)),
        out_shape=jax.ShapeDtypeStruct((b, s, n_pairs * 2 * v_dim), BF16),
        scratch_shapes=[
            pltpu.VMEM((n_slots, 1, tq), F32),
            pltpu.VMEM((n_slots, v_dim + DENOM_ROWS, tq), F32),
            pltpu.VMEM((tk, tq), F32),
            pltpu.VMEM((tk, tq), F32),
            pltpu.VMEM((tk, tq), BF16),
            pltpu.VMEM((tk, tq), BF16),
        ],
        compiler_params=pltpu.CompilerParams(
            dimension_semantics=("arbitrary", "arbitrary", "arbitrary"),
            vmem_limit_bytes=_vmem_limit(40 * 1024 * 1024)),
        name="diff_attention",
    )(lam, subln.reshape(subln.shape[0], 1, v_dim), q, q, qn2, qn2, k, k, kn2, kn2, vt)


def _dsa_attn_kernel(q_ref, qi_ref, wi_ref, qn2_ref, kd_ref, kid_ref, kn2_ref, vt_ref, o_ref,
                     sc_ref, m_ref, acc_ref, bias_ref, s0_ref, s1_ref, p0_ref, p1_ref, *,
                     tq, tkc, top_k, n_heads, n_idx_heads):
    s_bufs = (s0_ref, s1_ref)
    p_bufs = (p0_ref, p1_ref)
    i = pl.program_id(1)
    nk = (i * tq + tq + tkc - 1) // tkc
    lane = lax.broadcasted_iota(jnp.int32, (tq, LANES), 1)
    low = lane < HEAD_DIM
    key0 = lax.broadcasted_iota(jnp.int32, (tkc, tq), 0)
    qry = i * tq + lax.broadcasted_iota(jnp.int32, (tkc, tq), 1)
    k_f = float(top_k)

    def head_operand(ref, h):
        pair = ref[0, :, (h // 2) * LANES:(h // 2 + 1) * LANES]
        zero = jnp.zeros_like(pair)
        return jnp.where(low, pair, zero) if h % 2 == 0 else jnp.where(low, zero, pair)

    def key_chunk(ref, kc):
        return ref[0, pl.ds(pl.multiple_of(kc * tkc, tkc), tkc), :]

    wt = wi_ref[0, 0]
    qims = [head_operand(qi_ref, h) for h in range(n_idx_heads)]

    def score_body(kc, carry):
        kik = key_chunk(kid_ref, kc)
        score = jnp.zeros((tkc, tq), F32)
        for h in range(n_idx_heads):
            score = score + wt[h:h + 1, :] * jnp.maximum(_dot_nt(kik, qims[h]), 0.0)
        sc_ref[kc] = jnp.where(kc * tkc + key0 <= qry, score, -jnp.inf)
        return carry

    lax.fori_loop(0, nk, score_body, 0)

    def key_to_float(u):
        key = u ^ jnp.int32(INT_MIN)
        bits = jnp.where(key >= 0, key, key ^ jnp.int32(0x7FFFFFFF))
        return lax.bitcast_convert_type(bits, F32)

    def counts(*preds):
        def body(kc, cnts):
            sc = sc_ref[kc]
            return tuple(
                c + jnp.sum(jnp.where(pred(sc), 1.0, 0.0).reshape(tkc // COUNT_ROWS, COUNT_ROWS, tq),
                            axis=0)
                for c, pred in zip(cnts, preds))
        cnts = lax.fori_loop(0, nk, body, (jnp.zeros((COUNT_ROWS, tq), F32),) * len(preds))
        return [jnp.sum(c, axis=0, keepdims=True) for c in cnts]

    few_keys = i * tq + lax.broadcasted_iota(jnp.int32, (1, tq), 1) < top_k - 1
    n_ge0, n_gt0 = counts(lambda sc: sc >= 0.0, lambda sc: sc > 0.0)
    zero_thr = jnp.logical_and(n_gt0 < k_f, n_ge0 >= k_f)
    zero_tied = jnp.logical_and(zero_thr, n_ge0 > k_f)

    def bit_cond(state):
        step, _, settled = state
        return jnp.logical_and(step < 32, jnp.min(settled) < 0.5)

    def bit_body(state):
        step, prefix, settled = state
        cand_u = prefix | lax.shift_left(jnp.int32(1), 31 - step)
        cand = key_to_float(cand_u)
        (cnt,) = counts(lambda sc: sc >= cand)
        settled = jnp.maximum(settled, jnp.where(cnt == k_f, 1.0, 0.0))
        return step + 1, jnp.where(cnt >= k_f, cand_u, prefix), settled

    steps, prefix, settled = lax.while_loop(
        bit_cond, bit_body,
        (jnp.int32(1), jnp.where(n_ge0 >= k_f, jnp.int32(INT_MIN), jnp.int32(0)),
         jnp.where(jnp.logical_or(few_keys, zero_thr), 1.0, 0.0)))
    thr_key = jnp.maximum(prefix ^ jnp.int32(INT_MIN), jnp.int32(KEY_NEG_FLT_MAX))
    thr = lax.bitcast_convert_type(
        jnp.where(thr_key >= 0, thr_key, thr_key ^ jnp.int32(0x7FFFFFFF)), F32)

    any_tie = jnp.logical_or(jnp.max(jnp.where(zero_tied, 1.0, 0.0)) > 0.5,
                             jnp.min(settled) < 0.5)

    @pl.when(any_tie)
    def _():
        (n_gt,) = counts(lambda sc: sc > thr)
        room = k_f - n_gt
        r_i = lax.broadcasted_iota(jnp.int32, (tkc, tkc), 0)
        c_i = lax.broadcasted_iota(jnp.int32, (tkc, tkc), 1)
        prefix_op = jnp.where(r_i >= c_i, 1.0, 0.0).astype(BF16)

        def drop_body(kc, seen):
            sc = sc_ref[kc]
            tied = sc == thr
            rank = seen + _dot(prefix_op, jnp.where(tied, 1.0, 0.0).astype(BF16))
            sc_ref[kc] = jnp.where(jnp.logical_and(tied, rank > room), -jnp.inf, sc)
            return rank[tkc - 1:tkc, :]

        lax.fori_loop(0, nk, drop_body, jnp.zeros((1, tq), F32))

    ones_rows = jnp.ones((DENOM_ROWS, tkc), BF16)

    def chunk_operands(kc):
        kd = key_chunk(kd_ref, kc)
        vt = jnp.concatenate([vt_ref[0, kc], ones_rows], axis=0)
        bias_ref[...] = jnp.where(sc_ref[kc] >= thr, 0.0, MASKED)
        return kd, vt

    k2_max = jnp.max(kn2_ref[0, :, 0:1, :])
    bounds = [jnp.sqrt(qn2_ref[0, 0, h:h + 1, :] * k2_max) * BOUND_SAFETY for h in range(n_heads)]
    acc_ref[...] = jnp.zeros(acc_ref.shape, F32)

    def fast_body(kc, carry):
        kd, vt = chunk_operands(kc)

        def prob_pieces(h):
            qm = head_operand(q_ref, h)
            buf = p_bufs[h % 2]

            def piece(r):
                def run():
                    s = bias_ref[r:r + SCORE_PIECE, :] + _dot_nt(kd[r:r + SCORE_PIECE, :], qm)
                    buf[r:r + SCORE_PIECE, :] = jnp.exp2(s - bounds[h]).astype(BF16)
                return run

            return [piece(r) for r in range(0, tkc, SCORE_PIECE)]

        for run in prob_pieces(0):
            run()
        for h in range(n_heads):
            nxt = prob_pieces(h + 1) if h + 1 < n_heads else []
            for run in nxt[:1]:
                run()
            acc_ref[h] = acc_ref[h] + _dot(vt, p_bufs[h % 2][...])
            for run in nxt[1:]:
                run()
        return carry

    lax.fori_loop(0, nk, fast_body, 0)
    denom_min = functools.reduce(
        jnp.minimum, [acc_ref[h, HEAD_DIM:HEAD_DIM + 1, :] for h in range(n_heads)])

    @pl.when(jnp.logical_not(jnp.min(denom_min) > DENOM_FLOOR))
    def _():
        _dsa_exact_attention(q_ref, kd_ref, vt_ref, sc_ref, m_ref, acc_ref, bias_ref, s_bufs, p_bufs,
                             thr, nk, tkc, n_heads, head_operand, key_chunk, ones_rows)

    for j in range(n_heads // 2):
        halves = [acc_ref[h, :HEAD_DIM, :] / acc_ref[h, HEAD_DIM:HEAD_DIM + 1, :]
                  for h in (2 * j, 2 * j + 1)]
        o_ref[0, :, j * LANES:(j + 1) * LANES] = jnp.concatenate(halves, axis=0).T.astype(o_ref.dtype)


def _dsa_exact_attention(q_ref, kd_ref, vt_ref, sc_ref, m_ref, acc_ref, bias_ref, s_bufs, p_bufs,
                         thr, nk, tkc, n_heads, head_operand, key_chunk, ones_rows):
    m_ref[...] = jnp.full(m_ref.shape, MASKED, F32)
    acc_ref[...] = jnp.zeros(acc_ref.shape, F32)

    def attn_body(kc, carry):
        kd = key_chunk(kd_ref, kc)
        vt = jnp.concatenate([vt_ref[0, kc], ones_rows], axis=0)
        bias_ref[...] = jnp.where(sc_ref[kc] >= thr, 0.0, MASKED)

        def score_pieces(h):
            qm = head_operand(q_ref, h)
            buf = s_bufs[h % 2]
            maxes = []

            def piece(r):
                def run():
                    s = bias_ref[r:r + SCORE_PIECE, :] + _dot_nt(kd[r:r + SCORE_PIECE, :], qm)
                    buf[r:r + SCORE_PIECE, :] = s
                    maxes.append(_fold_rows(s, jnp.max))
                return run

            return [piece(r) for r in range(0, tkc, SCORE_PIECE)], maxes

        pieces, maxes = score_pieces(0)
        for run in pieces:
            run()
        for h in range(n_heads):
            nxt, nxt_maxes = score_pieces(h + 1) if h + 1 < n_heads else ((), None)
            _online_softmax_step(s_bufs[h % 2], p_bufs[h % 2], maxes, vt, m_ref, None, acc_ref, h,
                                 nxt)
            maxes = nxt_maxes
        return carry

    lax.fori_loop(0, nk, attn_body, 0)


def _dsa_attn_call(q, qi, wit, qn2, kd, kid, kn2, vt, n_idx_heads):
    b, s, a_q = q.shape
    n_heads = a_q // HEAD_DIM
    top_k = min(TOPK_MAX, s // 4)
    nkc, _, tkc = vt.shape[1:]
    tq = tkc
    per = tkc // tq
    kern = functools.partial(_dsa_attn_kernel, tq=tq, tkc=tkc, top_k=top_k, n_heads=n_heads,
                             n_idx_heads=n_idx_heads)
    rowq = lambda bi, i: (bi, i, 0)
    allk = lambda bi, i: (bi, 0, 0)
    return pl.pallas_call(
        kern,
        grid=(b, s // tq),
        in_specs=[
            pl.BlockSpec((1, tq, a_q), rowq),
            pl.BlockSpec((1, tq, qi.shape[-1]), rowq),
            pl.BlockSpec((1, 1, wit.shape[2], tq), lambda bi, i: (bi, i // per, 0, i % per)),
            pl.BlockSpec((1, 1, qn2.shape[2], tq), lambda bi, i: (bi, i // per, 0, i % per)),
            pl.BlockSpec((1, s, LANES), allk),
            pl.BlockSpec((1, s, LANES), allk),
            pl.BlockSpec((1, nkc, kn2.shape[2], tkc), lambda bi, i: (bi, 0, 0, 0)),
            pl.BlockSpec((1, nkc, HEAD_DIM, tkc), lambda bi, i: (bi, 0, 0, 0)),
        ],
        out_specs=pl.BlockSpec((1, tq, a_q), rowq),
        out_shape=jax.ShapeDtypeStruct((b, s, a_q), BF16),
        scratch_shapes=[
            pltpu.VMEM((nkc, tkc, tq), F32),
            pltpu.VMEM((n_heads, 1, tq), F32),
            pltpu.VMEM((n_heads, HEAD_DIM + DENOM_ROWS, tq), F32),
            pltpu.VMEM((tkc, tq), F32),
            pltpu.VMEM((tkc, tq), F32),
            pltpu.VMEM((tkc, tq), F32),
            pltpu.VMEM((tkc, tq), BF16),
            pltpu.VMEM((tkc, tq), BF16),
        ],
        compiler_params=pltpu.CompilerParams(
            dimension_semantics=("arbitrary", "arbitrary"),
            vmem_limit_bytes=_vmem_limit(6 * s * LANES * 2 + tq * s * 4 + 24 * 1024 * 1024)),
        name="dsa_attention",
    )(q, qi, wit, qn2, kd, kid, kn2, vt)


def _pad_to(w, axis, mult):
    n = w.shape[axis]
    pad = (-n) % mult
    if pad == 0:
        return w
    widths = [(0, 0)] * w.ndim
    widths[axis] = (0, pad)
    return jnp.pad(w, widths)


def _prep_dsa_w_in(w, a_q, n_idx_heads):
    o1 = a_q
    o2 = o1 + HEAD_DIM
    o3 = o2 + HEAD_DIM
    o4 = o3 + n_idx_heads * HEAD_DIM
    o5 = o4 + HEAD_DIM
    wq = w[..., :o1] * ATTN_Q_SCALE
    wk, wv, wqi, wki, wwi = w[..., o1:o2], w[..., o2:o3], w[..., o3:o4], w[..., o4:o5], w[..., o5:]
    rows = jnp.concatenate([wq, wqi, wk, wk, wki, wki], axis=-1).astype(BF16)
    cols = jnp.concatenate([wv, _pad_to(wwi, 2, 2 * SUBLANES)], axis=-1)
    return rows, jnp.swapaxes(cols, 1, 2).astype(BF16)


def kernel(x, c, positions, ada_w, ada_b, pre_norm, post_norm, ffn_w_gate, ffn_w_up, ffn_w_down,
           dsa_w_in, dsa_w_out, diff_w_in, diff_w_out, diff_lambda, diff_subln):
    b, s, d = x.shape
    depth = ada_w.shape[0]

    mods = _mod_call(c, ada_w, ada_b).reshape(depth, b, N_MOD, d)
    cos_f, sin_s = _rope_tables(positions)

    wg = _pad_to(ffn_w_gate, 3, MXU_DIM).astype(BF16)
    wu = _pad_to(ffn_w_up, 3, MXU_DIM).astype(BF16)
    wd = _pad_to(ffn_w_down, 2, MXU_DIM).astype(BF16)

    a_q = dsa_w_out.shape[1]
    n_idx_heads = (dsa_w_in.shape[2] - a_q - 3 * HEAD_DIM) // (HEAD_DIM + 1)
    dsa_in, dsa_in_t = _prep_dsa_w_in(dsa_w_in, a_q, n_idx_heads)
    dsa_out = dsa_w_out.astype(BF16)
    idx_q = n_idx_heads * HEAD_DIM
    wi_scale = n_idx_heads ** -0.5 * HEAD_DIM ** -0.5

    b_out = diff_w_out.shape[1]
    b_qk = (diff_w_in.shape[2] - b_out) // 2
    diff_in = jnp.concatenate(
        [diff_w_in[..., :b_qk] * ATTN_Q_SCALE, diff_w_in[..., b_qk:2 * b_qk]],
        axis=-1).astype(BF16)
    diff_in_t = jnp.swapaxes(diff_w_in[..., 2 * b_qk:], 1, 2).astype(BF16)
    diff_out = diff_w_out.astype(BF16)

    for i in range(depth):
        x = _ffn_call(x, mods, pre_norm, post_norm, wg, wu, wd, i, 0, 0)
        j = i // 2
        if i % 2 == 0:
            q, qi, kd, kid, vt, wit, qn2, kn2 = _proj_call(
                x, mods, pre_norm, cos_f, sin_s, dsa_in, dsa_in_t, i, j,
                row_cols=(a_q, idx_q, LANES, LANES),
                t_rows=(HEAD_DIM, 2 * SUBLANES), t_dtypes=(BF16, F32), t_scales=(None, wi_scale),
                norm_of=(0, 2))
            o = _dsa_attn_call(q, qi, wit, qn2, kd, kid, kn2, vt, n_idx_heads)
            x = _outproj_call(o, x, mods, post_norm, dsa_out, i, j)
        else:
            lam_init = 0.8 - 0.6 * math.exp(-0.3 * i)
            q, k, vt, qn2, kn2 = _proj_call(
                x, mods, pre_norm, cos_f, sin_s, diff_in, diff_in_t, i, j,
                row_cols=(b_qk, b_qk), t_rows=(b_out,), t_dtypes=(BF16,), t_scales=(None,),
                norm_of=(0, 1))
            o = _diff_attn_call(q, k, vt, qn2, kn2, diff_lambda, diff_subln, j, lam_init)
            x = _outproj_call(o, x, mods, post_norm, diff_out, i, j)
        x = _ffn_call(x, mods, pre_norm, post_norm, wg, wu, wd, i, 2, 1)
    return x
```

```python
import functools
import math

import jax
import jax.numpy as jnp
from jax import lax
from jax.experimental import pallas as pl
from jax.experimental.pallas import tpu as pltpu

F32 = jnp.float32
BF16 = jnp.bfloat16

EPS = 1e-6
ROPE_THETA = 500000.0
HEAD_DIM = 64
ROT_DIM = HEAD_DIM // 4
ROT_HALF = ROT_DIM // 2
TOPK_MAX = 256
FFN_HALF = 0.5
N_MOD = 9

LANES = 128
SUBLANES = 8
MXU_DIM = 256
VMEM_BYTES_V7X = 64 * 1024 * 1024

ROW_TILE = 512
FFN_SUB_ROWS = 512
DENOM_ROWS = 16
SCORE_PIECE = 128
SOFTMAX_STRIP = 128
COUNT_ROWS = 8

LOG2E = math.log2(math.e)
ATTN_Q_SCALE = HEAD_DIM ** -0.5 * LOG2E

MASKED = -1e30
BOUND_SAFETY = 1.03
DENOM_FLOOR = 2.0 ** -40
INT_MIN = -(2 ** 31)
KEY_NEG_FLT_MAX = -2139095040


def _vmem_limit(nbytes):
    return int(min(nbytes, VMEM_BYTES_V7X - 6 * 1024 * 1024))


def _dot(a, b):
    return jnp.dot(a, b, preferred_element_type=F32)


def _dot_nt(a, b):
    return lax.dot_general(a, b, (((1,), (1,)), ((), ())), preferred_element_type=F32)


def _rms(y):
    return y * lax.rsqrt(jnp.mean(y * y, axis=-1, keepdims=True) + EPS)


def _prenorm_mod(x, gain, shift, scale):
    return (_rms(x) * gain) * (1.0 + scale) + shift


def _silu(g):
    return g / (1.0 + jnp.exp(-g))


def _fold_rows(x, op):
    return op(x.reshape(x.shape[0] // SUBLANES, SUBLANES, x.shape[1]), axis=0)


def _mod_kernel(c_ref, w_ref, b_ref, o_ref):
    cond = _silu(c_ref[...])
    w = w_ref[0]
    c_hi = cond.astype(BF16)
    c_lo = (cond - c_hi.astype(F32)).astype(BF16)
    w_hi = w.astype(BF16)
    w_lo = (w - w_hi.astype(F32)).astype(BF16)
    o_ref[0] = _dot(c_hi, w_hi) + _dot(c_lo, w_hi) + _dot(c_hi, w_lo) + b_ref[0]


def _mod_call(c, ada_w, ada_b):
    depth, d, n = ada_w.shape
    b = c.shape[0]
    tn = 1024 if n % 1024 == 0 else n
    return pl.pallas_call(
        _mod_kernel,
        grid=(depth, n // tn),
        in_specs=[
            pl.BlockSpec((b, d), lambda l, j: (0, 0)),
            pl.BlockSpec((1, d, tn), lambda l, j: (l, 0, j)),
            pl.BlockSpec((1, 1, tn), lambda l, j: (l, 0, j)),
        ],
        out_specs=pl.BlockSpec((1, b, tn), lambda l, j: (l, 0, j)),
        out_shape=jax.ShapeDtypeStruct((depth, b, n), F32),
        compiler_params=pltpu.CompilerParams(
            dimension_semantics=("arbitrary", "arbitrary"),
            vmem_limit_bytes=_vmem_limit(40 * 1024 * 1024)),
        name="adaln_mod",
    )(c, ada_w, ada_b.reshape(depth, 1, n))


def _rope_kernel(ang_ref, cos_ref, sin_ref):
    a = ang_ref[0]
    lane = lax.broadcasted_iota(jnp.int32, a.shape, 1) & (HEAD_DIM - 1)
    rot = lane < ROT_DIM
    cos_ref[0] = jnp.where(rot, jnp.cos(a), 1.0)
    sin_ref[0] = jnp.where(rot, jnp.sin(a), 0.0)


def _rope_tables(positions):
    b, s = positions.shape
    inv = ROPE_THETA ** (-jnp.arange(0, ROT_DIM, 2, dtype=F32) / ROT_DIM)
    head = jnp.concatenate([-inv, inv, jnp.zeros((HEAD_DIM - ROT_DIM,), F32)])
    inv_lanes = jnp.tile(head, LANES // HEAD_DIM)
    ang = positions.astype(F32)[..., None] * inv_lanes
    tm = ROW_TILE if s % ROW_TILE == 0 else s
    spec = pl.BlockSpec((1, tm, LANES), lambda i, j: (i, j, 0))
    return pl.pallas_call(
        _rope_kernel,
        grid=(b, s // tm),
        in_specs=[spec],
        out_specs=[spec, spec],
        out_shape=[jax.ShapeDtypeStruct((b, s, LANES), F32)] * 2,
        compiler_params=pltpu.CompilerParams(dimension_semantics=("arbitrary", "arbitrary")),
        name="rope_tables",
    )(ang)


def _apply_rope(y, cos_f, sin_s, first_half):
    partner = jnp.where(first_half, pltpu.roll(y, LANES - ROT_HALF, 1), pltpu.roll(y, ROT_HALF, 1))
    return y * cos_f + partner * sin_s


def _ffn_kernel(x_ref, mod_ref, pre_ref, post_ref, wg_ref, wu_ref, wd_ref, o_ref, *, sub, f_chunks):
    shift = mod_ref[0, 0, 3 * sub:3 * sub + 1, :]
    scale = mod_ref[0, 0, 3 * sub + 1:3 * sub + 2, :]
    gate = mod_ref[0, 0, 3 * sub + 2:3 * sub + 3, :]
    tm = x_ref.shape[1]
    parts = [(r, r + FFN_SUB_ROWS) for r in range(0, tm, FFN_SUB_ROWS)]
    hs = {}
    ys = {}
    for n, (f0, f1) in enumerate(f_chunks):
        for r0, r1 in parts:
            if n == 0:
                hs[r0] = _prenorm_mod(x_ref[0, r0:r1, :], pre_ref[0, sub:sub + 1, :], shift,
                                      scale).astype(BF16)
            g = _dot(hs[r0], wg_ref[0, 0, :, f0:f1])
            u = _dot(hs[r0], wu_ref[0, 0, :, f0:f1])
            a = (_silu(g) * u).astype(BF16)
            part = _dot(a, wd_ref[0, 0, f0:f1, :])
            ys[r0] = part if n == 0 else ys[r0] + part
    for r0, r1 in parts:
        o_ref[0, r0:r1, :] = x_ref[0, r0:r1, :] + (FFN_HALF * gate) * (
            _rms(ys[r0]) * post_ref[0, sub:sub + 1, :])


def _ffn_call(x, mods, pre_norm, post_norm, wg, wu, wd, layer, sub, which):
    b, s, d = x.shape
    f = wg.shape[-1]
    tm = 2 * FFN_SUB_ROWS if s % (2 * FFN_SUB_ROWS) == 0 else FFN_SUB_ROWS
    assert s % tm == 0
    step = 4 * MXU_DIM
    f_chunks = tuple((f0, min(f0 + step, f)) for f0 in range(0, f, step))
    kern = functools.partial(_ffn_kernel, sub=sub, f_chunks=f_chunks)
    once = pl.Buffered(1)
    return pl.pallas_call(
        kern,
        grid=(b, s // tm),
        in_specs=[
            pl.BlockSpec((1, tm, d), lambda i, j: (i, j, 0)),
            pl.BlockSpec((1, 1, N_MOD, d), lambda i, j: (layer, i, 0, 0)),
            pl.BlockSpec((1, 3, d), lambda i, j: (layer, 0, 0)),
            pl.BlockSpec((1, 3, d), lambda i, j: (layer, 0, 0)),
            pl.BlockSpec((1, 1, d, f), lambda i, j: (layer, which, 0, 0), pipeline_mode=once),
            pl.BlockSpec((1, 1, d, f), lambda i, j: (layer, which, 0, 0), pipeline_mode=once),
            pl.BlockSpec((1, 1, f, d), lambda i, j: (layer, which, 0, 0), pipeline_mode=once),
        ],
        out_specs=pl.BlockSpec((1, tm, d), lambda i, j: (i, j, 0)),
        out_shape=jax.ShapeDtypeStruct((b, s, d), F32),
        compiler_params=pltpu.CompilerParams(
            dimension_semantics=("arbitrary", "arbitrary"),
            vmem_limit_bytes=_vmem_limit(3 * d * f * 2 + 4 * tm * d * 4 + 6 * tm * step * 4
                                         + 8 * 1024 * 1024)),
        name="swiglu_half_step",
    )(x, mods, pre_norm, post_norm, wg, wu, wd)


def _proj_kernel(x_ref, mod_ref, pre_ref, cos_ref, sin_ref, w_ref, wt_ref, *out_refs,
                 n_row_outs, norm_of, t_scales):
    x = x_ref[0]
    shift = mod_ref[0, 0, 3:4, :]
    scale = mod_ref[0, 0, 4:5, :]
    h = _prenorm_mod(x, pre_ref[0, 1:2, :], shift, scale).astype(BF16)
    cos_f = cos_ref[0]
    sin_s = sin_ref[0]
    lane = lax.broadcasted_iota(jnp.int32, cos_f.shape, 1) & (HEAD_DIM - 1)
    first_half = lane < ROT_HALF
    norm_refs = out_refs[len(out_refs) - len(norm_of):]
    col = 0
    for n, out_ref in enumerate(out_refs[:n_row_outs]):
        ncols = out_ref.shape[-1]
        y = _dot(h, w_ref[0, :, col:col + ncols])
        squares = []
        for j in range(ncols // LANES):
            blk = _apply_rope(y[:, j * LANES:(j + 1) * LANES], cos_f, sin_s, first_half)
            out_ref[0, :, j * LANES:(j + 1) * LANES] = blk.astype(out_ref.dtype)
            if n in norm_of:
                squares.append((blk * blk).astype(BF16))
        if n in norm_of:
            n_ref = norm_refs[norm_of.index(n)]
            rows = n_ref.shape[-2]
            head_of_col = lax.broadcasted_iota(jnp.int32, (rows, ncols), 1) // HEAD_DIM
            sel = jnp.where(head_of_col == lax.broadcasted_iota(jnp.int32, (rows, ncols), 0),
                            1.0, 0.0).astype(BF16)
            n_ref[0, 0] = _dot_nt(sel, jnp.concatenate(squares, axis=1))
        col += ncols
    r = 0
    for out_ref, t_scale in zip(out_refs[n_row_outs:len(out_refs) - len(norm_of)], t_scales):
        nrows = out_ref.shape[-2]
        yt = _dot_nt(wt_ref[0, r:r + nrows, :], h)
        if t_scale is not None:
            yt = yt * t_scale
        out_ref[0, 0] = yt.astype(out_ref.dtype)
        r += nrows


def _proj_call(x, mods, pre_norm, cos_f, sin_s, w, wt, layer, wl, row_cols, t_rows, t_dtypes,
               t_scales, norm_of):
    b, s, d = x.shape
    n = w.shape[-1]
    nt = wt.shape[-2]
    tm = ROW_TILE if s % ROW_TILE == 0 else s
    kern = functools.partial(_proj_kernel, n_row_outs=len(row_cols), norm_of=norm_of,
                             t_scales=t_scales)
    row = lambda i, j: (i, j, 0)
    once = pl.Buffered(1)
    t_rows = tuple(t_rows) + tuple(max(SUBLANES, row_cols[n] // HEAD_DIM) for n in norm_of)
    t_dtypes = tuple(t_dtypes) + (F32,) * len(norm_of)
    return pl.pallas_call(
        kern,
        grid=(b, s // tm),
        in_specs=[
            pl.BlockSpec((1, tm, d), row),
            pl.BlockSpec((1, 1, N_MOD, d), lambda i, j: (layer, i, 0, 0)),
            pl.BlockSpec((1, 3, d), lambda i, j: (layer, 0, 0)),
            pl.BlockSpec((1, tm, LANES), row),
            pl.BlockSpec((1, tm, LANES), row),
            pl.BlockSpec((1, d, n), lambda i, j: (wl, 0, 0), pipeline_mode=once),
            pl.BlockSpec((1, nt, d), lambda i, j: (wl, 0, 0), pipeline_mode=once),
        ],
        out_specs=([pl.BlockSpec((1, tm, nc), row) for nc in row_cols]
                   + [pl.BlockSpec((1, 1, nr, tm), lambda i, j: (i, j, 0, 0)) for nr in t_rows]),
        out_shape=([jax.ShapeDtypeStruct((b, s, nc), BF16) for nc in row_cols]
                   + [jax.ShapeDtypeStruct((b, s // tm, nr, tm), dt)
                      for nr, dt in zip(t_rows, t_dtypes)]),
        compiler_params=pltpu.CompilerParams(
            dimension_semantics=("arbitrary", "arbitrary"),
            vmem_limit_bytes=_vmem_limit(d * (n + nt) * 2 + 2 * tm * d * 4 + 6 * tm * (n + nt) * 2
                                         + 6 * tm * 1024 * 4 + 8 * 1024 * 1024)),
        name="mixer_in_proj",
    )(x, mods, pre_norm, cos_f, sin_s, w, wt)


def _outproj_kernel(o_ref, x_ref, mod_ref, post_ref, w_ref, out_ref):
    y = _dot(o_ref[0], w_ref[0])
    gate = mod_ref[0, 0, 5:6, :]
    out_ref[0] = x_ref[0] + gate * (_rms(y) * post_ref[0, 1:2, :])


def _outproj_call(o, x, mods, post_norm, w, layer, wl):
    b, s, d = x.shape
    k = o.shape[-1]
    tm = ROW_TILE if s % ROW_TILE == 0 else s
    row = lambda i, j: (i, j, 0)
    return pl.pallas_call(
        _outproj_kernel,
        grid=(b, s // tm),
        in_specs=[
            pl.BlockSpec((1, tm, k), row),
            pl.BlockSpec((1, tm, d), row),
            pl.BlockSpec((1, 1, N_MOD, d), lambda i, j: (layer, i, 0, 0)),
            pl.BlockSpec((1, 3, d), lambda i, j: (layer, 0, 0)),
            pl.BlockSpec((1, k, d), lambda i, j: (wl, 0, 0), pipeline_mode=pl.Buffered(1)),
        ],
        out_specs=pl.BlockSpec((1, tm, d), row),
        out_shape=jax.ShapeDtypeStruct((b, s, d), F32),
        compiler_params=pltpu.CompilerParams(
            dimension_semantics=("arbitrary", "arbitrary"),
            vmem_limit_bytes=_vmem_limit(k * d * 2 + 8 * tm * d * 4 + 8 * 1024 * 1024)),
        name="mixer_out_proj",
    )(o, x, mods, post_norm, w)


def _issue_pipelined(n_tiles, pieces_of, finish, n_bufs):
    for t in range(n_tiles):
        if t >= n_bufs:
            finish(t - n_bufs)
        for run in pieces_of(t):
            run()
    for t in range(max(0, n_tiles - n_bufs), n_tiles):
        finish(t)


def _online_softmax_step(s_ref, p_ref, tile_max, vt, m_ref, l_ref, acc_ref, slot, interleave=()):
    tk = s_ref.shape[0]
    strips = list(range(0, tk, SOFTMAX_STRIP))
    due = {}
    for j, run in enumerate(interleave):
        due.setdefault((j * len(strips)) // len(interleave), []).append(run)
    m8 = functools.reduce(jnp.maximum, tile_max)
    m_prev = m_ref[slot]
    m_new = jnp.maximum(m_prev, jnp.max(m8, axis=0, keepdims=True))
    alpha = jnp.exp2(m_prev - m_new)
    l8 = None
    for n, r in enumerate(strips):
        for run in due.get(n, ()):
            run()
        p = jnp.exp2(s_ref[r:r + SOFTMAX_STRIP, :] - m_new)
        p_ref[r:r + SOFTMAX_STRIP, :] = p.astype(BF16)
        if l_ref is not None:
            f = _fold_rows(p, jnp.sum)
            l8 = f if l8 is None else l8 + f
    if l_ref is not None:
        l_ref[slot] = alpha * l_ref[slot] + l8
    acc_ref[slot] = alpha * acc_ref[slot] + _dot(vt, p_ref[...])
    m_ref[slot] = m_new


def _diff_attn_kernel(lam_ref, sub_ref, q1_ref, q2_ref, qn1_ref, qn2_ref, k1_ref, k2_ref, kn1_ref,
                      kn2_ref, vt_ref, o_ref, m_ref, acc_ref, s0_ref, s1_ref, p0_ref, p1_ref, *,
                      tq, tk, lam_init, pairs):
    s_bufs = (s0_ref, s1_ref)
    p_bufs = (p0_ref, p1_ref)
    i = pl.program_id(2)
    lane = lax.broadcasted_iota(jnp.int32, (tq, LANES), 1)
    low = lane < HEAD_DIM
    v_dim = 2 * HEAD_DIM
    n_slots = 4 * pairs

    def slot_parts(slot):
        return slot // 4, (slot % 4) // 2, slot % 2

    qms = []
    bounds = []
    for slot in range(n_slots):
        pair, comp, half = slot_parts(slot)
        q = (q1_ref, q2_ref)[comp][0, :, pair * LANES:(pair + 1) * LANES]
        zero = jnp.zeros_like(q)
        qms.append(jnp.where(low, q, zero) if half == 0 else jnp.where(low, zero, q))
        j = 2 * pair + half
        k2_max = jnp.max((kn1_ref, kn2_ref)[comp][0, :, 0, j:j + 1, :])
        q2 = (qn1_ref, qn2_ref)[comp][0, 0, 0, j:j + 1, :]
        bounds.append(jnp.sqrt(q2 * k2_max) * BOUND_SAFETY)
    k_refs = (k1_ref, k2_ref)
    ones_rows = jnp.ones((DENOM_ROWS, tk), BF16)

    def chunk_operands(kc):
        off = pl.multiple_of(kc * tk, tk)
        ks = [k_r[0, pl.ds(off, tk), :] for k_r in k_refs]
        vt = vt_ref[0, kc]
        vts = [jnp.concatenate([vt[h * v_dim:(h + 1) * v_dim, :], ones_rows], axis=0)
               for h in range(2 * pairs)]
        return ks, vts

    def masked_scores(ks, slot, r, keep):
        pair, comp, _ = slot_parts(slot)
        k = ks[comp][r:r + SCORE_PIECE, pair * LANES:(pair + 1) * LANES]
        s = _dot_nt(k, qms[slot])
        if keep is not None:
            s = jnp.where(keep[r:r + SCORE_PIECE, :], s, MASKED)
        return s

    def causal_keep(kc):
        key_idx = kc * tk + lax.broadcasted_iota(jnp.int32, (tk, tq), 0)
        qry_idx = i * tq + lax.broadcasted_iota(jnp.int32, (tk, tq), 1)
        return key_idx <= qry_idx

    def sweep(step):
        n_full = (i * tq) // tk

        def full_body(kc, carry):
            step(kc, None)
            return carry

        lax.fori_loop(0, n_full, full_body, 0)
        for d in range(tq // tk):
            step(n_full + d, causal_keep(n_full + d))

    def fast_step(kc, keep):
        ks, vts = chunk_operands(kc)

        def prob_pieces(slot):
            buf = p_bufs[slot % len(p_bufs)]

            def piece(r):
                def run():
                    s = masked_scores(ks, slot, r, keep)
                    buf[r:r + SCORE_PIECE, :] = jnp.exp2(s - bounds[slot]).astype(BF16)
                return run

            return [piece(r) for r in range(0, tk, SCORE_PIECE)]

        def accumulate(slot):
            pair, _, half = slot_parts(slot)
            acc_ref[slot] = acc_ref[slot] + _dot(vts[2 * pair + half],
                                                 p_bufs[slot % len(p_bufs)][...])

        _issue_pipelined(n_slots, prob_pieces, accumulate, len(p_bufs))

    acc_ref[...] = jnp.zeros(acc_ref.shape, F32)
    sweep(fast_step)
    denom_min = functools.reduce(
        jnp.minimum, [acc_ref[slot, v_dim:v_dim + 1, :] for slot in range(n_slots)])

    @pl.when(jnp.logical_not(jnp.min(denom_min) > DENOM_FLOOR))
    def _():
        def exact_step(kc, keep):
            ks, vts = chunk_operands(kc)

            def score_pieces(slot):
                buf = s_bufs[slot % 2]
                maxes = []

                def piece(r):
                    def run():
                        s = masked_scores(ks, slot, r, keep)
                        buf[r:r + SCORE_PIECE, :] = s
                        maxes.append(_fold_rows(s, jnp.max))
                    return run

                return [piece(r) for r in range(0, tk, SCORE_PIECE)], maxes

            pieces, maxes = score_pieces(0)
            for run in pieces:
                run()
            for slot in range(n_slots):
                nxt, nxt_maxes = score_pieces(slot + 1) if slot + 1 < n_slots else ((), None)
                pair, _, half = slot_parts(slot)
                _online_softmax_step(s_bufs[slot % 2], p_bufs[slot % 2], maxes, vts[2 * pair + half],
                                     m_ref, None, acc_ref, slot, nxt)
                maxes = nxt_maxes

        m_ref[...] = jnp.full(m_ref.shape, MASKED, F32)
        acc_ref[...] = jnp.zeros(acc_ref.shape, F32)
        sweep(exact_step)

    lam = lam_ref[0]
    lam_val = (jnp.exp(jnp.sum(lam[0:1] * lam[1:2], axis=1, keepdims=True))
               - jnp.exp(jnp.sum(lam[2:3] * lam[3:4], axis=1, keepdims=True)) + lam_init)
    for pair in range(pairs):
        for half in range(2):
            s1 = 4 * pair + half
            s2 = s1 + 2
            o1 = acc_ref[s1, :v_dim, :] / acc_ref[s1, v_dim:v_dim + 1, :]
            o2 = acc_ref[s2, :v_dim, :] / acc_ref[s2, v_dim:v_dim + 1, :]
            o = (o1 - lam_val * o2).T
            o = _rms(o) * sub_ref[0] * (1.0 - lam_init)
            head = 2 * pair + half
            o_ref[0, :, head * v_dim:(head + 1) * v_dim] = o.astype(o_ref.dtype)


def _diff_attn_call(q, k, vt, qn2, kn2, lam, subln, wl, lam_init):
    b, s, qk = q.shape
    n_pairs = qk // (2 * LANES)
    pairs = 2 if n_pairs % 2 == 0 else 1
    groups = n_pairs // pairs
    nk, _, tk = vt.shape[1:]
    tq = tk
    v_dim = 2 * HEAD_DIM
    n_slots = 4 * pairs
    hps = 2 * pairs
    qn2 = qn2.reshape(b, s // tq, 2 * groups, hps, tq)
    kn2 = kn2.reshape(b, nk, 2 * groups, hps, tk)
    kern = functools.partial(_diff_attn_kernel, tq=tq, tk=tk, lam_init=lam_init, pairs=pairs)
    return pl.pallas_call(
        kern,
        grid=(b, groups, s // tq),
        in_specs=[
            pl.BlockSpec((1, 4, HEAD_DIM), lambda bi, g, i: (wl, 0, 0)),
            pl.BlockSpec((1, 1, v_dim), lambda bi, g, i: (wl, 0, 0)),
            pl.BlockSpec((1, tq, pairs * LANES), lambda bi, g, i: (bi, i, g)),
            pl.BlockSpec((1, tq, pairs * LANES), lambda bi, g, i: (bi, i, groups + g)),
            pl.BlockSpec((1, 1, 1, hps, tq), lambda bi, g, i: (bi, i, g, 0, 0)),
            pl.BlockSpec((1, 1, 1, hps, tq), lambda bi, g, i: (bi, i, groups + g, 0, 0)),
            pl.BlockSpec((1, s, pairs * LANES), lambda bi, g, i: (bi, 0, g)),
            pl.BlockSpec((1, s, pairs * LANES), lambda bi, g, i: (bi, 0, groups + g)),
            pl.BlockSpec((1, nk, 1, hps, tk), lambda bi, g, i: (bi, 0, g, 0, 0)),
            pl.BlockSpec((1, nk, 1, hps, tk), lambda bi, g, i: (bi, 0, groups + g, 0, 0)),
            pl.BlockSpec((1, nk, 2 * pairs * v_dim, tk), lambda bi, g, i: (bi, 0, g, 0)),
        ],
        out_specs=pl.BlockSpec((1, tq, 2 * pairs * v_dim), lambda bi, g, i: (bi, i, g)),
        out_shape=jax.ShapeDtypeStruct((b, s, n_pairs * 2 * v_dim), BF16),
        scratch_shapes=[
            pltpu.VMEM((n_slots, 1, tq), F32),
            pltpu.VMEM((n_slots, v_dim + DENOM_ROWS, tq), F32),
            pltpu.VMEM((tk, tq), F32),
            pltpu.VMEM((tk, tq), F32),
            pltpu.VMEM((tk, tq), BF16),
            pltpu.VMEM((tk, tq), BF16),
        ],
        compiler_params=pltpu.CompilerParams(
            dimension_semantics=("arbitrary", "arbitrary", "arbitrary"),
            vmem_limit_bytes=_vmem_limit(40 * 1024 * 1024)),
        name="diff_attention",
    )(lam, subln.reshape(subln.shape[0], 1, v_dim), q, q, qn2, qn2, k, k, kn2, kn2, vt)


def _dsa_attn_kernel(q_ref, qi_ref, wi_ref, qn2_ref, kd_ref, kid_ref, kn2_ref, vt_ref, o_ref,
                     sc_ref, m_ref, acc_ref, bias_ref, s0_ref, s1_ref, p0_ref, p1_ref, *,
                     tq, tkc, top_k, n_heads, n_idx_heads):
    s_bufs = (s0_ref, s1_ref)
    p_bufs = (p0_ref, p1_ref)
    i = pl.program_id(1)
    nk = (i * tq + tq + tkc - 1) // tkc
    lane = lax.broadcasted_iota(jnp.int32, (tq, LANES), 1)
    low = lane < HEAD_DIM
    key0 = lax.broadcasted_iota(jnp.int32, (tkc, tq), 0)
    qry = i * tq + lax.broadcasted_iota(jnp.int32, (tkc, tq), 1)
    k_f = float(top_k)

    def head_operand(ref, h):
        pair = ref[0, :, (h // 2) * LANES:(h // 2 + 1) * LANES]
        zero = jnp.zeros_like(pair)
        return jnp.where(low, pair, zero) if h % 2 == 0 else jnp.where(low, zero, pair)

    def key_chunk(ref, kc):
        return ref[0, pl.ds(pl.multiple_of(kc * tkc, tkc), tkc), :]

    wt = wi_ref[0, 0]
    qims = [head_operand(qi_ref, h) for h in range(n_idx_heads)]

    def fold_count(hit):
        return jnp.sum(jnp.where(hit, 1.0, 0.0).reshape(tkc // COUNT_ROWS, COUNT_ROWS, tq), axis=0)

    def score_body(kc, cnts):
        kik = key_chunk(kid_ref, kc)
        score = jnp.zeros((tkc, tq), F32)
        for h in range(n_idx_heads):
            score = score + wt[h:h + 1, :] * jnp.maximum(_dot_nt(kik, qims[h]), 0.0)
        score = jnp.where(kc * tkc + key0 <= qry, score, -jnp.inf)
        sc_ref[kc] = score
        return cnts[0] + fold_count(score >= 0.0), cnts[1] + fold_count(score > 0.0)

    zero_counts = lax.fori_loop(0, nk, score_body, (jnp.zeros((COUNT_ROWS, tq), F32),) * 2)
    n_ge0, n_gt0 = [jnp.sum(c, axis=0, keepdims=True) for c in zero_counts]

    def key_to_float(u):
        key = u ^ jnp.int32(INT_MIN)
        bits = jnp.where(key >= 0, key, key ^ jnp.int32(0x7FFFFFFF))
        return lax.bitcast_convert_type(bits, F32)

    def counts(*preds):
        def body(kc, cnts):
            sc = sc_ref[kc]
            return tuple(c + fold_count(pred(sc)) for c, pred in zip(cnts, preds))
        cnts = lax.fori_loop(0, nk, body, (jnp.zeros((COUNT_ROWS, tq), F32),) * len(preds))
        return [jnp.sum(c, axis=0, keepdims=True) for c in cnts]

    few_keys = i * tq + lax.broadcasted_iota(jnp.int32, (1, tq), 1) < top_k - 1
    zero_thr = jnp.logical_and(n_gt0 < k_f, n_ge0 >= k_f)
    zero_tied = jnp.logical_and(zero_thr, n_ge0 > k_f)

    def bit_cond(state):
        step, _, settled = state
        return jnp.logical_and(step < 32, jnp.min(settled) < 0.5)

    def bit_body(state):
        step, prefix, settled = state
        cand_u = prefix | lax.shift_left(jnp.int32(1), 31 - step)
        cand = key_to_float(cand_u)
        (cnt,) = counts(lambda sc: sc >= cand)
        settled = jnp.maximum(settled, jnp.where(cnt == k_f, 1.0, 0.0))
        return step + 1, jnp.where(cnt >= k_f, cand_u, prefix), settled

    steps, prefix, settled = lax.while_loop(
        bit_cond, bit_body,
        (jnp.int32(1), jnp.where(n_ge0 >= k_f, jnp.int32(INT_MIN), jnp.int32(0)),
         jnp.where(jnp.logical_or(few_keys, zero_thr), 1.0, 0.0)))
    thr_key = jnp.maximum(prefix ^ jnp.int32(INT_MIN), jnp.int32(KEY_NEG_FLT_MAX))
    thr = lax.bitcast_convert_type(
        jnp.where(thr_key >= 0, thr_key, thr_key ^ jnp.int32(0x7FFFFFFF)), F32)

    any_tie = jnp.logical_or(jnp.max(jnp.where(zero_tied, 1.0, 0.0)) > 0.5,
                             jnp.min(settled) < 0.5)

    @pl.when(any_tie)
    def _():
        (n_gt,) = counts(lambda sc: sc > thr)
        room = k_f - n_gt
        r_i = lax.broadcasted_iota(jnp.int32, (tkc, tkc), 0)
        c_i = lax.broadcasted_iota(jnp.int32, (tkc, tkc), 1)
        prefix_op = jnp.where(r_i >= c_i, 1.0, 0.0).astype(BF16)

        def drop_body(kc, seen):
            sc = sc_ref[kc]
            tied = sc == thr
            rank = seen + _dot(prefix_op, jnp.where(tied, 1.0, 0.0).astype(BF16))
            sc_ref[kc] = jnp.where(jnp.logical_and(tied, rank > room), -jnp.inf, sc)
            return rank[tkc - 1:tkc, :]

        lax.fori_loop(0, nk, drop_body, jnp.zeros((1, tq), F32))

    ones_rows = jnp.ones((DENOM_ROWS, tkc), BF16)

    def chunk_operands(kc):
        kd = key_chunk(kd_ref, kc)
        vt = jnp.concatenate([vt_ref[0, kc], ones_rows], axis=0)
        bias_ref[...] = jnp.where(sc_ref[kc] >= thr, 0.0, MASKED)
        return kd, vt

    k2_max = jnp.max(kn2_ref[0, :, 0:1, :])
    bounds = [jnp.sqrt(qn2_ref[0, 0, h:h + 1, :] * k2_max) * BOUND_SAFETY for h in range(n_heads)]
    acc_ref[...] = jnp.zeros(acc_ref.shape, F32)

    def fast_body(kc, carry):
        kd, vt = chunk_operands(kc)

        def prob_pieces(h):
            qm = head_operand(q_ref, h)
            buf = p_bufs[h % len(p_bufs)]

            def piece(r):
                def run():
                    s = bias_ref[r:r + SCORE_PIECE, :] + _dot_nt(kd[r:r + SCORE_PIECE, :], qm)
                    buf[r:r + SCORE_PIECE, :] = jnp.exp2(s - bounds[h]).astype(BF16)
                return run

            return [piece(r) for r in range(0, tkc, SCORE_PIECE)]

        def accumulate(h):
            acc_ref[h] = acc_ref[h] + _dot(vt, p_bufs[h % len(p_bufs)][...])

        _issue_pipelined(n_heads, prob_pieces, accumulate, len(p_bufs))
        return carry

    lax.fori_loop(0, nk, fast_body, 0)
    denom_min = functools.reduce(
        jnp.minimum, [acc_ref[h, HEAD_DIM:HEAD_DIM + 1, :] for h in range(n_heads)])

    @pl.when(jnp.logical_not(jnp.min(denom_min) > DENOM_FLOOR))
    def _():
        _dsa_exact_attention(q_ref, kd_ref, vt_ref, sc_ref, m_ref, acc_ref, bias_ref, s_bufs, p_bufs,
                             thr, nk, tkc, n_heads, head_operand, key_chunk, ones_rows)

    for j in range(n_heads // 2):
        halves = [acc_ref[h, :HEAD_DIM, :] / acc_ref[h, HEAD_DIM:HEAD_DIM + 1, :]
                  for h in (2 * j, 2 * j + 1)]
        o_ref[0, :, j * LANES:(j + 1) * LANES] = jnp.concatenate(halves, axis=0).T.astype(o_ref.dtype)


def _dsa_exact_attention(q_ref, kd_ref, vt_ref, sc_ref, m_ref, acc_ref, bias_ref, s_bufs, p_bufs,
                         thr, nk, tkc, n_heads, head_operand, key_chunk, ones_rows):
    m_ref[...] = jnp.full(m_ref.shape, MASKED, F32)
    acc_ref[...] = jnp.zeros(acc_ref.shape, F32)

    def attn_body(kc, carry):
        kd = key_chunk(kd_ref, kc)
        vt = jnp.concatenate([vt_ref[0, kc], ones_rows], axis=0)
        bias_ref[...] = jnp.where(sc_ref[kc] >= thr, 0.0, MASKED)

        def score_pieces(h):
            qm = head_operand(q_ref, h)
            buf = s_bufs[h % 2]
            maxes = []

            def piece(r):
                def run():
                    s = bias_ref[r:r + SCORE_PIECE, :] + _dot_nt(kd[r:r + SCORE_PIECE, :], qm)
                    buf[r:r + SCORE_PIECE, :] = s
                    maxes.append(_fold_rows(s, jnp.max))
                return run

            return [piece(r) for r in range(0, tkc, SCORE_PIECE)], maxes

        pieces, maxes = score_pieces(0)
        for run in pieces:
            run()
        for h in range(n_heads):
            nxt, nxt_maxes = score_pieces(h + 1) if h + 1 < n_heads else ((), None)
            _online_softmax_step(s_bufs[h % 2], p_bufs[h % 2], maxes, vt, m_ref, None, acc_ref, h,
                                 nxt)
            maxes = nxt_maxes
        return carry

    lax.fori_loop(0, nk, attn_body, 0)


def _dsa_attn_call(q, qi, wit, qn2, kd, kid, kn2, vt, n_idx_heads):
    b, s, a_q = q.shape
    n_heads = a_q // HEAD_DIM
    top_k = min(TOPK_MAX, s // 4)
    nkc, _, tkc = vt.shape[1:]
    tq = tkc
    per = tkc // tq
    kern = functools.partial(_dsa_attn_kernel, tq=tq, tkc=tkc, top_k=top_k, n_heads=n_heads,
                             n_idx_heads=n_idx_heads)
    rowq = lambda bi, i: (bi, i, 0)
    allk = lambda bi, i: (bi, 0, 0)
    return pl.pallas_call(
        kern,
        grid=(b, s // tq),
        in_specs=[
            pl.BlockSpec((1, tq, a_q), rowq),
            pl.BlockSpec((1, tq, qi.shape[-1]), rowq),
            pl.BlockSpec((1, 1, wit.shape[2], tq), lambda bi, i: (bi, i // per, 0, i % per)),
            pl.BlockSpec((1, 1, qn2.shape[2], tq), lambda bi, i: (bi, i // per, 0, i % per)),
            pl.BlockSpec((1, s, LANES), allk),
            pl.BlockSpec((1, s, LANES), allk),
            pl.BlockSpec((1, nkc, kn2.shape[2], tkc), lambda bi, i: (bi, 0, 0, 0)),
            pl.BlockSpec((1, nkc, HEAD_DIM, tkc), lambda bi, i: (bi, 0, 0, 0)),
        ],
        out_specs=pl.BlockSpec((1, tq, a_q), rowq),
        out_shape=jax.ShapeDtypeStruct((b, s, a_q), BF16),
        scratch_shapes=[
            pltpu.VMEM((nkc, tkc, tq), F32),
            pltpu.VMEM((n_heads, 1, tq), F32),
            pltpu.VMEM((n_heads, HEAD_DIM + DENOM_ROWS, tq), F32),
            pltpu.VMEM((tkc, tq), F32),
            pltpu.VMEM((tkc, tq), F32),
            pltpu.VMEM((tkc, tq), F32),
            pltpu.VMEM((tkc, tq), BF16),
            pltpu.VMEM((tkc, tq), BF16),
        ],
        compiler_params=pltpu.CompilerParams(
            dimension_semantics=("arbitrary", "arbitrary"),
            vmem_limit_bytes=_vmem_limit(6 * s * LANES * 2 + tq * s * 4 + 24 * 1024 * 1024)),
        name="dsa_attention",
    )(q, qi, wit, qn2, kd, kid, kn2, vt)


def _pad_to(w, axis, mult):
    n = w.shape[axis]
    pad = (-n) % mult
    if pad == 0:
        return w
    widths = [(0, 0)] * w.ndim
    widths[axis] = (0, pad)
    return jnp.pad(w, widths)


def _prep_dsa_w_in(w, a_q, n_idx_heads):
    o1 = a_q
    o2 = o1 + HEAD_DIM
    o3 = o2 + HEAD_DIM
    o4 = o3 + n_idx_heads * HEAD_DIM
    o5 = o4 + HEAD_DIM
    wq = w[..., :o1] * ATTN_Q_SCALE
    wk, wv, wqi, wki, wwi = w[..., o1:o2], w[..., o2:o3], w[..., o3:o4], w[..., o4:o5], w[..., o5:]
    rows = jnp.concatenate([wq, wqi, wk, wk, wki, wki], axis=-1).astype(BF16)
    cols = jnp.concatenate([wv, _pad_to(wwi, 2, 2 * SUBLANES)], axis=-1)
    return rows, jnp.swapaxes(cols, 1, 2).astype(BF16)


def kernel(x, c, positions, ada_w, ada_b, pre_norm, post_norm, ffn_w_gate, ffn_w_up, ffn_w_down,
           dsa_w_in, dsa_w_out, diff_w_in, diff_w_out, diff_lambda, diff_subln):
    b, s, d = x.shape
    depth = ada_w.shape[0]

    mods = _mod_call(c, ada_w, ada_b).reshape(depth, b, N_MOD, d)
    cos_f, sin_s = _rope_tables(positions)

    wg = _pad_to(ffn_w_gate, 3, MXU_DIM).astype(BF16)
    wu = _pad_to(ffn_w_up, 3, MXU_DIM).astype(BF16)
    wd = _pad_to(ffn_w_down, 2, MXU_DIM).astype(BF16)

    a_q = dsa_w_out.shape[1]
    n_idx_heads = (dsa_w_in.shape[2] - a_q - 3 * HEAD_DIM) // (HEAD_DIM + 1)
    dsa_in, dsa_in_t = _prep_dsa_w_in(dsa_w_in, a_q, n_idx_heads)
    dsa_out = dsa_w_out.astype(BF16)
    idx_q = n_idx_heads * HEAD_DIM
    wi_scale = n_idx_heads ** -0.5 * HEAD_DIM ** -0.5

    b_out = diff_w_out.shape[1]
    b_qk = (diff_w_in.shape[2] - b_out) // 2
    diff_in = jnp.concatenate(
        [diff_w_in[..., :b_qk] * ATTN_Q_SCALE, diff_w_in[..., b_qk:2 * b_qk]],
        axis=-1).astype(BF16)
    diff_in_t = jnp.swapaxes(diff_w_in[..., 2 * b_qk:], 1, 2).astype(BF16)
    diff_out = diff_w_out.astype(BF16)

    for i in range(depth):
        x = _ffn_call(x, mods, pre_norm, post_norm, wg, wu, wd, i, 0, 0)
        j = i // 2
        if i % 2 == 0:
            q, qi, kd, kid, vt, wit, qn2, kn2 = _proj_call(
                x, mods, pre_norm, cos_f, sin_s, dsa_in, dsa_in_t, i, j,
                row_cols=(a_q, idx_q, LANES, LANES),
                t_rows=(HEAD_DIM, 2 * SUBLANES), t_dtypes=(BF16, F32), t_scales=(None, wi_scale),
                norm_of=(0, 2))
            o = _dsa_attn_call(q, qi, wit, qn2, kd, kid, kn2, vt, n_idx_heads)
            x = _outproj_call(o, x, mods, post_norm, dsa_out, i, j)
        else:
            lam_init = 0.8 - 0.6 * math.exp(-0.3 * i)
            q, k, vt, qn2, kn2 = _proj_call(
                x, mods, pre_norm, cos_f, sin_s, diff_in, diff_in_t, i, j,
                row_cols=(b_qk, b_qk), t_rows=(b_out,), t_dtypes=(BF16,), t_scales=(None,),
                norm_of=(0, 1))
            o = _diff_attn_call(q, k, vt, qn2, kn2, diff_lambda, diff_subln, j, lam_init)
            x = _outproj_call(o, x, mods, post_norm, diff_out, i, j)
        x = _ffn_call(x, mods, pre_norm, post_norm, wg, wu, wd, i, 2, 1)
    return x
```

```python
import functools
import math

import jax
import jax.numpy as jnp
from jax import lax
from jax.experimental import pallas as pl
from jax.experimental.pallas import tpu as pltpu

F32 = jnp.float32
BF16 = jnp.bfloat16

EPS = 1e-6
ROPE_THETA = 500000.0
HEAD_DIM = 64
ROT_DIM = HEAD_DIM // 4
ROT_HALF = ROT_DIM // 2
TOPK_MAX = 256
FFN_HALF = 0.5
N_MOD = 9

LANES = 128
SUBLANES = 8
MXU_DIM = 256
VMEM_BYTES_V7X = 64 * 1024 * 1024

ROW_TILE = 512
FFN_SUB_ROWS = 512
DENOM_ROWS = 16
SCORE_PIECE = 128
SOFTMAX_STRIP = 128
COUNT_ROWS = 8

LOG2E = math.log2(math.e)
ATTN_Q_SCALE = HEAD_DIM ** -0.5 * LOG2E

MASKED = -1e30
BOUND_SAFETY = 1.03
DENOM_FLOOR = 2.0 ** -40
INT_MIN = -(2 ** 31)
KEY_NEG_FLT_MAX = -2139095040


def _vmem_limit(nbytes):
    return int(min(nbytes, VMEM_BYTES_V7X - 6 * 1024 * 1024))


def _dot(a, b):
    return jnp.dot(a, b, preferred_element_type=F32)


def _dot_nt(a, b):
    return lax.dot_general(a, b, (((1,), (1,)), ((), ())), preferred_element_type=F32)


def _rms(y):
    return y * lax.rsqrt(jnp.mean(y * y, axis=-1, keepdims=True) + EPS)


def _prenorm_mod(x, gain, shift, scale):
    return (_rms(x) * gain) * (1.0 + scale) + shift


def _silu(g):
    return g / (1.0 + jnp.exp(-g))


def _fold_rows(x, op):
    return op(x.reshape(x.shape[0] // SUBLANES, SUBLANES, x.shape[1]), axis=0)


def _mod_kernel(c_ref, w_ref, b_ref, o_ref):
    cond = _silu(c_ref[...])
    w = w_ref[0]
    c_hi = cond.astype(BF16)
    c_lo = (cond - c_hi.astype(F32)).astype(BF16)
    w_hi = w.astype(BF16)
    w_lo = (w - w_hi.astype(F32)).astype(BF16)
    o_ref[0] = _dot(c_hi, w_hi) + _dot(c_lo, w_hi) + _dot(c_hi, w_lo) + b_ref[0]


def _mod_call(c, ada_w, ada_b):
    depth, d, n = ada_w.shape
    b = c.shape[0]
    tn = 1024 if n % 1024 == 0 else n
    return pl.pallas_call(
        _mod_kernel,
        grid=(depth, n // tn),
        in_specs=[
            pl.BlockSpec((b, d), lambda l, j: (0, 0)),
            pl.BlockSpec((1, d, tn), lambda l, j: (l, 0, j)),
            pl.BlockSpec((1, 1, tn), lambda l, j: (l, 0, j)),
        ],
        out_specs=pl.BlockSpec((1, b, tn), lambda l, j: (l, 0, j)),
        out_shape=jax.ShapeDtypeStruct((depth, b, n), F32),
        compiler_params=pltpu.CompilerParams(
            dimension_semantics=("arbitrary", "arbitrary"),
            vmem_limit_bytes=_vmem_limit(40 * 1024 * 1024)),
        name="adaln_mod",
    )(c, ada_w, ada_b.reshape(depth, 1, n))


def _rope_kernel(ang_ref, cos_ref, sin_ref):
    a = ang_ref[0]
    lane = lax.broadcasted_iota(jnp.int32, a.shape, 1) & (HEAD_DIM - 1)
    rot = lane < ROT_DIM
    cos_ref[0] = jnp.where(rot, jnp.cos(a), 1.0)
    sin_ref[0] = jnp.where(rot, jnp.sin(a), 0.0)


def _rope_tables(positions):
    b, s = positions.shape
    inv = ROPE_THETA ** (-jnp.arange(0, ROT_DIM, 2, dtype=F32) / ROT_DIM)
    head = jnp.concatenate([-inv, inv, jnp.zeros((HEAD_DIM - ROT_DIM,), F32)])
    inv_lanes = jnp.tile(head, LANES // HEAD_DIM)
    ang = positions.astype(F32)[..., None] * inv_lanes
    tm = ROW_TILE if s % ROW_TILE == 0 else s
    spec = pl.BlockSpec((1, tm, LANES), lambda i, j: (i, j, 0))
    return pl.pallas_call(
        _rope_kernel,
        grid=(b, s // tm),
        in_specs=[spec],
        out_specs=[spec, spec],
        out_shape=[jax.ShapeDtypeStruct((b, s, LANES), F32)] * 2,
        compiler_params=pltpu.CompilerParams(dimension_semantics=("arbitrary", "arbitrary")),
        name="rope_tables",
    )(ang)


def _apply_rope(y, cos_f, sin_s, first_half):
    partner = jnp.where(first_half, pltpu.roll(y, LANES - ROT_HALF, 1), pltpu.roll(y, ROT_HALF, 1))
    return y * cos_f + partner * sin_s


def _ffn_kernel(x_ref, mod_ref, pre_ref, post_ref, wg_ref, wu_ref, wd_ref, o_ref, *, sub, f_chunks):
    shift = mod_ref[0, 0, 3 * sub:3 * sub + 1, :]
    scale = mod_ref[0, 0, 3 * sub + 1:3 * sub + 2, :]
    gate = mod_ref[0, 0, 3 * sub + 2:3 * sub + 3, :]
    tm = x_ref.shape[1]
    parts = [(r, r + FFN_SUB_ROWS) for r in range(0, tm, FFN_SUB_ROWS)]
    hs = {}
    ys = {}
    for n, (f0, f1) in enumerate(f_chunks):
        for r0, r1 in parts:
            if n == 0:
                hs[r0] = _prenorm_mod(x_ref[0, r0:r1, :], pre_ref[0, sub:sub + 1, :], shift,
                                      scale).astype(BF16)
            g = _dot(hs[r0], wg_ref[0, 0, :, f0:f1])
            u = _dot(hs[r0], wu_ref[0, 0, :, f0:f1])
            a = (_silu(g) * u).astype(BF16)
            part = _dot(a, wd_ref[0, 0, f0:f1, :])
            ys[r0] = part if n == 0 else ys[r0] + part
    for r0, r1 in parts:
        o_ref[0, r0:r1, :] = x_ref[0, r0:r1, :] + (FFN_HALF * gate) * (
            _rms(ys[r0]) * post_ref[0, sub:sub + 1, :])


def _ffn_call(x, mods, pre_norm, post_norm, wg, wu, wd, layer, sub, which):
    b, s, d = x.shape
    f = wg.shape[-1]
    tm = 2 * FFN_SUB_ROWS if s % (2 * FFN_SUB_ROWS) == 0 else FFN_SUB_ROWS
    assert s % tm == 0
    step = 4 * MXU_DIM
    f_chunks = tuple((f0, min(f0 + step, f)) for f0 in range(0, f, step))
    kern = functools.partial(_ffn_kernel, sub=sub, f_chunks=f_chunks)
    once = pl.Buffered(1)
    return pl.pallas_call(
        kern,
        grid=(b, s // tm),
        in_specs=[
            pl.BlockSpec((1, tm, d), lambda i, j: (i, j, 0)),
            pl.BlockSpec((1, 1, N_MOD, d), lambda i, j: (layer, i, 0, 0)),
            pl.BlockSpec((1, 3, d), lambda i, j: (layer, 0, 0)),
            pl.BlockSpec((1, 3, d), lambda i, j: (layer, 0, 0)),
            pl.BlockSpec((1, 1, d, f), lambda i, j: (layer, which, 0, 0), pipeline_mode=once),
            pl.BlockSpec((1, 1, d, f), lambda i, j: (layer, which, 0, 0), pipeline_mode=once),
            pl.BlockSpec((1, 1, f, d), lambda i, j: (layer, which, 0, 0), pipeline_mode=once),
        ],
        out_specs=pl.BlockSpec((1, tm, d), lambda i, j: (i, j, 0)),
        out_shape=jax.ShapeDtypeStruct((b, s, d), F32),
        compiler_params=pltpu.CompilerParams(
            dimension_semantics=("arbitrary", "arbitrary"),
            vmem_limit_bytes=_vmem_limit(3 * d * f * 2 + 4 * tm * d * 4 + 6 * tm * step * 4
                                         + 8 * 1024 * 1024)),
        name="swiglu_half_step",
    )(x, mods, pre_norm, post_norm, wg, wu, wd)


def _proj_kernel(x_ref, mod_ref, pre_ref, cos_ref, sin_ref, w_ref, wt_ref, *out_refs,
                 n_row_outs, norm_of, t_scales):
    x = x_ref[0]
    shift = mod_ref[0, 0, 3:4, :]
    scale = mod_ref[0, 0, 4:5, :]
    h = _prenorm_mod(x, pre_ref[0, 1:2, :], shift, scale).astype(BF16)
    cos_f = cos_ref[0]
    sin_s = sin_ref[0]
    lane = lax.broadcasted_iota(jnp.int32, cos_f.shape, 1) & (HEAD_DIM - 1)
    first_half = lane < ROT_HALF
    norm_refs = out_refs[len(out_refs) - len(norm_of):]
    col = 0
    for n, out_ref in enumerate(out_refs[:n_row_outs]):
        ncols = out_ref.shape[-1]
        y = _dot(h, w_ref[0, :, col:col + ncols])
        squares = []
        for j in range(ncols // LANES):
            blk = _apply_rope(y[:, j * LANES:(j + 1) * LANES], cos_f, sin_s, first_half)
            out_ref[0, :, j * LANES:(j + 1) * LANES] = blk.astype(out_ref.dtype)
            if n in norm_of:
                squares.append((blk * blk).astype(BF16))
        if n in norm_of:
            n_ref = norm_refs[norm_of.index(n)]
            head_of_col = lax.broadcasted_iota(jnp.int32, (ncols, LANES), 0) // HEAD_DIM
            sel = jnp.where(head_of_col == lax.broadcasted_iota(jnp.int32, (ncols, LANES), 1),
                            1.0, 0.0).astype(BF16)
            n_ref[0] = _dot(jnp.concatenate(squares, axis=1), sel)
        col += ncols
    r = 0
    for out_ref, t_scale in zip(out_refs[n_row_outs:len(out_refs) - len(norm_of)], t_scales):
        nrows = out_ref.shape[-2]
        yt = _dot_nt(wt_ref[0, r:r + nrows, :], h)
        if t_scale is not None:
            yt = yt * t_scale
        out_ref[0, 0] = yt.astype(out_ref.dtype)
        r += nrows


def _proj_call(x, mods, pre_norm, cos_f, sin_s, w, wt, layer, wl, row_cols, t_rows, t_dtypes,
               t_scales, norm_of):
    b, s, d = x.shape
    n = w.shape[-1]
    nt = wt.shape[-2]
    tm = ROW_TILE if s % ROW_TILE == 0 else s
    assert all(row_cols[m] // HEAD_DIM <= LANES for m in norm_of)
    kern = functools.partial(_proj_kernel, n_row_outs=len(row_cols), norm_of=norm_of,
                             t_scales=t_scales)
    row = lambda i, j: (i, j, 0)
    once = pl.Buffered(1)
    return pl.pallas_call(
        kern,
        grid=(b, s // tm),
        in_specs=[
            pl.BlockSpec((1, tm, d), row),
            pl.BlockSpec((1, 1, N_MOD, d), lambda i, j: (layer, i, 0, 0)),
            pl.BlockSpec((1, 3, d), lambda i, j: (layer, 0, 0)),
            pl.BlockSpec((1, tm, LANES), row),
            pl.BlockSpec((1, tm, LANES), row),
            pl.BlockSpec((1, d, n), lambda i, j: (wl, 0, 0), pipeline_mode=once),
            pl.BlockSpec((1, nt, d), lambda i, j: (wl, 0, 0), pipeline_mode=once),
        ],
        out_specs=([pl.BlockSpec((1, tm, nc), row) for nc in row_cols]
                   + [pl.BlockSpec((1, 1, nr, tm), lambda i, j: (i, j, 0, 0)) for nr in t_rows]
                   + [pl.BlockSpec((1, tm, LANES), row) for _ in norm_of]),
        out_shape=([jax.ShapeDtypeStruct((b, s, nc), BF16) for nc in row_cols]
                   + [jax.ShapeDtypeStruct((b, s // tm, nr, tm), dt)
                      for nr, dt in zip(t_rows, t_dtypes)]
                   + [jax.ShapeDtypeStruct((b, s, LANES), F32) for _ in norm_of]),
        compiler_params=pltpu.CompilerParams(
            dimension_semantics=("arbitrary", "arbitrary"),
            vmem_limit_bytes=_vmem_limit(d * (n + nt) * 2 + 2 * tm * d * 4 + 6 * tm * (n + nt) * 2
                                         + 6 * tm * 1024 * 4 + 8 * 1024 * 1024)),
        name="mixer_in_proj",
    )(x, mods, pre_norm, cos_f, sin_s, w, wt)


def _outproj_kernel(o_ref, x_ref, mod_ref, post_ref, w_ref, out_ref):
    y = _dot(o_ref[0], w_ref[0])
    gate = mod_ref[0, 0, 5:6, :]
    out_ref[0] = x_ref[0] + gate * (_rms(y) * post_ref[0, 1:2, :])


def _outproj_call(o, x, mods, post_norm, w, layer, wl):
    b, s, d = x.shape
    k = o.shape[-1]
    tm = ROW_TILE if s % ROW_TILE == 0 else s
    row = lambda i, j: (i, j, 0)
    return pl.pallas_call(
        _outproj_kernel,
        grid=(b, s // tm),
        in_specs=[
            pl.BlockSpec((1, tm, k), row),
            pl.BlockSpec((1, tm, d), row),
            pl.BlockSpec((1, 1, N_MOD, d), lambda i, j: (layer, i, 0, 0)),
            pl.BlockSpec((1, 3, d), lambda i, j: (layer, 0, 0)),
            pl.BlockSpec((1, k, d), lambda i, j: (wl, 0, 0), pipeline_mode=pl.Buffered(1)),
        ],
        out_specs=pl.BlockSpec((1, tm, d), row),
        out_shape=jax.ShapeDtypeStruct((b, s, d), F32),
        compiler_params=pltpu.CompilerParams(
            dimension_semantics=("arbitrary", "arbitrary"),
            vmem_limit_bytes=_vmem_limit(k * d * 2 + 8 * tm * d * 4 + 8 * 1024 * 1024)),
        name="mixer_out_proj",
    )(o, x, mods, post_norm, w)


def _issue_pipelined(n_tiles, pieces_of, finish, n_bufs):
    for t in range(n_tiles):
        if t >= n_bufs:
            finish(t - n_bufs)
        for run in pieces_of(t):
            run()
    for t in range(max(0, n_tiles - n_bufs), n_tiles):
        finish(t)


def _online_softmax_step(s_ref, p_ref, tile_max, vt, m_ref, l_ref, acc_ref, slot, interleave=()):
    tk = s_ref.shape[0]
    strips = list(range(0, tk, SOFTMAX_STRIP))
    due = {}
    for j, run in enumerate(interleave):
        due.setdefault((j * len(strips)) // len(interleave), []).append(run)
    m8 = functools.reduce(jnp.maximum, tile_max)
    m_prev = m_ref[slot]
    m_new = jnp.maximum(m_prev, jnp.max(m8, axis=0, keepdims=True))
    alpha = jnp.exp2(m_prev - m_new)
    l8 = None
    for n, r in enumerate(strips):
        for run in due.get(n, ()):
            run()
        p = jnp.exp2(s_ref[r:r + SOFTMAX_STRIP, :] - m_new)
        p_ref[r:r + SOFTMAX_STRIP, :] = p.astype(BF16)
        if l_ref is not None:
            f = _fold_rows(p, jnp.sum)
            l8 = f if l8 is None else l8 + f
    if l_ref is not None:
        l_ref[slot] = alpha * l_ref[slot] + l8
    acc_ref[slot] = alpha * acc_ref[slot] + _dot(vt, p_ref[...])
    m_ref[slot] = m_new


def _diff_attn_kernel(lam_ref, sub_ref, q1_ref, q2_ref, qn_ref, k1_ref, k2_ref, kn_ref, vt_ref,
                      o_ref, m_ref, acc_ref, s0_ref, s1_ref, p0_ref, p1_ref, *,
                      tq, tk, lam_init, pairs):
    s_bufs = (s0_ref, s1_ref)
    p_bufs = (p0_ref, p1_ref)
    i = pl.program_id(2)
    lane = lax.broadcasted_iota(jnp.int32, (tq, LANES), 1)
    low = lane < HEAD_DIM
    v_dim = 2 * HEAD_DIM
    n_slots = 4 * pairs

    def slot_parts(slot):
        return slot // 4, (slot % 4) // 2, slot % 2

    heads = 2 * pairs * pl.num_programs(1)
    q2_t = qn_ref[0].T
    k2_lanes = jnp.max(kn_ref[0], axis=0, keepdims=True)
    row_id = lax.broadcasted_iota(jnp.int32, q2_t.shape, 0)
    lane_id = lax.broadcasted_iota(jnp.int32, k2_lanes.shape, 1)
    qms = []
    bounds = []
    for slot in range(n_slots):
        pair, comp, half = slot_parts(slot)
        q = (q1_ref, q2_ref)[comp][0, :, pair * LANES:(pair + 1) * LANES]
        zero = jnp.zeros_like(q)
        qms.append(jnp.where(low, q, zero) if half == 0 else jnp.where(low, zero, q))
        head = comp * heads + 2 * pairs * pl.program_id(1) + 2 * pair + half
        k2_max = jnp.max(jnp.where(lane_id == head, k2_lanes, 0.0))
        q2 = jnp.sum(jnp.where(row_id == head, q2_t, 0.0), axis=0, keepdims=True)
        bounds.append(jnp.sqrt(q2 * k2_max) * BOUND_SAFETY)
    k_refs = (k1_ref, k2_ref)
    ones_rows = jnp.ones((DENOM_ROWS, tk), BF16)

    def chunk_operands(kc):
        off = pl.multiple_of(kc * tk, tk)
        ks = [k_r[0, pl.ds(off, tk), :] for k_r in k_refs]
        vt = vt_ref[0, kc]
        vts = [jnp.concatenate([vt[h * v_dim:(h + 1) * v_dim, :], ones_rows], axis=0)
               for h in range(2 * pairs)]
        return ks, vts

    def masked_scores(ks, slot, r, keep):
        pair, comp, _ = slot_parts(slot)
        k = ks[comp][r:r + SCORE_PIECE, pair * LANES:(pair + 1) * LANES]
        s = _dot_nt(k, qms[slot])
        if keep is not None:
            s = jnp.where(keep[r:r + SCORE_PIECE, :], s, MASKED)
        return s

    def causal_keep(kc):
        key_idx = kc * tk + lax.broadcasted_iota(jnp.int32, (tk, tq), 0)
        qry_idx = i * tq + lax.broadcasted_iota(jnp.int32, (tk, tq), 1)
        return key_idx <= qry_idx

    def sweep(step):
        n_full = (i * tq) // tk

        def full_body(kc, carry):
            step(kc, None)
            return carry

        lax.fori_loop(0, n_full, full_body, 0)
        for d in range(tq // tk):
            step(n_full + d, causal_keep(n_full + d))

    def fast_step(kc, keep):
        ks, vts = chunk_operands(kc)

        def prob_pieces(slot):
            buf = p_bufs[slot % len(p_bufs)]

            def piece(r):
                def run():
                    s = masked_scores(ks, slot, r, keep)
                    buf[r:r + SCORE_PIECE, :] = jnp.exp2(s - bounds[slot]).astype(BF16)
                return run

            return [piece(r) for r in range(0, tk, SCORE_PIECE)]

        def accumulate(slot):
            pair, _, half = slot_parts(slot)
            acc_ref[slot] = acc_ref[slot] + _dot(vts[2 * pair + half],
                                                 p_bufs[slot % len(p_bufs)][...])

        _issue_pipelined(n_slots, prob_pieces, accumulate, len(p_bufs))

    acc_ref[...] = jnp.zeros(acc_ref.shape, F32)
    sweep(fast_step)
    denom_min = functools.reduce(
        jnp.minimum, [acc_ref[slot, v_dim:v_dim + 1, :] for slot in range(n_slots)])

    @pl.when(jnp.logical_not(jnp.min(denom_min) > DENOM_FLOOR))
    def _():
        def exact_step(kc, keep):
            ks, vts = chunk_operands(kc)

            def score_pieces(slot):
                buf = s_bufs[slot % 2]
                maxes = []

                def piece(r):
                    def run():
                        s = masked_scores(ks, slot, r, keep)
                        buf[r:r + SCORE_PIECE, :] = s
                        maxes.append(_fold_rows(s, jnp.max))
                    return run

                return [piece(r) for r in range(0, tk, SCORE_PIECE)], maxes

            pieces, maxes = score_pieces(0)
            for run in pieces:
                run()
            for slot in range(n_slots):
                nxt, nxt_maxes = score_pieces(slot + 1) if slot + 1 < n_slots else ((), None)
                pair, _, half = slot_parts(slot)
                _online_softmax_step(s_bufs[slot % 2], p_bufs[slot % 2], maxes, vts[2 * pair + half],
                                     m_ref, None, acc_ref, slot, nxt)
                maxes = nxt_maxes

        m_ref[...] = jnp.full(m_ref.shape, MASKED, F32)
        acc_ref[...] = jnp.zeros(acc_ref.shape, F32)
        sweep(exact_step)

    lam = lam_ref[0]
    lam_val = (jnp.exp(jnp.sum(lam[0:1] * lam[1:2], axis=1, keepdims=True))
               - jnp.exp(jnp.sum(lam[2:3] * lam[3:4], axis=1, keepdims=True)) + lam_init)
    for pair in range(pairs):
        for half in range(2):
            s1 = 4 * pair + half
            s2 = s1 + 2
            o1 = acc_ref[s1, :v_dim, :] / acc_ref[s1, v_dim:v_dim + 1, :]
            o2 = acc_ref[s2, :v_dim, :] / acc_ref[s2, v_dim:v_dim + 1, :]
            o = (o1 - lam_val * o2).T
            o = _rms(o) * sub_ref[0] * (1.0 - lam_init)
            head = 2 * pair + half
            o_ref[0, :, head * v_dim:(head + 1) * v_dim] = o.astype(o_ref.dtype)


def _diff_attn_call(q, k, vt, qn2, kn2, lam, subln, wl, lam_init):
    b, s, qk = q.shape
    n_pairs = qk // (2 * LANES)
    pairs = 2 if n_pairs % 2 == 0 else 1
    groups = n_pairs // pairs
    nk, _, tk = vt.shape[1:]
    tq = tk
    v_dim = 2 * HEAD_DIM
    n_slots = 4 * pairs
    kern = functools.partial(_diff_attn_kernel, tq=tq, tk=tk, lam_init=lam_init, pairs=pairs)
    return pl.pallas_call(
        kern,
        grid=(b, groups, s // tq),
        in_specs=[
            pl.BlockSpec((1, 4, HEAD_DIM), lambda bi, g, i: (wl, 0, 0)),
            pl.BlockSpec((1, 1, v_dim), lambda bi, g, i: (wl, 0, 0)),
            pl.BlockSpec((1, tq, pairs * LANES), lambda bi, g, i: (bi, i, g)),
            pl.BlockSpec((1, tq, pairs * LANES), lambda bi, g, i: (bi, i, groups + g)),
            pl.BlockSpec((1, tq, LANES), lambda bi, g, i: (bi, i, 0)),
            pl.BlockSpec((1, s, pairs * LANES), lambda bi, g, i: (bi, 0, g)),
            pl.BlockSpec((1, s, pairs * LANES), lambda bi, g, i: (bi, 0, groups + g)),
            pl.BlockSpec((1, s, LANES), lambda bi, g, i: (bi, 0, 0)),
            pl.BlockSpec((1, nk, 2 * pairs * v_dim, tk), lambda bi, g, i: (bi, 0, g, 0)),
        ],
        out_specs=pl.BlockSpec((1, tq, 2 * pairs * v_dim), lambda bi, g, i: (bi, i, g)),
        out_shape=jax.ShapeDtypeStruct((b, s, n_pairs * 2 * v_dim), BF16),
        scratch_shapes=[
            pltpu.VMEM((n_slots, 1, tq), F32),
            pltpu.VMEM((n_slots, v_dim + DENOM_ROWS, tq), F32),
            pltpu.VMEM((tk, tq), F32),
            pltpu.VMEM((tk, tq), F32),
            pltpu.VMEM((tk, tq), BF16),
            pltpu.VMEM((tk, tq), BF16),
        ],
        compiler_params=pltpu.CompilerParams(
            dimension_semantics=("arbitrary", "arbitrary", "arbitrary"),
            vmem_limit_bytes=_vmem_limit(40 * 1024 * 1024)),
        name="diff_attention",
    )(lam, subln.reshape(subln.shape[0], 1, v_dim), q, q, qn2, k, k, kn2, vt)


def _dsa_attn_kernel(q_ref, qi_ref, wi_ref, qn2_ref, kd_ref, kid_ref, kn2_ref, vt_ref, o_ref,
                     sc_ref, m_ref, acc_ref, bias_ref, s0_ref, s1_ref, p0_ref, p1_ref, *,
                     tq, tkc, top_k, n_heads, n_idx_heads):
    s_bufs = (s0_ref, s1_ref)
    p_bufs = (p0_ref, p1_ref)
    i = pl.program_id(1)
    nk = (i * tq + tq + tkc - 1) // tkc
    lane = lax.broadcasted_iota(jnp.int32, (tq, LANES), 1)
    low = lane < HEAD_DIM
    key0 = lax.broadcasted_iota(jnp.int32, (tkc, tq), 0)
    qry = i * tq + lax.broadcasted_iota(jnp.int32, (tkc, tq), 1)
    k_f = float(top_k)

    def head_operand(ref, h):
        pair = ref[0, :, (h // 2) * LANES:(h // 2 + 1) * LANES]
        zero = jnp.zeros_like(pair)
        return jnp.where(low, pair, zero) if h % 2 == 0 else jnp.where(low, zero, pair)

    def key_chunk(ref, kc):
        return ref[0, pl.ds(pl.multiple_of(kc * tkc, tkc), tkc), :]

    wt = wi_ref[0, 0]
    qims = [head_operand(qi_ref, h) for h in range(n_idx_heads)]

    def fold_count(hit):
        return jnp.sum(jnp.where(hit, 1.0, 0.0).reshape(tkc // COUNT_ROWS, COUNT_ROWS, tq), axis=0)

    def score_body(kc, cnts, on_diagonal):
        kik = key_chunk(kid_ref, kc)
        score = jnp.zeros((tkc, tq), F32)
        for h in range(n_idx_heads):
            score = score + wt[h:h + 1, :] * jnp.maximum(_dot_nt(kik, qims[h]), 0.0)
        if on_diagonal:
            score = jnp.where(kc * tkc + key0 <= qry, score, -jnp.inf)
        sc_ref[kc] = score
        return cnts[0] + fold_count(score >= 0.0), cnts[1] + fold_count(score > 0.0)

    n_below = (i * tq) // tkc
    zero_counts = lax.fori_loop(0, n_below, functools.partial(score_body, on_diagonal=False),
                                (jnp.zeros((COUNT_ROWS, tq), F32),) * 2)
    zero_counts = lax.fori_loop(n_below, nk, functools.partial(score_body, on_diagonal=True),
                                zero_counts)
    n_ge0, n_gt0 = [jnp.sum(c, axis=0, keepdims=True) for c in zero_counts]

    def key_to_float(u):
        key = u ^ jnp.int32(INT_MIN)
        bits = jnp.where(key >= 0, key, key ^ jnp.int32(0x7FFFFFFF))
        return lax.bitcast_convert_type(bits, F32)

    def counts(*preds):
        def body(kc, cnts):
            sc = sc_ref[kc]
            return tuple(c + fold_count(pred(sc)) for c, pred in zip(cnts, preds))
        cnts = lax.fori_loop(0, nk, body, (jnp.zeros((COUNT_ROWS, tq), F32),) * len(preds))
        return [jnp.sum(c, axis=0, keepdims=True) for c in cnts]

    few_keys = i * tq + lax.broadcasted_iota(jnp.int32, (1, tq), 1) < top_k - 1
    zero_thr = jnp.logical_and(n_gt0 < k_f, n_ge0 >= k_f)
    zero_tied = jnp.logical_and(zero_thr, n_ge0 > k_f)

    def bit_cond(state):
        step, _, settled = state
        return jnp.logical_and(step < 32, jnp.min(settled) < 0.5)

    def bit_body(state):
        step, prefix, settled = state
        cand_u = prefix | lax.shift_left(jnp.int32(1), 31 - step)
        cand = key_to_float(cand_u)
        (cnt,) = counts(lambda sc: sc >= cand)
        settled = jnp.maximum(settled, jnp.where(cnt == k_f, 1.0, 0.0))
        return step + 1, jnp.where(cnt >= k_f, cand_u, prefix), settled

    steps, prefix, settled = lax.while_loop(
        bit_cond, bit_body,
        (jnp.int32(1), jnp.where(n_ge0 >= k_f, jnp.int32(INT_MIN), jnp.int32(0)),
         jnp.where(jnp.logical_or(few_keys, zero_thr), 1.0, 0.0)))
    thr_key = jnp.maximum(prefix ^ jnp.int32(INT_MIN), jnp.int32(KEY_NEG_FLT_MAX))
    thr = lax.bitcast_convert_type(
        jnp.where(thr_key >= 0, thr_key, thr_key ^ jnp.int32(0x7FFFFFFF)), F32)

    any_tie = jnp.logical_or(jnp.max(jnp.where(zero_tied, 1.0, 0.0)) > 0.5,
                             jnp.min(settled) < 0.5)

    @pl.when(any_tie)
    def _():
        (n_gt,) = counts(lambda sc: sc > thr)
        room = k_f - n_gt
        r_i = lax.broadcasted_iota(jnp.int32, (tkc, tkc), 0)
        c_i = lax.broadcasted_iota(jnp.int32, (tkc, tkc), 1)
        prefix_op = jnp.where(r_i >= c_i, 1.0, 0.0).astype(BF16)

        def drop_body(kc, seen):
            sc = sc_ref[kc]
            tied = sc == thr
            rank = seen + _dot(prefix_op, jnp.where(tied, 1.0, 0.0).astype(BF16))
            sc_ref[kc] = jnp.where(jnp.logical_and(tied, rank > room), -jnp.inf, sc)
            return rank[tkc - 1:tkc, :]

        lax.fori_loop(0, nk, drop_body, jnp.zeros((1, tq), F32))

    ones_rows = jnp.ones((DENOM_ROWS, tkc), BF16)

    def chunk_operands(kc):
        kd = key_chunk(kd_ref, kc)
        vt = jnp.concatenate([vt_ref[0, kc], ones_rows], axis=0)
        bias_ref[...] = jnp.where(sc_ref[kc] >= thr, 0.0, MASKED)
        return kd, vt

    k2_max = jnp.max(kn2_ref[0])
    q2_t = qn2_ref[0].T
    bounds = [jnp.sqrt(q2_t[h:h + 1, :] * k2_max) * BOUND_SAFETY for h in range(n_heads)]
    acc_ref[...] = jnp.zeros(acc_ref.shape, F32)

    def fast_body(kc, carry):
        kd, vt = chunk_operands(kc)

        def prob_pieces(h):
            qm = head_operand(q_ref, h)
            buf = p_bufs[h % len(p_bufs)]

            def piece(r):
                def run():
                    s = bias_ref[r:r + SCORE_PIECE, :] + _dot_nt(kd[r:r + SCORE_PIECE, :], qm)
                    buf[r:r + SCORE_PIECE, :] = jnp.exp2(s - bounds[h]).astype(BF16)
                return run

            return [piece(r) for r in range(0, tkc, SCORE_PIECE)]

        def accumulate(h):
            acc_ref[h] = acc_ref[h] + _dot(vt, p_bufs[h % len(p_bufs)][...])

        _issue_pipelined(n_heads, prob_pieces, accumulate, len(p_bufs))
        return carry

    lax.fori_loop(0, nk, fast_body, 0)
    denom_min = functools.reduce(
        jnp.minimum, [acc_ref[h, HEAD_DIM:HEAD_DIM + 1, :] for h in range(n_heads)])

    @pl.when(jnp.logical_not(jnp.min(denom_min) > DENOM_FLOOR))
    def _():
        _dsa_exact_attention(q_ref, kd_ref, vt_ref, sc_ref, m_ref, acc_ref, bias_ref, s_bufs, p_bufs,
                             thr, nk, tkc, n_heads, head_operand, key_chunk, ones_rows)

    for j in range(n_heads // 2):
        halves = [acc_ref[h, :HEAD_DIM, :] / acc_ref[h, HEAD_DIM:HEAD_DIM + 1, :]
                  for h in (2 * j, 2 * j + 1)]
        o_ref[0, :, j * LANES:(j + 1) * LANES] = jnp.concatenate(halves, axis=0).T.astype(o_ref.dtype)


def _dsa_exact_attention(q_ref, kd_ref, vt_ref, sc_ref, m_ref, acc_ref, bias_ref, s_bufs, p_bufs,
                         thr, nk, tkc, n_heads, head_operand, key_chunk, ones_rows):
    m_ref[...] = jnp.full(m_ref.shape, MASKED, F32)
    acc_ref[...] = jnp.zeros(acc_ref.shape, F32)

    def attn_body(kc, carry):
        kd = key_chunk(kd_ref, kc)
        vt = jnp.concatenate([vt_ref[0, kc], ones_rows], axis=0)
        bias_ref[...] = jnp.where(sc_ref[kc] >= thr, 0.0, MASKED)

        def score_pieces(h):
            qm = head_operand(q_ref, h)
            buf = s_bufs[h % 2]
            maxes = []

            def piece(r):
                def run():
                    s = bias_ref[r:r + SCORE_PIECE, :] + _dot_nt(kd[r:r + SCORE_PIECE, :], qm)
                    buf[r:r + SCORE_PIECE, :] = s
                    maxes.append(_fold_rows(s, jnp.max))
                return run

            return [piece(r) for r in range(0, tkc, SCORE_PIECE)], maxes

        pieces, maxes = score_pieces(0)
        for run in pieces:
            run()
        for h in range(n_heads):
            nxt, nxt_maxes = score_pieces(h + 1) if h + 1 < n_heads else ((), None)
            _online_softmax_step(s_bufs[h % 2], p_bufs[h % 2], maxes, vt, m_ref, None, acc_ref, h,
                                 nxt)
            maxes = nxt_maxes
        return carry

    lax.fori_loop(0, nk, attn_body, 0)


def _dsa_attn_call(q, qi, wit, qn2, kd, kid, kn2, vt, n_idx_heads):
    b, s, a_q = q.shape
    n_heads = a_q // HEAD_DIM
    top_k = min(TOPK_MAX, s // 4)
    nkc, _, tkc = vt.shape[1:]
    tq = tkc
    per = tkc // tq
    kern = functools.partial(_dsa_attn_kernel, tq=tq, tkc=tkc, top_k=top_k, n_heads=n_heads,
                             n_idx_heads=n_idx_heads)
    rowq = lambda bi, i: (bi, i, 0)
    allk = lambda bi, i: (bi, 0, 0)
    return pl.pallas_call(
        kern,
        grid=(b, s // tq),
        in_specs=[
            pl.BlockSpec((1, tq, a_q), rowq),
            pl.BlockSpec((1, tq, qi.shape[-1]), rowq),
            pl.BlockSpec((1, 1, wit.shape[2], tq), lambda bi, i: (bi, i // per, 0, i % per)),
            pl.BlockSpec((1, tq, LANES), rowq),
            pl.BlockSpec((1, s, LANES), allk),
            pl.BlockSpec((1, s, LANES), allk),
            pl.BlockSpec((1, s, LANES), allk),
            pl.BlockSpec((1, nkc, HEAD_DIM, tkc), lambda bi, i: (bi, 0, 0, 0)),
        ],
        out_specs=pl.BlockSpec((1, tq, a_q), rowq),
        out_shape=jax.ShapeDtypeStruct((b, s, a_q), BF16),
        scratch_shapes=[
            pltpu.VMEM((nkc, tkc, tq), F32),
            pltpu.VMEM((n_heads, 1, tq), F32),
            pltpu.VMEM((n_heads, HEAD_DIM + DENOM_ROWS, tq), F32),
            pltpu.VMEM((tkc, tq), F32),
            pltpu.VMEM((tkc, tq), F32),
            pltpu.VMEM((tkc, tq), F32),
            pltpu.VMEM((tkc, tq), BF16),
            pltpu.VMEM((tkc, tq), BF16),
        ],
        compiler_params=pltpu.CompilerParams(
            dimension_semantics=("arbitrary", "arbitrary"),
            vmem_limit_bytes=_vmem_limit(6 * s * LANES * 2 + tq * s * 4 + 24 * 1024 * 1024)),
        name="dsa_attention",
    )(q, qi, wit, qn2, kd, kid, kn2, vt)


def _pad_to(w, axis, mult):
    n = w.shape[axis]
    pad = (-n) % mult
    if pad == 0:
        return w
    widths = [(0, 0)] * w.ndim
    widths[axis] = (0, pad)
    return jnp.pad(w, widths)


def _prep_dsa_w_in(w, a_q, n_idx_heads):
    o1 = a_q
    o2 = o1 + HEAD_DIM
    o3 = o2 + HEAD_DIM
    o4 = o3 + n_idx_heads * HEAD_DIM
    o5 = o4 + HEAD_DIM
    wq = w[..., :o1] * ATTN_Q_SCALE
    wk, wv, wqi, wki, wwi = w[..., o1:o2], w[..., o2:o3], w[..., o3:o4], w[..., o4:o5], w[..., o5:]
    rows = jnp.concatenate([wq, wqi, wk, wk, wki, wki], axis=-1).astype(BF16)
    cols = jnp.concatenate([wv, _pad_to(wwi, 2, 2 * SUBLANES)], axis=-1)
    return rows, jnp.swapaxes(cols, 1, 2).astype(BF16)


def kernel(x, c, positions, ada_w, ada_b, pre_norm, post_norm, ffn_w_gate, ffn_w_up, ffn_w_down,
           dsa_w_in, dsa_w_out, diff_w_in, diff_w_out, diff_lambda, diff_subln):
    b, s, d = x.shape
    depth = ada_w.shape[0]

    mods = _mod_call(c, ada_w, ada_b).reshape(depth, b, N_MOD, d)
    cos_f, sin_s = _rope_tables(positions)

    wg = _pad_to(ffn_w_gate, 3, MXU_DIM).astype(BF16)
    wu = _pad_to(ffn_w_up, 3, MXU_DIM).astype(BF16)
    wd = _pad_to(ffn_w_down, 2, MXU_DIM).astype(BF16)

    a_q = dsa_w_out.shape[1]
    n_idx_heads = (dsa_w_in.shape[2] - a_q - 3 * HEAD_DIM) // (HEAD_DIM + 1)
    dsa_in, dsa_in_t = _prep_dsa_w_in(dsa_w_in, a_q, n_idx_heads)
    dsa_out = dsa_w_out.astype(BF16)
    idx_q = n_idx_heads * HEAD_DIM
    wi_scale = n_idx_heads ** -0.5 * HEAD_DIM ** -0.5

    b_out = diff_w_out.shape[1]
    b_qk = (diff_w_in.shape[2] - b_out) // 2
    diff_in = jnp.concatenate(
        [diff_w_in[..., :b_qk] * ATTN_Q_SCALE, diff_w_in[..., b_qk:2 * b_qk]],
        axis=-1).astype(BF16)
    diff_in_t = jnp.swapaxes(diff_w_in[..., 2 * b_qk:], 1, 2).astype(BF16)
    diff_out = diff_w_out.astype(BF16)

    for i in range(depth):
        x = _ffn_call(x, mods, pre_norm, post_norm, wg, wu, wd, i, 0, 0)
        j = i // 2
        if i % 2 == 0:
            q, qi, kd, kid, vt, wit, qn2, kn2 = _proj_call(
                x, mods, pre_norm, cos_f, sin_s, dsa_in, dsa_in_t, i, j,
                row_cols=(a_q, idx_q, LANES, LANES),
                t_rows=(HEAD_DIM, 2 * SUBLANES), t_dtypes=(BF16, F32), t_scales=(None, wi_scale),
                norm_of=(0, 2))
            o = _dsa_attn_call(q, qi, wit, qn2, kd, kid, kn2, vt, n_idx_heads)
            x = _outproj_call(o, x, mods, post_norm, dsa_out, i, j)
        else:
            lam_init = 0.8 - 0.6 * math.exp(-0.3 * i)
            q, k, vt, qn2, kn2 = _proj_call(
                x, mods, pre_norm, cos_f, sin_s, diff_in, diff_in_t, i, j,
                row_cols=(b_qk, b_qk), t_rows=(b_out,), t_dtypes=(BF16,), t_scales=(None,),
                norm_of=(0, 1))
            o = _diff_attn_call(q, k, vt, qn2, kn2, diff_lambda, diff_subln, j, lam_init)
            x = _outproj_call(o, x, mods, post_norm, diff_out, i, j)
        x = _ffn_call(x, mods, pre_norm, post_norm, wg, wu, wd, i, 2, 1)
    return x
```

```python
import functools
import math

import jax
import jax.numpy as jnp
from jax import lax
from jax.experimental import pallas as pl
from jax.experimental.pallas import tpu as pltpu

F32 = jnp.float32
BF16 = jnp.bfloat16

EPS = 1e-6
ROPE_THETA = 500000.0
HEAD_DIM = 64
ROT_DIM = HEAD_DIM // 4
ROT_HALF = ROT_DIM // 2
TOPK_MAX = 256
FFN_HALF = 0.5
N_MOD = 9

LANES = 128
SUBLANES = 8
MXU_DIM = 256
VMEM_BYTES_V7X = 64 * 1024 * 1024

ROW_TILE = 512
FFN_SUB_ROWS = 512
DENOM_ROWS = 16
SCORE_PIECE = 128
SOFTMAX_STRIP = 128
COUNT_ROWS = 8

LOG2E = math.log2(math.e)
ATTN_Q_SCALE = HEAD_DIM ** -0.5 * LOG2E

MASKED = -1e30
BOUND_SAFETY = 1.03
DENOM_FLOOR = 2.0 ** -40
INT_MIN = -(2 ** 31)
KEY_NEG_FLT_MAX = -2139095040


def _vmem_limit(nbytes):
    return int(min(nbytes, VMEM_BYTES_V7X - 6 * 1024 * 1024))


def _dot(a, b):
    return jnp.dot(a, b, preferred_element_type=F32)


def _dot_nt(a, b):
    return lax.dot_general(a, b, (((1,), (1,)), ((), ())), preferred_element_type=F32)


def _rms(y):
    return y * lax.rsqrt(jnp.mean(y * y, axis=-1, keepdims=True) + EPS)


def _prenorm_mod(x, gain, shift, scale):
    return (_rms(x) * gain) * (1.0 + scale) + shift


def _silu(g):
    return g / (1.0 + jnp.exp(-g))


def _fold_rows(x, op):
    return op(x.reshape(x.shape[0] // SUBLANES, SUBLANES, x.shape[1]), axis=0)


def _mod_kernel(c_ref, w_ref, b_ref, o_ref):
    cond = _silu(c_ref[...])
    w = w_ref[0]
    c_hi = cond.astype(BF16)
    c_lo = (cond - c_hi.astype(F32)).astype(BF16)
    w_hi = w.astype(BF16)
    w_lo = (w - w_hi.astype(F32)).astype(BF16)
    o_ref[0] = _dot(c_hi, w_hi) + _dot(c_lo, w_hi) + _dot(c_hi, w_lo) + b_ref[0]


def _mod_call(c, ada_w, ada_b):
    depth, d, n = ada_w.shape
    b = c.shape[0]
    tn = 1024 if n % 1024 == 0 else n
    return pl.pallas_call(
        _mod_kernel,
        grid=(depth, n // tn),
        in_specs=[
            pl.BlockSpec((b, d), lambda l, j: (0, 0)),
            pl.BlockSpec((1, d, tn), lambda l, j: (l, 0, j)),
            pl.BlockSpec((1, 1, tn), lambda l, j: (l, 0, j)),
        ],
        out_specs=pl.BlockSpec((1, b, tn), lambda l, j: (l, 0, j)),
        out_shape=jax.ShapeDtypeStruct((depth, b, n), F32),
        compiler_params=pltpu.CompilerParams(
            dimension_semantics=("arbitrary", "arbitrary"),
            vmem_limit_bytes=_vmem_limit(40 * 1024 * 1024)),
        name="adaln_mod",
    )(c, ada_w, ada_b.reshape(depth, 1, n))


def _rope_kernel(ang_ref, cos_ref, sin_ref):
    a = ang_ref[0]
    lane = lax.broadcasted_iota(jnp.int32, a.shape, 1) & (HEAD_DIM - 1)
    rot = lane < ROT_DIM
    cos_ref[0] = jnp.where(rot, jnp.cos(a), 1.0)
    sin_ref[0] = jnp.where(rot, jnp.sin(a), 0.0)


def _rope_tables(positions):
    b, s = positions.shape
    inv = ROPE_THETA ** (-jnp.arange(0, ROT_DIM, 2, dtype=F32) / ROT_DIM)
    head = jnp.concatenate([-inv, inv, jnp.zeros((HEAD_DIM - ROT_DIM,), F32)])
    inv_lanes = jnp.tile(head, LANES // HEAD_DIM)
    ang = positions.astype(F32)[..., None] * inv_lanes
    tm = ROW_TILE if s % ROW_TILE == 0 else s
    spec = pl.BlockSpec((1, tm, LANES), lambda i, j: (i, j, 0))
    return pl.pallas_call(
        _rope_kernel,
        grid=(b, s // tm),
        in_specs=[spec],
        out_specs=[spec, spec],
        out_shape=[jax.ShapeDtypeStruct((b, s, LANES), F32)] * 2,
        compiler_params=pltpu.CompilerParams(dimension_semantics=("arbitrary", "arbitrary")),
        name="rope_tables",
    )(ang)


def _apply_rope(y, cos_f, sin_s, first_half):
    partner = jnp.where(first_half, pltpu.roll(y, LANES - ROT_HALF, 1), pltpu.roll(y, ROT_HALF, 1))
    return y * cos_f + partner * sin_s


def _ffn_kernel(x_ref, mod_ref, pre_ref, post_ref, wg_ref, wu_ref, wd_ref, o_ref, *, sub, f_chunks):
    shift = mod_ref[0, 0, 3 * sub:3 * sub + 1, :]
    scale = mod_ref[0, 0, 3 * sub + 1:3 * sub + 2, :]
    gate = mod_ref[0, 0, 3 * sub + 2:3 * sub + 3, :]
    tm = x_ref.shape[1]
    parts = [(r, r + FFN_SUB_ROWS) for r in range(0, tm, FFN_SUB_ROWS)]
    hs = {}
    ys = {}
    for n, (f0, f1) in enumerate(f_chunks):
        for r0, r1 in parts:
            if n == 0:
                hs[r0] = _prenorm_mod(x_ref[0, r0:r1, :], pre_ref[0, sub:sub + 1, :], shift,
                                      scale).astype(BF16)
            g = _dot(hs[r0], wg_ref[0, 0, :, f0:f1])
            u = _dot(hs[r0], wu_ref[0, 0, :, f0:f1])
            a = (_silu(g) * u).astype(BF16)
            part = _dot(a, wd_ref[0, 0, f0:f1, :])
            ys[r0] = part if n == 0 else ys[r0] + part
    for r0, r1 in parts:
        o_ref[0, r0:r1, :] = x_ref[0, r0:r1, :] + (FFN_HALF * gate) * (
            _rms(ys[r0]) * post_ref[0, sub:sub + 1, :])


def _ffn_call(x, mods, pre_norm, post_norm, wg, wu, wd, layer, sub, which):
    b, s, d = x.shape
    f = wg.shape[-1]
    tm = 2 * FFN_SUB_ROWS if s % (2 * FFN_SUB_ROWS) == 0 else FFN_SUB_ROWS
    assert s % tm == 0
    step = 4 * MXU_DIM
    f_chunks = tuple((f0, min(f0 + step, f)) for f0 in range(0, f, step))
    kern = functools.partial(_ffn_kernel, sub=sub, f_chunks=f_chunks)
    once = pl.Buffered(1)
    return pl.pallas_call(
        kern,
        grid=(b, s // tm),
        in_specs=[
            pl.BlockSpec((1, tm, d), lambda i, j: (i, j, 0)),
            pl.BlockSpec((1, 1, N_MOD, d), lambda i, j: (layer, i, 0, 0)),
            pl.BlockSpec((1, 3, d), lambda i, j: (layer, 0, 0)),
            pl.BlockSpec((1, 3, d), lambda i, j: (layer, 0, 0)),
            pl.BlockSpec((1, 1, d, f), lambda i, j: (layer, which, 0, 0), pipeline_mode=once),
            pl.BlockSpec((1, 1, d, f), lambda i, j: (layer, which, 0, 0), pipeline_mode=once),
            pl.BlockSpec((1, 1, f, d), lambda i, j: (layer, which, 0, 0), pipeline_mode=once),
        ],
        out_specs=pl.BlockSpec((1, tm, d), lambda i, j: (i, j, 0)),
        out_shape=jax.ShapeDtypeStruct((b, s, d), F32),
        compiler_params=pltpu.CompilerParams(
            dimension_semantics=("arbitrary", "arbitrary"),
            vmem_limit_bytes=_vmem_limit(3 * d * f * 2 + 4 * tm * d * 4 + 6 * tm * step * 4
                                         + 8 * 1024 * 1024)),
        name="swiglu_half_step",
    )(x, mods, pre_norm, post_norm, wg, wu, wd)


def _proj_kernel(x_ref, mod_ref, pre_ref, cos_ref, sin_ref, w_ref, wt_ref, *out_refs,
                 n_row_outs, norm_of, norm_tile_max, t_scales):
    x = x_ref[0]
    shift = mod_ref[0, 0, 3:4, :]
    scale = mod_ref[0, 0, 4:5, :]
    h = _prenorm_mod(x, pre_ref[0, 1:2, :], shift, scale).astype(BF16)
    cos_f = cos_ref[0]
    sin_s = sin_ref[0]
    lane = lax.broadcasted_iota(jnp.int32, cos_f.shape, 1) & (HEAD_DIM - 1)
    first_half = lane < ROT_HALF
    norm_refs = out_refs[len(out_refs) - len(norm_of):]
    col = 0
    for n, out_ref in enumerate(out_refs[:n_row_outs]):
        ncols = out_ref.shape[-1]
        y = _dot(h, w_ref[0, :, col:col + ncols])
        squares = []
        for j in range(ncols // LANES):
            blk = _apply_rope(y[:, j * LANES:(j + 1) * LANES], cos_f, sin_s, first_half)
            out_ref[0, :, j * LANES:(j + 1) * LANES] = blk.astype(out_ref.dtype)
            if n in norm_of:
                squares.append((blk * blk).astype(BF16))
        if n in norm_of:
            n_ref = norm_refs[norm_of.index(n)]
            head_of_col = lax.broadcasted_iota(jnp.int32, (ncols, LANES), 0) // HEAD_DIM
            sel = jnp.where(head_of_col == lax.broadcasted_iota(jnp.int32, (ncols, LANES), 1),
                            1.0, 0.0).astype(BF16)
            norms = _dot(jnp.concatenate(squares, axis=1), sel)
            if n in norm_tile_max:
                n_ref[0, 0] = jnp.broadcast_to(jnp.max(norms, axis=0, keepdims=True),
                                               (SUBLANES, LANES))
            else:
                n_ref[0] = norms
        col += ncols
    r = 0
    for out_ref, t_scale in zip(out_refs[n_row_outs:len(out_refs) - len(norm_of)], t_scales):
        nrows = out_ref.shape[-2]
        yt = _dot_nt(wt_ref[0, r:r + nrows, :], h)
        if t_scale is not None:
            yt = yt * t_scale
        out_ref[0, 0] = yt.astype(out_ref.dtype)
        r += nrows


def _proj_call(x, mods, pre_norm, cos_f, sin_s, w, wt, layer, wl, row_cols, t_rows, t_dtypes,
               t_scales, norm_of, norm_tile_max):
    b, s, d = x.shape
    n = w.shape[-1]
    nt = wt.shape[-2]
    tm = ROW_TILE if s % ROW_TILE == 0 else s
    assert all(row_cols[m] // HEAD_DIM <= LANES for m in norm_of)
    kern = functools.partial(_proj_kernel, n_row_outs=len(row_cols), norm_of=norm_of,
                             norm_tile_max=norm_tile_max, t_scales=t_scales)
    tile4 = lambda i, j: (i, j, 0, 0)
    row = lambda i, j: (i, j, 0)
    once = pl.Buffered(1)
    return pl.pallas_call(
        kern,
        grid=(b, s // tm),
        in_specs=[
            pl.BlockSpec((1, tm, d), row),
            pl.BlockSpec((1, 1, N_MOD, d), lambda i, j: (layer, i, 0, 0)),
            pl.BlockSpec((1, 3, d), lambda i, j: (layer, 0, 0)),
            pl.BlockSpec((1, tm, LANES), row),
            pl.BlockSpec((1, tm, LANES), row),
            pl.BlockSpec((1, d, n), lambda i, j: (wl, 0, 0), pipeline_mode=once),
            pl.BlockSpec((1, nt, d), lambda i, j: (wl, 0, 0), pipeline_mode=once),
        ],
        out_specs=([pl.BlockSpec((1, tm, nc), row) for nc in row_cols]
                   + [pl.BlockSpec((1, 1, nr, tm), tile4) for nr in t_rows]
                   + [pl.BlockSpec((1, 1, SUBLANES, LANES), tile4) if m in norm_tile_max
                      else pl.BlockSpec((1, tm, LANES), row) for m in norm_of]),
        out_shape=([jax.ShapeDtypeStruct((b, s, nc), BF16) for nc in row_cols]
                   + [jax.ShapeDtypeStruct((b, s // tm, nr, tm), dt)
                      for nr, dt in zip(t_rows, t_dtypes)]
                   + [jax.ShapeDtypeStruct((b, s // tm, SUBLANES, LANES) if m in norm_tile_max
                                           else (b, s, LANES), F32) for m in norm_of]),
        compiler_params=pltpu.CompilerParams(
            dimension_semantics=("arbitrary", "arbitrary"),
            vmem_limit_bytes=_vmem_limit(d * (n + nt) * 2 + 2 * tm * d * 4 + 6 * tm * (n + nt) * 2
                                         + 6 * tm * 1024 * 4 + 8 * 1024 * 1024)),
        name="mixer_in_proj",
    )(x, mods, pre_norm, cos_f, sin_s, w, wt)


def _outproj_kernel(o_ref, x_ref, mod_ref, post_ref, w_ref, out_ref):
    y = _dot(o_ref[0], w_ref[0])
    gate = mod_ref[0, 0, 5:6, :]
    out_ref[0] = x_ref[0] + gate * (_rms(y) * post_ref[0, 1:2, :])


def _outproj_call(o, x, mods, post_norm, w, layer, wl):
    b, s, d = x.shape
    k = o.shape[-1]
    tm = ROW_TILE if s % ROW_TILE == 0 else s
    row = lambda i, j: (i, j, 0)
    return pl.pallas_call(
        _outproj_kernel,
        grid=(b, s // tm),
        in_specs=[
            pl.BlockSpec((1, tm, k), row),
            pl.BlockSpec((1, tm, d), row),
            pl.BlockSpec((1, 1, N_MOD, d), lambda i, j: (layer, i, 0, 0)),
            pl.BlockSpec((1, 3, d), lambda i, j: (layer, 0, 0)),
            pl.BlockSpec((1, k, d), lambda i, j: (wl, 0, 0), pipeline_mode=pl.Buffered(1)),
        ],
        out_specs=pl.BlockSpec((1, tm, d), row),
        out_shape=jax.ShapeDtypeStruct((b, s, d), F32),
        compiler_params=pltpu.CompilerParams(
            dimension_semantics=("arbitrary", "arbitrary"),
            vmem_limit_bytes=_vmem_limit(k * d * 2 + 8 * tm * d * 4 + 8 * 1024 * 1024)),
        name="mixer_out_proj",
    )(o, x, mods, post_norm, w)


def _issue_pipelined(n_tiles, pieces_of, finish, n_bufs):
    for t in range(n_tiles):
        if t >= n_bufs:
            finish(t - n_bufs)
        for run in pieces_of(t):
            run()
    for t in range(max(0, n_tiles - n_bufs), n_tiles):
        finish(t)


def _online_softmax_step(s_ref, p_ref, tile_max, vt, m_ref, l_ref, acc_ref, slot, interleave=()):
    tk = s_ref.shape[0]
    strips = list(range(0, tk, SOFTMAX_STRIP))
    due = {}
    for j, run in enumerate(interleave):
        due.setdefault((j * len(strips)) // len(interleave), []).append(run)
    m8 = functools.reduce(jnp.maximum, tile_max)
    m_prev = m_ref[slot]
    m_new = jnp.maximum(m_prev, jnp.max(m8, axis=0, keepdims=True))
    alpha = jnp.exp2(m_prev - m_new)
    l8 = None
    for n, r in enumerate(strips):
        for run in due.get(n, ()):
            run()
        p = jnp.exp2(s_ref[r:r + SOFTMAX_STRIP, :] - m_new)
        p_ref[r:r + SOFTMAX_STRIP, :] = p.astype(BF16)
        if l_ref is not None:
            f = _fold_rows(p, jnp.sum)
            l8 = f if l8 is None else l8 + f
    if l_ref is not None:
        l_ref[slot] = alpha * l_ref[slot] + l8
    acc_ref[slot] = alpha * acc_ref[slot] + _dot(vt, p_ref[...])
    m_ref[slot] = m_new


def _diff_attn_kernel(lam_ref, sub_ref, q1_ref, q2_ref, qn_ref, k1_ref, k2_ref, kn_ref, vt_ref,
                      o_ref, m_ref, acc_ref, s0_ref, s1_ref, p0_ref, p1_ref, *,
                      tq, tk, lam_init, pairs):
    s_bufs = (s0_ref, s1_ref)
    p_bufs = (p0_ref, p1_ref)
    i = pl.program_id(2)
    lane = lax.broadcasted_iota(jnp.int32, (tq, LANES), 1)
    low = lane < HEAD_DIM
    v_dim = 2 * HEAD_DIM
    n_slots = 4 * pairs

    def slot_parts(slot):
        return slot // 4, (slot % 4) // 2, slot % 2

    heads = 2 * pairs * pl.num_programs(1)
    q2_t = qn_ref[0].T
    k2_lanes = jnp.max(jnp.max(kn_ref[0], axis=0), axis=0, keepdims=True)
    row_id = lax.broadcasted_iota(jnp.int32, q2_t.shape, 0)
    lane_id = lax.broadcasted_iota(jnp.int32, k2_lanes.shape, 1)
    qms = []
    bounds = []
    for slot in range(n_slots):
        pair, comp, half = slot_parts(slot)
        q = (q1_ref, q2_ref)[comp][0, :, pair * LANES:(pair + 1) * LANES]
        zero = jnp.zeros_like(q)
        qms.append(jnp.where(low, q, zero) if half == 0 else jnp.where(low, zero, q))
        head = comp * heads + 2 * pairs * pl.program_id(1) + 2 * pair + half
        k2_max = jnp.max(jnp.where(lane_id == head, k2_lanes, 0.0))
        q2 = jnp.sum(jnp.where(row_id == head, q2_t, 0.0), axis=0, keepdims=True)
        bounds.append(jnp.sqrt(q2 * k2_max) * BOUND_SAFETY)
    k_refs = (k1_ref, k2_ref)
    ones_rows = jnp.ones((DENOM_ROWS, tk), BF16)

    def chunk_operands(kc):
        off = pl.multiple_of(kc * tk, tk)
        ks = [k_r[0, pl.ds(off, tk), :] for k_r in k_refs]
        vt = vt_ref[0, kc]
        vts = [jnp.concatenate([vt[h * v_dim:(h + 1) * v_dim, :], ones_rows], axis=0)
               for h in range(2 * pairs)]
        return ks, vts

    def masked_scores(ks, slot, r, keep):
        pair, comp, _ = slot_parts(slot)
        k = ks[comp][r:r + SCORE_PIECE, pair * LANES:(pair + 1) * LANES]
        s = _dot_nt(k, qms[slot])
        if keep is not None:
            s = jnp.where(keep[r:r + SCORE_PIECE, :], s, MASKED)
        return s

    def causal_keep(kc):
        key_idx = kc * tk + lax.broadcasted_iota(jnp.int32, (tk, tq), 0)
        qry_idx = i * tq + lax.broadcasted_iota(jnp.int32, (tk, tq), 1)
        return key_idx <= qry_idx

    def sweep(step):
        n_full = (i * tq) // tk

        def full_body(kc, carry):
            step(kc, None)
            return carry

        lax.fori_loop(0, n_full, full_body, 0)
        for d in range(tq // tk):
            step(n_full + d, causal_keep(n_full + d))

    def fast_step(kc, keep):
        ks, vts = chunk_operands(kc)

        def prob_pieces(slot):
            buf = p_bufs[slot % len(p_bufs)]

            def piece(r):
                def run():
                    s = masked_scores(ks, slot, r, keep)
                    buf[r:r + SCORE_PIECE, :] = jnp.exp2(s - bounds[slot]).astype(BF16)
                return run

            return [piece(r) for r in range(0, tk, SCORE_PIECE)]

        def accumulate(slot):
            pair, _, half = slot_parts(slot)
            acc_ref[slot] = acc_ref[slot] + _dot(vts[2 * pair + half],
                                                 p_bufs[slot % len(p_bufs)][...])

        _issue_pipelined(n_slots, prob_pieces, accumulate, len(p_bufs))

    acc_ref[...] = jnp.zeros(acc_ref.shape, F32)
    sweep(fast_step)
    denom_min = functools.reduce(
        jnp.minimum, [acc_ref[slot, v_dim:v_dim + 1, :] for slot in range(n_slots)])

    @pl.when(jnp.logical_not(jnp.min(denom_min) > DENOM_FLOOR))
    def _():
        def exact_step(kc, keep):
            ks, vts = chunk_operands(kc)

            def score_pieces(slot):
                buf = s_bufs[slot % 2]
                maxes = []

                def piece(r):
                    def run():
                        s = masked_scores(ks, slot, r, keep)
                        buf[r:r + SCORE_PIECE, :] = s
                        maxes.append(_fold_rows(s, jnp.max))
                    return run

                return [piece(r) for r in range(0, tk, SCORE_PIECE)], maxes

            pieces, maxes = score_pieces(0)
            for run in pieces:
                run()
            for slot in range(n_slots):
                nxt, nxt_maxes = score_pieces(slot + 1) if slot + 1 < n_slots else ((), None)
                pair, _, half = slot_parts(slot)
                _online_softmax_step(s_bufs[slot % 2], p_bufs[slot % 2], maxes, vts[2 * pair + half],
                                     m_ref, None, acc_ref, slot, nxt)
                maxes = nxt_maxes

        m_ref[...] = jnp.full(m_ref.shape, MASKED, F32)
        acc_ref[...] = jnp.zeros(acc_ref.shape, F32)
        sweep(exact_step)

    lam = lam_ref[0]
    lam_val = (jnp.exp(jnp.sum(lam[0:1] * lam[1:2], axis=1, keepdims=True))
               - jnp.exp(jnp.sum(lam[2:3] * lam[3:4], axis=1, keepdims=True)) + lam_init)
    for pair in range(pairs):
        for half in range(2):
            s1 = 4 * pair + half
            s2 = s1 + 2
            o1 = acc_ref[s1, :v_dim, :] / acc_ref[s1, v_dim:v_dim + 1, :]
            o2 = acc_ref[s2, :v_dim, :] / acc_ref[s2, v_dim:v_dim + 1, :]
            o = (o1 - lam_val * o2).T
            o = _rms(o) * sub_ref[0] * (1.0 - lam_init)
            head = 2 * pair + half
            o_ref[0, :, head * v_dim:(head + 1) * v_dim] = o.astype(o_ref.dtype)


def _diff_attn_call(q, k, vt, qn2, kn2, lam, subln, wl, lam_init):
    b, s, qk = q.shape
    n_pairs = qk // (2 * LANES)
    pairs = 2 if n_pairs % 2 == 0 else 1
    groups = n_pairs // pairs
    nk, _, tk = vt.shape[1:]
    tq = tk
    v_dim = 2 * HEAD_DIM
    n_slots = 4 * pairs
    kern = functools.partial(_diff_attn_kernel, tq=tq, tk=tk, lam_init=lam_init, pairs=pairs)
    return pl.pallas_call(
        kern,
        grid=(b, groups, s // tq),
        in_specs=[
            pl.BlockSpec((1, 4, HEAD_DIM), lambda bi, g, i: (wl, 0, 0)),
            pl.BlockSpec((1, 1, v_dim), lambda bi, g, i: (wl, 0, 0)),
            pl.BlockSpec((1, tq, pairs * LANES), lambda bi, g, i: (bi, i, g)),
            pl.BlockSpec((1, tq, pairs * LANES), lambda bi, g, i: (bi, i, groups + g)),
            pl.BlockSpec((1, tq, LANES), lambda bi, g, i: (bi, i, 0)),
            pl.BlockSpec((1, s, pairs * LANES), lambda bi, g, i: (bi, 0, g)),
            pl.BlockSpec((1, s, pairs * LANES), lambda bi, g, i: (bi, 0, groups + g)),
            pl.BlockSpec((1,) + kn2.shape[1:], lambda bi, g, i: (bi, 0, 0, 0)),
            pl.BlockSpec((1, nk, 2 * pairs * v_dim, tk), lambda bi, g, i: (bi, 0, g, 0)),
        ],
        out_specs=pl.BlockSpec((1, tq, 2 * pairs * v_dim), lambda bi, g, i: (bi, i, g)),
        out_shape=jax.ShapeDtypeStruct((b, s, n_pairs * 2 * v_dim), BF16),
        scratch_shapes=[
            pltpu.VMEM((n_slots, 1, tq), F32),
            pltpu.VMEM((n_slots, v_dim + DENOM_ROWS, tq), F32),
            pltpu.VMEM((tk, tq), F32),
            pltpu.VMEM((tk, tq), F32),
            pltpu.VMEM((tk, tq), BF16),
            pltpu.VMEM((tk, tq), BF16),
        ],
        compiler_params=pltpu.CompilerParams(
            dimension_semantics=("arbitrary", "arbitrary", "arbitrary"),
            vmem_limit_bytes=_vmem_limit(40 * 1024 * 1024)),
        name="diff_attention",
    )(lam, subln.reshape(subln.shape[0], 1, v_dim), q, q, qn2, k, k, kn2, vt)


def _dsa_attn_kernel(q_ref, qi_ref, wi_ref, qn2_ref, kd_ref, kid_ref, kn2_ref, vt_ref, o_ref,
                     sc_ref, m_ref, acc_ref, bias_ref, s0_ref, s1_ref, p0_ref, p1_ref, *,
                     tq, tkc, top_k, n_heads, n_idx_heads):
    s_bufs = (s0_ref, s1_ref)
    p_bufs = (p0_ref, p1_ref)
    i = pl.program_id(1)
    nk = (i * tq + tq + tkc - 1) // tkc
    lane = lax.broadcasted_iota(jnp.int32, (tq, LANES), 1)
    low = lane < HEAD_DIM
    key0 = lax.broadcasted_iota(jnp.int32, (tkc, tq), 0)
    qry = i * tq + lax.broadcasted_iota(jnp.int32, (tkc, tq), 1)
    k_f = float(top_k)

    def head_operand(ref, h):
        pair = ref[0, :, (h // 2) * LANES:(h // 2 + 1) * LANES]
        zero = jnp.zeros_like(pair)
        return jnp.where(low, pair, zero) if h % 2 == 0 else jnp.where(low, zero, pair)

    def key_chunk(ref, kc):
        return ref[0, pl.ds(pl.multiple_of(kc * tkc, tkc), tkc), :]

    wt = wi_ref[0, 0]
    qims = [head_operand(qi_ref, h) for h in range(n_idx_heads)]

    def fold_count(hit):
        return jnp.sum(jnp.where(hit, 1.0, 0.0).reshape(tkc // COUNT_ROWS, COUNT_ROWS, tq), axis=0)

    def score_body(kc, cnts, on_diagonal):
        kik = key_chunk(kid_ref, kc)
        score = jnp.zeros((tkc, tq), F32)
        for h in range(n_idx_heads):
            score = score + wt[h:h + 1, :] * jnp.maximum(_dot_nt(kik, qims[h]), 0.0)
        if on_diagonal:
            score = jnp.where(kc * tkc + key0 <= qry, score, -jnp.inf)
        sc_ref[kc] = score
        return cnts[0] + fold_count(score >= 0.0), cnts[1] + fold_count(score > 0.0)

    n_below = (i * tq) // tkc
    zero_counts = lax.fori_loop(0, n_below, functools.partial(score_body, on_diagonal=False),
                                (jnp.zeros((COUNT_ROWS, tq), F32),) * 2)
    zero_counts = lax.fori_loop(n_below, nk, functools.partial(score_body, on_diagonal=True),
                                zero_counts)
    n_ge0, n_gt0 = [jnp.sum(c, axis=0, keepdims=True) for c in zero_counts]

    def key_to_float(u):
        key = u ^ jnp.int32(INT_MIN)
        bits = jnp.where(key >= 0, key, key ^ jnp.int32(0x7FFFFFFF))
        return lax.bitcast_convert_type(bits, F32)

    def counts(*preds):
        def body(kc, cnts):
            sc = sc_ref[kc]
            return tuple(c + fold_count(pred(sc)) for c, pred in zip(cnts, preds))
        cnts = lax.fori_loop(0, nk, body, (jnp.zeros((COUNT_ROWS, tq), F32),) * len(preds))
        return [jnp.sum(c, axis=0, keepdims=True) for c in cnts]

    few_keys = i * tq + lax.broadcasted_iota(jnp.int32, (1, tq), 1) < top_k - 1
    zero_thr = jnp.logical_and(n_gt0 < k_f, n_ge0 >= k_f)
    zero_tied = jnp.logical_and(zero_thr, n_ge0 > k_f)

    def bit_cond(state):
        step, _, settled = state
        return jnp.logical_and(step < 32, jnp.min(settled) < 0.5)

    def bit_body(state):
        step, prefix, settled = state
        cand_u = prefix | lax.shift_left(jnp.int32(1), 31 - step)
        cand = key_to_float(cand_u)
        (cnt,) = counts(lambda sc: sc >= cand)
        settled = jnp.maximum(settled, jnp.where(cnt == k_f, 1.0, 0.0))
        return step + 1, jnp.where(cnt >= k_f, cand_u, prefix), settled

    steps, prefix, settled = lax.while_loop(
        bit_cond, bit_body,
        (jnp.int32(1), jnp.where(n_ge0 >= k_f, jnp.int32(INT_MIN), jnp.int32(0)),
         jnp.where(jnp.logical_or(few_keys, zero_thr), 1.0, 0.0)))
    thr_key = jnp.maximum(prefix ^ jnp.int32(INT_MIN), jnp.int32(KEY_NEG_FLT_MAX))
    thr = lax.bitcast_convert_type(
        jnp.where(thr_key >= 0, thr_key, thr_key ^ jnp.int32(0x7FFFFFFF)), F32)

    any_tie = jnp.logical_or(jnp.max(jnp.where(zero_tied, 1.0, 0.0)) > 0.5,
                             jnp.min(settled) < 0.5)

    @pl.when(any_tie)
    def _():
        (n_gt,) = counts(lambda sc: sc > thr)
        room = k_f - n_gt
        r_i = lax.broadcasted_iota(jnp.int32, (tkc, tkc), 0)
        c_i = lax.broadcasted_iota(jnp.int32, (tkc, tkc), 1)
        prefix_op = jnp.where(r_i >= c_i, 1.0, 0.0).astype(BF16)

        def drop_body(kc, seen):
            sc = sc_ref[kc]
            tied = sc == thr
            rank = seen + _dot(prefix_op, jnp.where(tied, 1.0, 0.0).astype(BF16))
            sc_ref[kc] = jnp.where(jnp.logical_and(tied, rank > room), -jnp.inf, sc)
            return rank[tkc - 1:tkc, :]

        lax.fori_loop(0, nk, drop_body, jnp.zeros((1, tq), F32))

    ones_rows = jnp.ones((DENOM_ROWS, tkc), BF16)

    def chunk_operands(kc):
        kd = key_chunk(kd_ref, kc)
        vt = jnp.concatenate([vt_ref[0, kc], ones_rows], axis=0)
        bias_ref[...] = jnp.where(sc_ref[kc] >= thr, 0.0, MASKED)
        return kd, vt

    k2_max = jnp.max(kn2_ref[0])
    q2_t = qn2_ref[0].T
    bounds = [jnp.sqrt(q2_t[h:h + 1, :] * k2_max) * BOUND_SAFETY for h in range(n_heads)]
    acc_ref[...] = jnp.zeros(acc_ref.shape, F32)

    def fast_body(kc, carry):
        kd, vt = chunk_operands(kc)

        def prob_pieces(h):
            qm = head_operand(q_ref, h)
            buf = p_bufs[h % len(p_bufs)]

            def piece(r):
                def run():
                    s = bias_ref[r:r + SCORE_PIECE, :] + _dot_nt(kd[r:r + SCORE_PIECE, :], qm)
                    buf[r:r + SCORE_PIECE, :] = jnp.exp2(s - bounds[h]).astype(BF16)
                return run

            return [piece(r) for r in range(0, tkc, SCORE_PIECE)]

        def accumulate(h):
            acc_ref[h] = acc_ref[h] + _dot(vt, p_bufs[h % len(p_bufs)][...])

        _issue_pipelined(n_heads, prob_pieces, accumulate, len(p_bufs))
        return carry

    lax.fori_loop(0, nk, fast_body, 0)
    denom_min = functools.reduce(
        jnp.minimum, [acc_ref[h, HEAD_DIM:HEAD_DIM + 1, :] for h in range(n_heads)])

    @pl.when(jnp.logical_not(jnp.min(denom_min) > DENOM_FLOOR))
    def _():
        _dsa_exact_attention(q_ref, kd_ref, vt_ref, sc_ref, m_ref, acc_ref, bias_ref, s_bufs, p_bufs,
                             thr, nk, tkc, n_heads, head_operand, key_chunk, ones_rows)

    for j in range(n_heads // 2):
        halves = [acc_ref[h, :HEAD_DIM, :] / acc_ref[h, HEAD_DIM:HEAD_DIM + 1, :]
                  for h in (2 * j, 2 * j + 1)]
        o_ref[0, :, j * LANES:(j + 1) * LANES] = jnp.concatenate(halves, axis=0).T.astype(o_ref.dtype)


def _dsa_exact_attention(q_ref, kd_ref, vt_ref, sc_ref, m_ref, acc_ref, bias_ref, s_bufs, p_bufs,
                         thr, nk, tkc, n_heads, head_operand, key_chunk, ones_rows):
    m_ref[...] = jnp.full(m_ref.shape, MASKED, F32)
    acc_ref[...] = jnp.zeros(acc_ref.shape, F32)

    def attn_body(kc, carry):
        kd = key_chunk(kd_ref, kc)
        vt = jnp.concatenate([vt_ref[0, kc], ones_rows], axis=0)
        bias_ref[...] = jnp.where(sc_ref[kc] >= thr, 0.0, MASKED)

        def score_pieces(h):
            qm = head_operand(q_ref, h)
            buf = s_bufs[h % 2]
            maxes = []

            def piece(r):
                def run():
                    s = bias_ref[r:r + SCORE_PIECE, :] + _dot_nt(kd[r:r + SCORE_PIECE, :], qm)
                    buf[r:r + SCORE_PIECE, :] = s
                    maxes.append(_fold_rows(s, jnp.max))
                return run

            return [piece(r) for r in range(0, tkc, SCORE_PIECE)], maxes

        pieces, maxes = score_pieces(0)
        for run in pieces:
            run()
        for h in range(n_heads):
            nxt, nxt_maxes = score_pieces(h + 1) if h + 1 < n_heads else ((), None)
            _online_softmax_step(s_bufs[h % 2], p_bufs[h % 2], maxes, vt, m_ref, None, acc_ref, h,
                                 nxt)
            maxes = nxt_maxes
        return carry

    lax.fori_loop(0, nk, attn_body, 0)


def _dsa_attn_call(q, qi, wit, qn2, kd, kid, kn2, vt, n_idx_heads):
    b, s, a_q = q.shape
    n_heads = a_q // HEAD_DIM
    top_k = min(TOPK_MAX, s // 4)
    nkc, _, tkc = vt.shape[1:]
    tq = tkc
    per = tkc // tq
    kern = functools.partial(_dsa_attn_kernel, tq=tq, tkc=tkc, top_k=top_k, n_heads=n_heads,
                             n_idx_heads=n_idx_heads)
    rowq = lambda bi, i: (bi, i, 0)
    allk = lambda bi, i: (bi, 0, 0)
    return pl.pallas_call(
        kern,
        grid=(b, s // tq),
        in_specs=[
            pl.BlockSpec((1, tq, a_q), rowq),
            pl.BlockSpec((1, tq, qi.shape[-1]), rowq),
            pl.BlockSpec((1, 1, wit.shape[2], tq), lambda bi, i: (bi, i // per, 0, i % per)),
            pl.BlockSpec((1, tq, LANES), rowq),
            pl.BlockSpec((1, s, LANES), allk),
            pl.BlockSpec((1, s, LANES), allk),
            pl.BlockSpec((1,) + kn2.shape[1:], lambda bi, i: (bi, 0, 0, 0)),
            pl.BlockSpec((1, nkc, HEAD_DIM, tkc), lambda bi, i: (bi, 0, 0, 0)),
        ],
        out_specs=pl.BlockSpec((1, tq, a_q), rowq),
        out_shape=jax.ShapeDtypeStruct((b, s, a_q), BF16),
        scratch_shapes=[
            pltpu.VMEM((nkc, tkc, tq), F32),
            pltpu.VMEM((n_heads, 1, tq), F32),
            pltpu.VMEM((n_heads, HEAD_DIM + DENOM_ROWS, tq), F32),
            pltpu.VMEM((tkc, tq), F32),
            pltpu.VMEM((tkc, tq), F32),
            pltpu.VMEM((tkc, tq), F32),
            pltpu.VMEM((tkc, tq), BF16),
            pltpu.VMEM((tkc, tq), BF16),
        ],
        compiler_params=pltpu.CompilerParams(
            dimension_semantics=("arbitrary", "arbitrary"),
            vmem_limit_bytes=_vmem_limit(6 * s * LANES * 2 + tq * s * 4 + 24 * 1024 * 1024)),
        name="dsa_attention",
    )(q, qi, wit, qn2, kd, kid, kn2, vt)


def _pad_to(w, axis, mult):
    n = w.shape[axis]
    pad = (-n) % mult
    if pad == 0:
        return w
    widths = [(0, 0)] * w.ndim
    widths[axis] = (0, pad)
    return jnp.pad(w, widths)


def _prep_dsa_w_in(w, a_q, n_idx_heads):
    o1 = a_q
    o2 = o1 + HEAD_DIM
    o3 = o2 + HEAD_DIM
    o4 = o3 + n_idx_heads * HEAD_DIM
    o5 = o4 + HEAD_DIM
    wq = w[..., :o1] * ATTN_Q_SCALE
    wk, wv, wqi, wki, wwi = w[..., o1:o2], w[..., o2:o3], w[..., o3:o4], w[..., o4:o5], w[..., o5:]
    rows = jnp.concatenate([wq, wqi, wk, wk, wki, wki], axis=-1).astype(BF16)
    cols = jnp.concatenate([wv, _pad_to(wwi, 2, 2 * SUBLANES)], axis=-1)
    return rows, jnp.swapaxes(cols, 1, 2).astype(BF16)


def kernel(x, c, positions, ada_w, ada_b, pre_norm, post_norm, ffn_w_gate, ffn_w_up, ffn_w_down,
           dsa_w_in, dsa_w_out, diff_w_in, diff_w_out, diff_lambda, diff_subln):
    b, s, d = x.shape
    depth = ada_w.shape[0]

    mods = _mod_call(c, ada_w, ada_b).reshape(depth, b, N_MOD, d)
    cos_f, sin_s = _rope_tables(positions)

    wg = _pad_to(ffn_w_gate, 3, MXU_DIM).astype(BF16)
    wu = _pad_to(ffn_w_up, 3, MXU_DIM).astype(BF16)
    wd = _pad_to(ffn_w_down, 2, MXU_DIM).astype(BF16)

    a_q = dsa_w_out.shape[1]
    n_idx_heads = (dsa_w_in.shape[2] - a_q - 3 * HEAD_DIM) // (HEAD_DIM + 1)
    dsa_in, dsa_in_t = _prep_dsa_w_in(dsa_w_in, a_q, n_idx_heads)
    dsa_out = dsa_w_out.astype(BF16)
    idx_q = n_idx_heads * HEAD_DIM
    wi_scale = n_idx_heads ** -0.5 * HEAD_DIM ** -0.5

    b_out = diff_w_out.shape[1]
    b_qk = (diff_w_in.shape[2] - b_out) // 2
    diff_in = jnp.concatenate(
        [diff_w_in[..., :b_qk] * ATTN_Q_SCALE, diff_w_in[..., b_qk:2 * b_qk]],
        axis=-1).astype(BF16)
    diff_in_t = jnp.swapaxes(diff_w_in[..., 2 * b_qk:], 1, 2).astype(BF16)
    diff_out = diff_w_out.astype(BF16)

    for i in range(depth):
        x = _ffn_call(x, mods, pre_norm, post_norm, wg, wu, wd, i, 0, 0)
        j = i // 2
        if i % 2 == 0:
            q, qi, kd, kid, vt, wit, qn2, kn2 = _proj_call(
                x, mods, pre_norm, cos_f, sin_s, dsa_in, dsa_in_t, i, j,
                row_cols=(a_q, idx_q, LANES, LANES),
                t_rows=(HEAD_DIM, 2 * SUBLANES), t_dtypes=(BF16, F32), t_scales=(None, wi_scale),
                norm_of=(0, 2), norm_tile_max=(2,))
            o = _dsa_attn_call(q, qi, wit, qn2, kd, kid, kn2, vt, n_idx_heads)
            x = _outproj_call(o, x, mods, post_norm, dsa_out, i, j)
        else:
            lam_init = 0.8 - 0.6 * math.exp(-0.3 * i)
            q, k, vt, qn2, kn2 = _proj_call(
                x, mods, pre_norm, cos_f, sin_s, diff_in, diff_in_t, i, j,
                row_cols=(b_qk, b_qk), t_rows=(b_out,), t_dtypes=(BF16,), t_scales=(None,),
                norm_of=(0, 1), norm_tile_max=(1,))
            o = _diff_attn_call(q, k, vt, qn2, kn2, diff_lambda, diff_subln, j, lam_init)
            x = _outproj_call(o, x, mods, post_norm, diff_out, i, j)
        x = _ffn_call(x, mods, pre_norm, post_norm, wg, wu, wd, i, 2, 1)
    return x
```

```python
import functools
import math

import jax
import jax.numpy as jnp
from jax import lax
from jax.experimental import pallas as pl
from jax.experimental.pallas import tpu as pltpu

F32 = jnp.float32
BF16 = jnp.bfloat16

EPS = 1e-6
ROPE_THETA = 500000.0
HEAD_DIM = 64
ROT_DIM = HEAD_DIM // 4
ROT_HALF = ROT_DIM // 2
TOPK_MAX = 256
FFN_HALF = 0.5
N_MOD = 9

LANES = 128
SUBLANES = 8
MXU_DIM = 256
VMEM_BYTES_V7X = 64 * 1024 * 1024

ROW_TILE = 512
FFN_SUB_ROWS = 512
DENOM_ROWS = 16
SCORE_PIECE = 128
SOFTMAX_STRIP = 128
SEARCH_STEPS_PER_TEST = 2
COUNT_ROWS = 8

LOG2E = math.log2(math.e)
ATTN_Q_SCALE = HEAD_DIM ** -0.5 * LOG2E

MASKED = -1e30
BOUND_SAFETY = 1.03
DENOM_FLOOR = 2.0 ** -40
INT_MIN = -(2 ** 31)
KEY_NEG_FLT_MAX = -2139095040


def _vmem_limit(nbytes):
    return int(min(nbytes, VMEM_BYTES_V7X - 6 * 1024 * 1024))


def _dot(a, b):
    return jnp.dot(a, b, preferred_element_type=F32)


def _dot_nt(a, b):
    return lax.dot_general(a, b, (((1,), (1,)), ((), ())), preferred_element_type=F32)


def _rms(y):
    return y * lax.rsqrt(jnp.mean(y * y, axis=-1, keepdims=True) + EPS)


def _prenorm_mod(x, gain, shift, scale):
    return (_rms(x) * gain) * (1.0 + scale) + shift


def _silu(g):
    return g / (1.0 + jnp.exp(-g))


def _fold_rows(x, op):
    return op(x.reshape(x.shape[0] // SUBLANES, SUBLANES, x.shape[1]), axis=0)


def _mod_kernel(c_ref, w_ref, b_ref, o_ref):
    cond = _silu(c_ref[...])
    w = w_ref[0]
    c_hi = cond.astype(BF16)
    c_lo = (cond - c_hi.astype(F32)).astype(BF16)
    w_hi = w.astype(BF16)
    w_lo = (w - w_hi.astype(F32)).astype(BF16)
    o_ref[0] = _dot(c_hi, w_hi) + _dot(c_lo, w_hi) + _dot(c_hi, w_lo) + b_ref[0]


def _mod_call(c, ada_w, ada_b):
    depth, d, n = ada_w.shape
    b = c.shape[0]
    tn = 1024 if n % 1024 == 0 else n
    return pl.pallas_call(
        _mod_kernel,
        grid=(depth, n // tn),
        in_specs=[
            pl.BlockSpec((b, d), lambda l, j: (0, 0)),
            pl.BlockSpec((1, d, tn), lambda l, j: (l, 0, j)),
            pl.BlockSpec((1, 1, tn), lambda l, j: (l, 0, j)),
        ],
        out_specs=pl.BlockSpec((1, b, tn), lambda l, j: (l, 0, j)),
        out_shape=jax.ShapeDtypeStruct((depth, b, n), F32),
        compiler_params=pltpu.CompilerParams(
            dimension_semantics=("arbitrary", "arbitrary"),
            vmem_limit_bytes=_vmem_limit(40 * 1024 * 1024)),
        name="adaln_mod",
    )(c, ada_w, ada_b.reshape(depth, 1, n))


def _rope_kernel(ang_ref, cos_ref, sin_ref):
    a = ang_ref[0]
    lane = lax.broadcasted_iota(jnp.int32, a.shape, 1) & (HEAD_DIM - 1)
    rot = lane < ROT_DIM
    cos_ref[0] = jnp.where(rot, jnp.cos(a), 1.0)
    sin_ref[0] = jnp.where(rot, jnp.sin(a), 0.0)


def _rope_tables(positions):
    b, s = positions.shape
    inv = ROPE_THETA ** (-jnp.arange(0, ROT_DIM, 2, dtype=F32) / ROT_DIM)
    head = jnp.concatenate([-inv, inv, jnp.zeros((HEAD_DIM - ROT_DIM,), F32)])
    inv_lanes = jnp.tile(head, LANES // HEAD_DIM)
    ang = positions.astype(F32)[..., None] * inv_lanes
    tm = ROW_TILE if s % ROW_TILE == 0 else s
    spec = pl.BlockSpec((1, tm, LANES), lambda i, j: (i, j, 0))
    return pl.pallas_call(
        _rope_kernel,
        grid=(b, s // tm),
        in_specs=[spec],
        out_specs=[spec, spec],
        out_shape=[jax.ShapeDtypeStruct((b, s, LANES), F32)] * 2,
        compiler_params=pltpu.CompilerParams(dimension_semantics=("arbitrary", "arbitrary")),
        name="rope_tables",
    )(ang)


def _apply_rope(y, cos_f, sin_s, first_half):
    partner = jnp.where(first_half, pltpu.roll(y, LANES - ROT_HALF, 1), pltpu.roll(y, ROT_HALF, 1))
    return y * cos_f + partner * sin_s


def _ffn_kernel(x_ref, mod_ref, pre_ref, post_ref, wg_ref, wu_ref, wd_ref, o_ref, *, sub, f_chunks):
    shift = mod_ref[0, 0, 3 * sub:3 * sub + 1, :]
    scale = mod_ref[0, 0, 3 * sub + 1:3 * sub + 2, :]
    gate = mod_ref[0, 0, 3 * sub + 2:3 * sub + 3, :]
    tm = x_ref.shape[1]
    parts = [(r, r + FFN_SUB_ROWS) for r in range(0, tm, FFN_SUB_ROWS)]
    hs = {}
    ys = {}
    for n, (f0, f1) in enumerate(f_chunks):
        for r0, r1 in parts:
            if n == 0:
                hs[r0] = _prenorm_mod(x_ref[0, r0:r1, :], pre_ref[0, sub:sub + 1, :], shift,
                                      scale).astype(BF16)
            g = _dot(hs[r0], wg_ref[0, 0, :, f0:f1])
            u = _dot(hs[r0], wu_ref[0, 0, :, f0:f1])
            a = (_silu(g) * u).astype(BF16)
            part = _dot(a, wd_ref[0, 0, f0:f1, :])
            ys[r0] = part if n == 0 else ys[r0] + part
    for r0, r1 in parts:
        o_ref[0, r0:r1, :] = x_ref[0, r0:r1, :] + (FFN_HALF * gate) * (
            _rms(ys[r0]) * post_ref[0, sub:sub + 1, :])


def _ffn_call(x, mods, pre_norm, post_norm, wg, wu, wd, layer, sub, which):
    b, s, d = x.shape
    f = wg.shape[-1]
    tm = 2 * FFN_SUB_ROWS if s % (2 * FFN_SUB_ROWS) == 0 else FFN_SUB_ROWS
    assert s % tm == 0
    step = 4 * MXU_DIM
    f_chunks = tuple((f0, min(f0 + step, f)) for f0 in range(0, f, step))
    kern = functools.partial(_ffn_kernel, sub=sub, f_chunks=f_chunks)
    once = pl.Buffered(1)
    return pl.pallas_call(
        kern,
        grid=(b, s // tm),
        in_specs=[
            pl.BlockSpec((1, tm, d), lambda i, j: (i, j, 0)),
            pl.BlockSpec((1, 1, N_MOD, d), lambda i, j: (layer, i, 0, 0)),
            pl.BlockSpec((1, 3, d), lambda i, j: (layer, 0, 0)),
            pl.BlockSpec((1, 3, d), lambda i, j: (layer, 0, 0)),
            pl.BlockSpec((1, 1, d, f), lambda i, j: (layer, which, 0, 0), pipeline_mode=once),
            pl.BlockSpec((1, 1, d, f), lambda i, j: (layer, which, 0, 0), pipeline_mode=once),
            pl.BlockSpec((1, 1, f, d), lambda i, j: (layer, which, 0, 0), pipeline_mode=once),
        ],
        out_specs=pl.BlockSpec((1, tm, d), lambda i, j: (i, j, 0)),
        out_shape=jax.ShapeDtypeStruct((b, s, d), F32),
        compiler_params=pltpu.CompilerParams(
            dimension_semantics=("arbitrary", "arbitrary"),
            vmem_limit_bytes=_vmem_limit(3 * d * f * 2 + 4 * tm * d * 4 + 6 * tm * step * 4
                                         + 8 * 1024 * 1024)),
        name="swiglu_half_step",
    )(x, mods, pre_norm, post_norm, wg, wu, wd)


def _proj_kernel(x_ref, mod_ref, pre_ref, cos_ref, sin_ref, w_ref, wt_ref, *out_refs,
                 n_row_outs, norm_of, norm_tile_max, t_scales):
    x = x_ref[0]
    shift = mod_ref[0, 0, 3:4, :]
    scale = mod_ref[0, 0, 4:5, :]
    h = _prenorm_mod(x, pre_ref[0, 1:2, :], shift, scale).astype(BF16)
    cos_f = cos_ref[0]
    sin_s = sin_ref[0]
    lane = lax.broadcasted_iota(jnp.int32, cos_f.shape, 1) & (HEAD_DIM - 1)
    first_half = lane < ROT_HALF
    norm_refs = out_refs[len(out_refs) - len(norm_of):]
    col = 0
    for n, out_ref in enumerate(out_refs[:n_row_outs]):
        ncols = out_ref.shape[-1]
        y = _dot(h, w_ref[0, :, col:col + ncols])
        squares = []
        for j in range(ncols // LANES):
            blk = _apply_rope(y[:, j * LANES:(j + 1) * LANES], cos_f, sin_s, first_half)
            out_ref[0, :, j * LANES:(j + 1) * LANES] = blk.astype(out_ref.dtype)
            if n in norm_of:
                squares.append((blk * blk).astype(BF16))
        if n in norm_of:
            n_ref = norm_refs[norm_of.index(n)]
            head_of_col = lax.broadcasted_iota(jnp.int32, (ncols, LANES), 0) // HEAD_DIM
            sel = jnp.where(head_of_col == lax.broadcasted_iota(jnp.int32, (ncols, LANES), 1),
                            1.0, 0.0).astype(BF16)
            norms = _dot(jnp.concatenate(squares, axis=1), sel)
            if n in norm_tile_max:
                n_ref[0, 0] = jnp.broadcast_to(jnp.max(norms, axis=0, keepdims=True),
                                               (SUBLANES, LANES))
            else:
                n_ref[0] = norms
        col += ncols
    r = 0
    for out_ref, t_scale in zip(out_refs[n_row_outs:len(out_refs) - len(norm_of)], t_scales):
        nrows = out_ref.shape[-2]
        yt = _dot_nt(wt_ref[0, r:r + nrows, :], h)
        if t_scale is not None:
            yt = yt * t_scale
        out_ref[0, 0] = yt.astype(out_ref.dtype)
        r += nrows


def _proj_call(x, mods, pre_norm, cos_f, sin_s, w, wt, layer, wl, row_cols, t_rows, t_dtypes,
               t_scales, norm_of, norm_tile_max):
    b, s, d = x.shape
    n = w.shape[-1]
    nt = wt.shape[-2]
    tm = ROW_TILE if s % ROW_TILE == 0 else s
    assert all(row_cols[m] // HEAD_DIM <= LANES for m in norm_of)
    kern = functools.partial(_proj_kernel, n_row_outs=len(row_cols), norm_of=norm_of,
                             norm_tile_max=norm_tile_max, t_scales=t_scales)
    tile4 = lambda i, j: (i, j, 0, 0)
    row = lambda i, j: (i, j, 0)
    once = pl.Buffered(1)
    return pl.pallas_call(
        kern,
        grid=(b, s // tm),
        in_specs=[
            pl.BlockSpec((1, tm, d), row),
            pl.BlockSpec((1, 1, N_MOD, d), lambda i, j: (layer, i, 0, 0)),
            pl.BlockSpec((1, 3, d), lambda i, j: (layer, 0, 0)),
            pl.BlockSpec((1, tm, LANES), row),
            pl.BlockSpec((1, tm, LANES), row),
            pl.BlockSpec((1, d, n), lambda i, j: (wl, 0, 0), pipeline_mode=once),
            pl.BlockSpec((1, nt, d), lambda i, j: (wl, 0, 0), pipeline_mode=once),
        ],
        out_specs=([pl.BlockSpec((1, tm, nc), row) for nc in row_cols]
                   + [pl.BlockSpec((1, 1, nr, tm), tile4) for nr in t_rows]
                   + [pl.BlockSpec((1, 1, SUBLANES, LANES), tile4) if m in norm_tile_max
                      else pl.BlockSpec((1, tm, LANES), row) for m in norm_of]),
        out_shape=([jax.ShapeDtypeStruct((b, s, nc), BF16) for nc in row_cols]
                   + [jax.ShapeDtypeStruct((b, s // tm, nr, tm), dt)
                      for nr, dt in zip(t_rows, t_dtypes)]
                   + [jax.ShapeDtypeStruct((b, s // tm, SUBLANES, LANES) if m in norm_tile_max
                                           else (b, s, LANES), F32) for m in norm_of]),
        compiler_params=pltpu.CompilerParams(
            dimension_semantics=("arbitrary", "arbitrary"),
            vmem_limit_bytes=_vmem_limit(d * (n + nt) * 2 + 2 * tm * d * 4 + 6 * tm * (n + nt) * 2
                                         + 6 * tm * 1024 * 4 + 8 * 1024 * 1024)),
        name="mixer_in_proj",
    )(x, mods, pre_norm, cos_f, sin_s, w, wt)


def _outproj_kernel(o_ref, x_ref, mod_ref, post_ref, w_ref, out_ref):
    y = _dot(o_ref[0], w_ref[0])
    gate = mod_ref[0, 0, 5:6, :]
    out_ref[0] = x_ref[0] + gate * (_rms(y) * post_ref[0, 1:2, :])


def _outproj_call(o, x, mods, post_norm, w, layer, wl):
    b, s, d = x.shape
    k = o.shape[-1]
    tm = ROW_TILE if s % ROW_TILE == 0 else s
    row = lambda i, j: (i, j, 0)
    return pl.pallas_call(
        _outproj_kernel,
        grid=(b, s // tm),
        in_specs=[
            pl.BlockSpec((1, tm, k), row),
            pl.BlockSpec((1, tm, d), row),
            pl.BlockSpec((1, 1, N_MOD, d), lambda i, j: (layer, i, 0, 0)),
            pl.BlockSpec((1, 3, d), lambda i, j: (layer, 0, 0)),
            pl.BlockSpec((1, k, d), lambda i, j: (wl, 0, 0), pipeline_mode=pl.Buffered(1)),
        ],
        out_specs=pl.BlockSpec((1, tm, d), row),
        out_shape=jax.ShapeDtypeStruct((b, s, d), F32),
        compiler_params=pltpu.CompilerParams(
            dimension_semantics=("arbitrary", "arbitrary"),
            vmem_limit_bytes=_vmem_limit(k * d * 2 + 8 * tm * d * 4 + 8 * 1024 * 1024)),
        name="mixer_out_proj",
    )(o, x, mods, post_norm, w)


def _issue_pipelined(n_tiles, pieces_of, finish, n_bufs):
    for t in range(n_tiles):
        if t >= n_bufs:
            finish(t - n_bufs)
        for run in pieces_of(t):
            run()
    for t in range(max(0, n_tiles - n_bufs), n_tiles):
        finish(t)


def _online_softmax_step(s_ref, p_ref, tile_max, vt, m_ref, l_ref, acc_ref, slot, interleave=()):
    tk = s_ref.shape[0]
    strips = list(range(0, tk, SOFTMAX_STRIP))
    due = {}
    for j, run in enumerate(interleave):
        due.setdefault((j * len(strips)) // len(interleave), []).append(run)
    m8 = functools.reduce(jnp.maximum, tile_max)
    m_prev = m_ref[slot]
    m_new = jnp.maximum(m_prev, jnp.max(m8, axis=0, keepdims=True))
    alpha = jnp.exp2(m_prev - m_new)
    l8 = None
    for n, r in enumerate(strips):
        for run in due.get(n, ()):
            run()
        p = jnp.exp2(s_ref[r:r + SOFTMAX_STRIP, :] - m_new)
        p_ref[r:r + SOFTMAX_STRIP, :] = p.astype(BF16)
        if l_ref is not None:
            f = _fold_rows(p, jnp.sum)
            l8 = f if l8 is None else l8 + f
    if l_ref is not None:
        l_ref[slot] = alpha * l_ref[slot] + l8
    acc_ref[slot] = alpha * acc_ref[slot] + _dot(vt, p_ref[...])
    m_ref[slot] = m_new


def _diff_attn_kernel(lam_ref, sub_ref, q1_ref, q2_ref, qn_ref, k1_ref, k2_ref, kn_ref, vt_ref,
                      o_ref, m_ref, acc_ref, s0_ref, s1_ref, p0_ref, p1_ref, *,
                      tq, tk, lam_init, pairs):
    s_bufs = (s0_ref, s1_ref)
    p_bufs = (p0_ref, p1_ref)
    i = pl.program_id(2)
    lane = lax.broadcasted_iota(jnp.int32, (tq, LANES), 1)
    low = lane < HEAD_DIM
    v_dim = 2 * HEAD_DIM
    n_slots = 4 * pairs

    def slot_parts(slot):
        return slot // 4, (slot % 4) // 2, slot % 2

    heads = 2 * pairs * pl.num_programs(1)
    q2_t = qn_ref[0].T
    k2_lanes = jnp.max(jnp.max(kn_ref[0], axis=0), axis=0, keepdims=True)
    row_id = lax.broadcasted_iota(jnp.int32, q2_t.shape, 0)
    lane_id = lax.broadcasted_iota(jnp.int32, k2_lanes.shape, 1)
    qms = []
    bounds = []
    for slot in range(n_slots):
        pair, comp, half = slot_parts(slot)
        q = (q1_ref, q2_ref)[comp][0, :, pair * LANES:(pair + 1) * LANES]
        zero = jnp.zeros_like(q)
        qms.append(jnp.where(low, q, zero) if half == 0 else jnp.where(low, zero, q))
        head = comp * heads + 2 * pairs * pl.program_id(1) + 2 * pair + half
        k2_max = jnp.max(jnp.where(lane_id == head, k2_lanes, 0.0))
        q2 = jnp.sum(jnp.where(row_id == head, q2_t, 0.0), axis=0, keepdims=True)
        bounds.append(jnp.sqrt(q2 * k2_max) * BOUND_SAFETY)
    k_refs = (k1_ref, k2_ref)
    ones_rows = jnp.ones((DENOM_ROWS, tk), BF16)

    def chunk_operands(kc):
        off = pl.multiple_of(kc * tk, tk)
        ks = [k_r[0, pl.ds(off, tk), :] for k_r in k_refs]
        vt = vt_ref[0, kc]
        vts = [jnp.concatenate([vt[h * v_dim:(h + 1) * v_dim, :], ones_rows], axis=0)
               for h in range(2 * pairs)]
        return ks, vts

    def masked_scores(ks, slot, r, keep):
        pair, comp, _ = slot_parts(slot)
        k = ks[comp][r:r + SCORE_PIECE, pair * LANES:(pair + 1) * LANES]
        s = _dot_nt(k, qms[slot])
        if keep is not None:
            s = jnp.where(keep[r:r + SCORE_PIECE, :], s, MASKED)
        return s

    def causal_keep(kc):
        key_idx = kc * tk + lax.broadcasted_iota(jnp.int32, (tk, tq), 0)
        qry_idx = i * tq + lax.broadcasted_iota(jnp.int32, (tk, tq), 1)
        return key_idx <= qry_idx

    def sweep(step):
        n_full = (i * tq) // tk

        def full_body(kc, carry):
            step(kc, None)
            return carry

        lax.fori_loop(0, n_full, full_body, 0)
        for d in range(tq // tk):
            step(n_full + d, causal_keep(n_full + d))

    def fast_step(kc, keep):
        ks, vts = chunk_operands(kc)

        def prob_pieces(slot):
            buf = p_bufs[slot % len(p_bufs)]

            def piece(r):
                def run():
                    s = masked_scores(ks, slot, r, keep)
                    buf[r:r + SCORE_PIECE, :] = jnp.exp2(s - bounds[slot]).astype(BF16)
                return run

            return [piece(r) for r in range(0, tk, SCORE_PIECE)]

        def accumulate(slot):
            pair, _, half = slot_parts(slot)
            acc_ref[slot] = acc_ref[slot] + _dot(vts[2 * pair + half],
                                                 p_bufs[slot % len(p_bufs)][...])

        _issue_pipelined(n_slots, prob_pieces, accumulate, len(p_bufs))

    acc_ref[...] = jnp.zeros(acc_ref.shape, F32)
    sweep(fast_step)
    denom_min = functools.reduce(
        jnp.minimum, [acc_ref[slot, v_dim:v_dim + 1, :] for slot in range(n_slots)])

    @pl.when(jnp.logical_not(jnp.min(denom_min) > DENOM_FLOOR))
    def _():
        def exact_step(kc, keep):
            ks, vts = chunk_operands(kc)

            def score_pieces(slot):
                buf = s_bufs[slot % 2]
                maxes = []

                def piece(r):
                    def run():
                        s = masked_scores(ks, slot, r, keep)
                        buf[r:r + SCORE_PIECE, :] = s
                        maxes.append(_fold_rows(s, jnp.max))
                    return run

                return [piece(r) for r in range(0, tk, SCORE_PIECE)], maxes

            pieces, maxes = score_pieces(0)
            for run in pieces:
                run()
            for slot in range(n_slots):
                nxt, nxt_maxes = score_pieces(slot + 1) if slot + 1 < n_slots else ((), None)
                pair, _, half = slot_parts(slot)
                _online_softmax_step(s_bufs[slot % 2], p_bufs[slot % 2], maxes, vts[2 * pair + half],
                                     m_ref, None, acc_ref, slot, nxt)
                maxes = nxt_maxes

        m_ref[...] = jnp.full(m_ref.shape, MASKED, F32)
        acc_ref[...] = jnp.zeros(acc_ref.shape, F32)
        sweep(exact_step)

    lam = lam_ref[0]
    lam_val = (jnp.exp(jnp.sum(lam[0:1] * lam[1:2], axis=1, keepdims=True))
               - jnp.exp(jnp.sum(lam[2:3] * lam[3:4], axis=1, keepdims=True)) + lam_init)
    for pair in range(pairs):
        for half in range(2):
            s1 = 4 * pair + half
            s2 = s1 + 2
            o1 = acc_ref[s1, :v_dim, :] / acc_ref[s1, v_dim:v_dim + 1, :]
            o2 = acc_ref[s2, :v_dim, :] / acc_ref[s2, v_dim:v_dim + 1, :]
            o = (o1 - lam_val * o2).T
            o = _rms(o) * sub_ref[0] * (1.0 - lam_init)
            head = 2 * pair + half
            o_ref[0, :, head * v_dim:(head + 1) * v_dim] = o.astype(o_ref.dtype)


def _diff_attn_call(q, k, vt, qn2, kn2, lam, subln, wl, lam_init):
    b, s, qk = q.shape
    n_pairs = qk // (2 * LANES)
    pairs = 2 if n_pairs % 2 == 0 else 1
    groups = n_pairs // pairs
    nk, _, tk = vt.shape[1:]
    tq = tk
    v_dim = 2 * HEAD_DIM
    n_slots = 4 * pairs
    kern = functools.partial(_diff_attn_kernel, tq=tq, tk=tk, lam_init=lam_init, pairs=pairs)
    return pl.pallas_call(
        kern,
        grid=(b, groups, s // tq),
        in_specs=[
            pl.BlockSpec((1, 4, HEAD_DIM), lambda bi, g, i: (wl, 0, 0)),
            pl.BlockSpec((1, 1, v_dim), lambda bi, g, i: (wl, 0, 0)),
            pl.BlockSpec((1, tq, pairs * LANES), lambda bi, g, i: (bi, i, g)),
            pl.BlockSpec((1, tq, pairs * LANES), lambda bi, g, i: (bi, i, groups + g)),
            pl.BlockSpec((1, tq, LANES), lambda bi, g, i: (bi, i, 0)),
            pl.BlockSpec((1, s, pairs * LANES), lambda bi, g, i: (bi, 0, g)),
            pl.BlockSpec((1, s, pairs * LANES), lambda bi, g, i: (bi, 0, groups + g)),
            pl.BlockSpec((1,) + kn2.shape[1:], lambda bi, g, i: (bi, 0, 0, 0)),
            pl.BlockSpec((1, nk, 2 * pairs * v_dim, tk), lambda bi, g, i: (bi, 0, g, 0)),
        ],
        out_specs=pl.BlockSpec((1, tq, 2 * pairs * v_dim), lambda bi, g, i: (bi, i, g)),
        out_shape=jax.ShapeDtypeStruct((b, s, n_pairs * 2 * v_dim), BF16),
        scratch_shapes=[
            pltpu.VMEM((n_slots, 1, tq), F32),
            pltpu.VMEM((n_slots, v_dim + DENOM_ROWS, tq), F32),
            pltpu.VMEM((tk, tq), F32),
            pltpu.VMEM((tk, tq), F32),
            pltpu.VMEM((tk, tq), BF16),
            pltpu.VMEM((tk, tq), BF16),
        ],
        compiler_params=pltpu.CompilerParams(
            dimension_semantics=("arbitrary", "arbitrary", "arbitrary"),
            vmem_limit_bytes=_vmem_limit(40 * 1024 * 1024)),
        name="diff_attention",
    )(lam, subln.reshape(subln.shape[0], 1, v_dim), q, q, qn2, k, k, kn2, vt)


def _dsa_attn_kernel(q_ref, qi_ref, wi_ref, qn2_ref, kd_ref, kid_ref, kn2_ref, vt_ref, o_ref,
                     sc_ref, m_ref, acc_ref, bias_ref, s0_ref, s1_ref, p0_ref, p1_ref, *,
                     tq, tkc, top_k, n_heads, n_idx_heads):
    s_bufs = (s0_ref, s1_ref)
    p_bufs = (p0_ref, p1_ref)
    i = pl.program_id(1)
    nk = (i * tq + tq + tkc - 1) // tkc
    lane = lax.broadcasted_iota(jnp.int32, (tq, LANES), 1)
    low = lane < HEAD_DIM
    key0 = lax.broadcasted_iota(jnp.int32, (tkc, tq), 0)
    qry = i * tq + lax.broadcasted_iota(jnp.int32, (tkc, tq), 1)
    k_f = float(top_k)

    def head_operand(ref, h):
        pair = ref[0, :, (h // 2) * LANES:(h // 2 + 1) * LANES]
        zero = jnp.zeros_like(pair)
        return jnp.where(low, pair, zero) if h % 2 == 0 else jnp.where(low, zero, pair)

    def key_chunk(ref, kc):
        return ref[0, pl.ds(pl.multiple_of(kc * tkc, tkc), tkc), :]

    wt = wi_ref[0, 0]
    qims = [head_operand(qi_ref, h) for h in range(n_idx_heads)]

    def fold_count(hit):
        return jnp.sum(jnp.where(hit, 1.0, 0.0).reshape(tkc // COUNT_ROWS, COUNT_ROWS, tq), axis=0)

    def score_body(kc, cnts, on_diagonal):
        kik = key_chunk(kid_ref, kc)
        score = jnp.zeros((tkc, tq), F32)
        for h in range(n_idx_heads):
            score = score + wt[h:h + 1, :] * jnp.maximum(_dot_nt(kik, qims[h]), 0.0)
        if on_diagonal:
            score = jnp.where(kc * tkc + key0 <= qry, score, -jnp.inf)
        sc_ref[kc] = score
        return cnts[0] + fold_count(score >= 0.0), cnts[1] + fold_count(score > 0.0)

    n_below = (i * tq) // tkc
    zero_counts = lax.fori_loop(0, n_below, functools.partial(score_body, on_diagonal=False),
                                (jnp.zeros((COUNT_ROWS, tq), F32),) * 2)
    zero_counts = lax.fori_loop(n_below, nk, functools.partial(score_body, on_diagonal=True),
                                zero_counts)
    n_ge0, n_gt0 = [jnp.sum(c, axis=0, keepdims=True) for c in zero_counts]

    def key_to_float(u):
        key = u ^ jnp.int32(INT_MIN)
        bits = jnp.where(key >= 0, key, key ^ jnp.int32(0x7FFFFFFF))
        return lax.bitcast_convert_type(bits, F32)

    def counts(*preds):
        def body(kc, cnts):
            sc = sc_ref[kc]
            return tuple(c + fold_count(pred(sc)) for c, pred in zip(cnts, preds))
        cnts = lax.fori_loop(0, nk, body, (jnp.zeros((COUNT_ROWS, tq), F32),) * len(preds))
        return [jnp.sum(c, axis=0, keepdims=True) for c in cnts]

    few_keys = i * tq + lax.broadcasted_iota(jnp.int32, (1, tq), 1) < top_k - 1
    zero_thr = jnp.logical_and(n_gt0 < k_f, n_ge0 >= k_f)
    zero_tied = jnp.logical_and(zero_thr, n_ge0 > k_f)

    def bit_cond(state):
        step, _, settled = state
        return jnp.logical_and(step < 32, jnp.min(settled) < 0.5)

    def bit_body(state):
        step, prefix, settled = state
        for _ in range(SEARCH_STEPS_PER_TEST):
            bit = jnp.where(step < 32, lax.shift_left(jnp.int32(1), jnp.maximum(31 - step, 0)), 0)
            cand_u = prefix | bit
            cand = key_to_float(cand_u)
            (cnt,) = counts(lambda sc: sc >= cand)
            settled = jnp.maximum(settled, jnp.where(cnt == k_f, 1.0, 0.0))
            prefix = jnp.where(cnt >= k_f, cand_u, prefix)
            step = step + 1
        return step, prefix, settled

    _, prefix, settled = lax.while_loop(
        bit_cond, bit_body,
        (jnp.int32(1), jnp.where(n_ge0 >= k_f, jnp.int32(INT_MIN), jnp.int32(0)),
         jnp.where(jnp.logical_or(few_keys, zero_thr), 1.0, 0.0)))
    thr_key = jnp.maximum(prefix ^ jnp.int32(INT_MIN), jnp.int32(KEY_NEG_FLT_MAX))
    thr = lax.bitcast_convert_type(
        jnp.where(thr_key >= 0, thr_key, thr_key ^ jnp.int32(0x7FFFFFFF)), F32)

    any_tie = jnp.logical_or(jnp.max(jnp.where(zero_tied, 1.0, 0.0)) > 0.5,
                             jnp.min(settled) < 0.5)

    @pl.when(any_tie)
    def _():
        (n_gt,) = counts(lambda sc: sc > thr)
        room = k_f - n_gt
        r_i = lax.broadcasted_iota(jnp.int32, (tkc, tkc), 0)
        c_i = lax.broadcasted_iota(jnp.int32, (tkc, tkc), 1)
        prefix_op = jnp.where(r_i >= c_i, 1.0, 0.0).astype(BF16)

        def drop_body(kc, seen):
            sc = sc_ref[kc]
            tied = sc == thr
            rank = seen + _dot(prefix_op, jnp.where(tied, 1.0, 0.0).astype(BF16))
            sc_ref[kc] = jnp.where(jnp.logical_and(tied, rank > room), -jnp.inf, sc)
            return rank[tkc - 1:tkc, :]

        lax.fori_loop(0, nk, drop_body, jnp.zeros((1, tq), F32))

    ones_rows = jnp.ones((DENOM_ROWS, tkc), BF16)

    def chunk_operands(kc):
        kd = key_chunk(kd_ref, kc)
        vt = jnp.concatenate([vt_ref[0, kc], ones_rows], axis=0)
        bias_ref[...] = jnp.where(sc_ref[kc] >= thr, 0.0, MASKED)
        return kd, vt

    k2_max = jnp.max(kn2_ref[0])
    q2_t = qn2_ref[0].T
    bounds = [jnp.sqrt(q2_t[h:h + 1, :] * k2_max) * BOUND_SAFETY for h in range(n_heads)]
    acc_ref[...] = jnp.zeros(acc_ref.shape, F32)

    def fast_body(kc, carry):
        kd, vt = chunk_operands(kc)

        def prob_pieces(h):
            qm = head_operand(q_ref, h)
            buf = p_bufs[h % len(p_bufs)]

            def piece(r):
                def run():
                    s = bias_ref[r:r + SCORE_PIECE, :] + _dot_nt(kd[r:r + SCORE_PIECE, :], qm)
                    buf[r:r + SCORE_PIECE, :] = jnp.exp2(s - bounds[h]).astype(BF16)
                return run

            return [piece(r) for r in range(0, tkc, SCORE_PIECE)]

        def accumulate(h):
            acc_ref[h] = acc_ref[h] + _dot(vt, p_bufs[h % len(p_bufs)][...])

        _issue_pipelined(n_heads, prob_pieces, accumulate, len(p_bufs))
        return carry

    lax.fori_loop(0, nk, fast_body, 0)
    denom_min = functools.reduce(
        jnp.minimum, [acc_ref[h, HEAD_DIM:HEAD_DIM + 1, :] for h in range(n_heads)])

    @pl.when(jnp.logical_not(jnp.min(denom_min) > DENOM_FLOOR))
    def _():
        _dsa_exact_attention(q_ref, kd_ref, vt_ref, sc_ref, m_ref, acc_ref, bias_ref, s_bufs, p_bufs,
                             thr, nk, tkc, n_heads, head_operand, key_chunk, ones_rows)

    for j in range(n_heads // 2):
        halves = [acc_ref[h, :HEAD_DIM, :] / acc_ref[h, HEAD_DIM:HEAD_DIM + 1, :]
                  for h in (2 * j, 2 * j + 1)]
        o_ref[0, :, j * LANES:(j + 1) * LANES] = jnp.concatenate(halves, axis=0).T.astype(o_ref.dtype)


def _dsa_exact_attention(q_ref, kd_ref, vt_ref, sc_ref, m_ref, acc_ref, bias_ref, s_bufs, p_bufs,
                         thr, nk, tkc, n_heads, head_operand, key_chunk, ones_rows):
    m_ref[...] = jnp.full(m_ref.shape, MASKED, F32)
    acc_ref[...] = jnp.zeros(acc_ref.shape, F32)

    def attn_body(kc, carry):
        kd = key_chunk(kd_ref, kc)
        vt = jnp.concatenate([vt_ref[0, kc], ones_rows], axis=0)
        bias_ref[...] = jnp.where(sc_ref[kc] >= thr, 0.0, MASKED)

        def score_pieces(h):
            qm = head_operand(q_ref, h)
            buf = s_bufs[h % 2]
            maxes = []

            def piece(r):
                def run():
                    s = bias_ref[r:r + SCORE_PIECE, :] + _dot_nt(kd[r:r + SCORE_PIECE, :], qm)
                    buf[r:r + SCORE_PIECE, :] = s
                    maxes.append(_fold_rows(s, jnp.max))
                return run

            return [piece(r) for r in range(0, tkc, SCORE_PIECE)], maxes

        pieces, maxes = score_pieces(0)
        for run in pieces:
            run()
        for h in range(n_heads):
            nxt, nxt_maxes = score_pieces(h + 1) if h + 1 < n_heads else ((), None)
            _online_softmax_step(s_bufs[h % 2], p_bufs[h % 2], maxes, vt, m_ref, None, acc_ref, h,
                                 nxt)
            maxes = nxt_maxes
        return carry

    lax.fori_loop(0, nk, attn_body, 0)


def _dsa_attn_call(q, qi, wit, qn2, kd, kid, kn2, vt, n_idx_heads):
    b, s, a_q = q.shape
    n_heads = a_q // HEAD_DIM
    top_k = min(TOPK_MAX, s // 4)
    nkc, _, tkc = vt.shape[1:]
    tq = tkc
    per = tkc // tq
    kern = functools.partial(_dsa_attn_kernel, tq=tq, tkc=tkc, top_k=top_k, n_heads=n_heads,
                             n_idx_heads=n_idx_heads)
    rowq = lambda bi, i: (bi, i, 0)
    allk = lambda bi, i: (bi, 0, 0)
    return pl.pallas_call(
        kern,
        grid=(b, s // tq),
        in_specs=[
            pl.BlockSpec((1, tq, a_q), rowq),
            pl.BlockSpec((1, tq, qi.shape[-1]), rowq),
            pl.BlockSpec((1, 1, wit.shape[2], tq), lambda bi, i: (bi, i // per, 0, i % per)),
            pl.BlockSpec((1, tq, LANES), rowq),
            pl.BlockSpec((1, s, LANES), allk),
            pl.BlockSpec((1, s, LANES), allk),
            pl.BlockSpec((1,) + kn2.shape[1:], lambda bi, i: (bi, 0, 0, 0)),
            pl.BlockSpec((1, nkc, HEAD_DIM, tkc), lambda bi, i: (bi, 0, 0, 0)),
        ],
        out_specs=pl.BlockSpec((1, tq, a_q), rowq),
        out_shape=jax.ShapeDtypeStruct((b, s, a_q), BF16),
        scratch_shapes=[
            pltpu.VMEM((nkc, tkc, tq), F32),
            pltpu.VMEM((n_heads, 1, tq), F32),
            pltpu.VMEM((n_heads, HEAD_DIM + DENOM_ROWS, tq), F32),
            pltpu.VMEM((tkc, tq), F32),
            pltpu.VMEM((tkc, tq), F32),
            pltpu.VMEM((tkc, tq), F32),
            pltpu.VMEM((tkc, tq), BF16),
            pltpu.VMEM((tkc, tq), BF16),
        ],
        compiler_params=pltpu.CompilerParams(
            dimension_semantics=("arbitrary", "arbitrary"),
            vmem_limit_bytes=_vmem_limit(6 * s * LANES * 2 + tq * s * 4 + 24 * 1024 * 1024)),
        name="dsa_attention",
    )(q, qi, wit, qn2, kd, kid, kn2, vt)


def _pad_to(w, axis, mult):
    n = w.shape[axis]
    pad = (-n) % mult
    if pad == 0:
        return w
    widths = [(0, 0)] * w.ndim
    widths[axis] = (0, pad)
    return jnp.pad(w, widths)


def _prep_dsa_w_in(w, a_q, n_idx_heads):
    o1 = a_q
    o2 = o1 + HEAD_DIM
    o3 = o2 + HEAD_DIM
    o4 = o3 + n_idx_heads * HEAD_DIM
    o5 = o4 + HEAD_DIM
    wq = w[..., :o1] * ATTN_Q_SCALE
    wk, wv, wqi, wki, wwi = w[..., o1:o2], w[..., o2:o3], w[..., o3:o4], w[..., o4:o5], w[..., o5:]
    rows = jnp.concatenate([wq, wqi, wk, wk, wki, wki], axis=-1).astype(BF16)
    cols = jnp.concatenate([wv, _pad_to(wwi, 2, 2 * SUBLANES)], axis=-1)
    return rows, jnp.swapaxes(cols, 1, 2).astype(BF16)


def kernel(x, c, positions, ada_w, ada_b, pre_norm, post_norm, ffn_w_gate, ffn_w_up, ffn_w_down,
           dsa_w_in, dsa_w_out, diff_w_in, diff_w_out, diff_lambda, diff_subln):
    b, s, d = x.shape
    depth = ada_w.shape[0]

    mods = _mod_call(c, ada_w, ada_b).reshape(depth, b, N_MOD, d)
    cos_f, sin_s = _rope_tables(positions)

    wg = _pad_to(ffn_w_gate, 3, MXU_DIM).astype(BF16)
    wu = _pad_to(ffn_w_up, 3, MXU_DIM).astype(BF16)
    wd = _pad_to(ffn_w_down, 2, MXU_DIM).astype(BF16)

    a_q = dsa_w_out.shape[1]
    n_idx_heads = (dsa_w_in.shape[2] - a_q - 3 * HEAD_DIM) // (HEAD_DIM + 1)
    dsa_in, dsa_in_t = _prep_dsa_w_in(dsa_w_in, a_q, n_idx_heads)
    dsa_out = dsa_w_out.astype(BF16)
    idx_q = n_idx_heads * HEAD_DIM
    wi_scale = n_idx_heads ** -0.5 * HEAD_DIM ** -0.5

    b_out = diff_w_out.shape[1]
    b_qk = (diff_w_in.shape[2] - b_out) // 2
    diff_in = jnp.concatenate(
        [diff_w_in[..., :b_qk] * ATTN_Q_SCALE, diff_w_in[..., b_qk:2 * b_qk]],
        axis=-1).astype(BF16)
    diff_in_t = jnp.swapaxes(diff_w_in[..., 2 * b_qk:], 1, 2).astype(BF16)
    diff_out = diff_w_out.astype(BF16)

    for i in range(depth):
        x = _ffn_call(x, mods, pre_norm, post_norm, wg, wu, wd, i, 0, 0)
        j = i // 2
        if i % 2 == 0:
            q, qi, kd, kid, vt, wit, qn2, kn2 = _proj_call(
                x, mods, pre_norm, cos_f, sin_s, dsa_in, dsa_in_t, i, j,
                row_cols=(a_q, idx_q, LANES, LANES),
                t_rows=(HEAD_DIM, 2 * SUBLANES), t_dtypes=(BF16, F32), t_scales=(None, wi_scale),
                norm_of=(0, 2), norm_tile_max=(2,))
            o = _dsa_attn_call(q, qi, wit, qn2, kd, kid, kn2, vt, n_idx_heads)
            x = _outproj_call(o, x, mods, post_norm, dsa_out, i, j)
        else:
            lam_init = 0.8 - 0.6 * math.exp(-0.3 * i)
            q, k, vt, qn2, kn2 = _proj_call(
                x, mods, pre_norm, cos_f, sin_s, diff_in, diff_in_t, i, j,
                row_cols=(b_qk, b_qk), t_rows=(b_out,), t_dtypes=(BF16,), t_scales=(None,),
                norm_of=(0, 1), norm_tile_max=(1,))
            o = _diff_attn_call(q, k, vt, qn2, kn2, diff_lambda, diff_subln, j, lam_init)
            x = _outproj_call(o, x, mods, post_norm, diff_out, i, j)
        x = _ffn_call(x, mods, pre_norm, post_norm, wg, wu, wd, i, 2, 1)
    return x
```

```python
import functools
import math

import jax
import jax.numpy as jnp
from jax import lax
from jax.experimental import pallas as pl
from jax.experimental.pallas import tpu as pltpu

F32 = jnp.float32
BF16 = jnp.bfloat16

EPS = 1e-6
ROPE_THETA = 500000.0
HEAD_DIM = 64
ROT_DIM = HEAD_DIM // 4
ROT_HALF = ROT_DIM // 2
TOPK_MAX = 256
FFN_HALF = 0.5
N_MOD = 9

LANES = 128
SUBLANES = 8
MXU_DIM = 256
VMEM_BYTES_V7X = 64 * 1024 * 1024

ROW_TILE = 512
FFN_SUB_ROWS = 256
FFN_PARTS = 4
DENOM_ROWS = 16
SCORE_PIECE = 128
SOFTMAX_STRIP = 128
SEARCH_STEPS_PER_TEST = 2
COUNT_ROWS = 8

LOG2E = math.log2(math.e)
ATTN_Q_SCALE = HEAD_DIM ** -0.5 * LOG2E

MASKED = -1e30
BOUND_SAFETY = 1.03
DENOM_FLOOR = 2.0 ** -40
INT_MIN = -(2 ** 31)
KEY_NEG_FLT_MAX = -2139095040


def _vmem_limit(nbytes):
    return int(min(nbytes, VMEM_BYTES_V7X - 6 * 1024 * 1024))


def _dot(a, b):
    return jnp.dot(a, b, preferred_element_type=F32)


def _dot_nt(a, b):
    return lax.dot_general(a, b, (((1,), (1,)), ((), ())), preferred_element_type=F32)


def _rms(y):
    return y * lax.rsqrt(jnp.mean(y * y, axis=-1, keepdims=True) + EPS)


def _prenorm_mod(x, gain, shift, scale):
    return (_rms(x) * gain) * (1.0 + scale) + shift


def _silu(g):
    return g / (1.0 + jnp.exp(-g))


def _fold_rows(x, op):
    return op(x.reshape(x.shape[0] // SUBLANES, SUBLANES, x.shape[1]), axis=0)


def _mod_kernel(c_ref, w_ref, b_ref, o_ref):
    cond = _silu(c_ref[...])
    w = w_ref[0]
    c_hi = cond.astype(BF16)
    c_lo = (cond - c_hi.astype(F32)).astype(BF16)
    w_hi = w.astype(BF16)
    w_lo = (w - w_hi.astype(F32)).astype(BF16)
    o_ref[0] = _dot(c_hi, w_hi) + _dot(c_lo, w_hi) + _dot(c_hi, w_lo) + b_ref[0]


def _mod_call(c, ada_w, ada_b):
    depth, d, n = ada_w.shape
    b = c.shape[0]
    tn = 1024 if n % 1024 == 0 else n
    return pl.pallas_call(
        _mod_kernel,
        grid=(depth, n // tn),
        in_specs=[
            pl.BlockSpec((b, d), lambda l, j: (0, 0)),
            pl.BlockSpec((1, d, tn), lambda l, j: (l, 0, j)),
            pl.BlockSpec((1, 1, tn), lambda l, j: (l, 0, j)),
        ],
        out_specs=pl.BlockSpec((1, b, tn), lambda l, j: (l, 0, j)),
        out_shape=jax.ShapeDtypeStruct((depth, b, n), F32),
        compiler_params=pltpu.CompilerParams(
            dimension_semantics=("arbitrary", "arbitrary"),
            vmem_limit_bytes=_vmem_limit(40 * 1024 * 1024)),
        name="adaln_mod",
    )(c, ada_w, ada_b.reshape(depth, 1, n))


def _rope_kernel(ang_ref, cos_ref, sin_ref):
    a = ang_ref[0]
    lane = lax.broadcasted_iota(jnp.int32, a.shape, 1) & (HEAD_DIM - 1)
    rot = lane < ROT_DIM
    cos_ref[0] = jnp.where(rot, jnp.cos(a), 1.0)
    sin_ref[0] = jnp.where(rot, jnp.sin(a), 0.0)


def _rope_tables(positions):
    b, s = positions.shape
    inv = ROPE_THETA ** (-jnp.arange(0, ROT_DIM, 2, dtype=F32) / ROT_DIM)
    head = jnp.concatenate([-inv, inv, jnp.zeros((HEAD_DIM - ROT_DIM,), F32)])
    inv_lanes = jnp.tile(head, LANES // HEAD_DIM)
    ang = positions.astype(F32)[..., None] * inv_lanes
    tm = ROW_TILE if s % ROW_TILE == 0 else s
    spec = pl.BlockSpec((1, tm, LANES), lambda i, j: (i, j, 0))
    return pl.pallas_call(
        _rope_kernel,
        grid=(b, s // tm),
        in_specs=[spec],
        out_specs=[spec, spec],
        out_shape=[jax.ShapeDtypeStruct((b, s, LANES), F32)] * 2,
        compiler_params=pltpu.CompilerParams(dimension_semantics=("arbitrary", "arbitrary")),
        name="rope_tables",
    )(ang)


def _apply_rope(y, cos_f, sin_s, first_half):
    partner = jnp.where(first_half, pltpu.roll(y, LANES - ROT_HALF, 1), pltpu.roll(y, ROT_HALF, 1))
    return y * cos_f + partner * sin_s


def _ffn_kernel(x_ref, mod_ref, pre_ref, post_ref, wg_ref, wu_ref, wd_ref, o_ref, *, sub, f_chunks):
    shift = mod_ref[0, 0, 3 * sub:3 * sub + 1, :]
    scale = mod_ref[0, 0, 3 * sub + 1:3 * sub + 2, :]
    gate = mod_ref[0, 0, 3 * sub + 2:3 * sub + 3, :]
    tm = x_ref.shape[1]
    parts = [(r, r + FFN_SUB_ROWS) for r in range(0, tm, FFN_SUB_ROWS)]
    hs = {}
    ys = {}
    for n, (f0, f1) in enumerate(f_chunks):
        for r0, r1 in parts:
            if n == 0:
                hs[r0] = _prenorm_mod(x_ref[0, r0:r1, :], pre_ref[0, sub:sub + 1, :], shift,
                                      scale).astype(BF16)
            g = _dot(hs[r0], wg_ref[0, 0, :, f0:f1])
            u = _dot(hs[r0], wu_ref[0, 0, :, f0:f1])
            a = (_silu(g) * u).astype(BF16)
            part = _dot(a, wd_ref[0, 0, f0:f1, :])
            ys[r0] = part if n == 0 else ys[r0] + part
    for r0, r1 in parts:
        o_ref[0, r0:r1, :] = x_ref[0, r0:r1, :] + (FFN_HALF * gate) * (
            _rms(ys[r0]) * post_ref[0, sub:sub + 1, :])


def _ffn_call(x, mods, pre_norm, post_norm, wg, wu, wd, layer, sub, which):
    b, s, d = x.shape
    f = wg.shape[-1]
    tm = FFN_PARTS * FFN_SUB_ROWS if s % (FFN_PARTS * FFN_SUB_ROWS) == 0 else FFN_SUB_ROWS
    assert s % tm == 0
    step = 4 * MXU_DIM
    f_chunks = tuple((f0, min(f0 + step, f)) for f0 in range(0, f, step))
    kern = functools.partial(_ffn_kernel, sub=sub, f_chunks=f_chunks)
    once = pl.Buffered(1)
    return pl.pallas_call(
        kern,
        grid=(b, s // tm),
        in_specs=[
            pl.BlockSpec((1, tm, d), lambda i, j: (i, j, 0)),
            pl.BlockSpec((1, 1, N_MOD, d), lambda i, j: (layer, i, 0, 0)),
            pl.BlockSpec((1, 3, d), lambda i, j: (layer, 0, 0)),
            pl.BlockSpec((1, 3, d), lambda i, j: (layer, 0, 0)),
            pl.BlockSpec((1, 1, d, f), lambda i, j: (layer, which, 0, 0), pipeline_mode=once),
            pl.BlockSpec((1, 1, d, f), lambda i, j: (layer, which, 0, 0), pipeline_mode=once),
            pl.BlockSpec((1, 1, f, d), lambda i, j: (layer, which, 0, 0), pipeline_mode=once),
        ],
        out_specs=pl.BlockSpec((1, tm, d), lambda i, j: (i, j, 0)),
        out_shape=jax.ShapeDtypeStruct((b, s, d), F32),
        compiler_params=pltpu.CompilerParams(
            dimension_semantics=("arbitrary", "arbitrary"),
            vmem_limit_bytes=_vmem_limit(3 * d * f * 2 + 4 * tm * d * 4 + 6 * tm * step * 4
                                         + 8 * 1024 * 1024)),
        name="swiglu_half_step",
    )(x, mods, pre_norm, post_norm, wg, wu, wd)


def _proj_kernel(x_ref, mod_ref, pre_ref, cos_ref, sin_ref, w_ref, wt_ref, *out_refs,
                 n_row_outs, norm_of, norm_tile_max, t_scales):
    x = x_ref[0]
    shift = mod_ref[0, 0, 3:4, :]
    scale = mod_ref[0, 0, 4:5, :]
    h = _prenorm_mod(x, pre_ref[0, 1:2, :], shift, scale).astype(BF16)
    cos_f = cos_ref[0]
    sin_s = sin_ref[0]
    lane = lax.broadcasted_iota(jnp.int32, cos_f.shape, 1) & (HEAD_DIM - 1)
    first_half = lane < ROT_HALF
    norm_refs = out_refs[len(out_refs) - len(norm_of):]
    col = 0
    for n, out_ref in enumerate(out_refs[:n_row_outs]):
        ncols = out_ref.shape[-1]
        y = _dot(h, w_ref[0, :, col:col + ncols])
        squares = []
        for j in range(ncols // LANES):
            blk = _apply_rope(y[:, j * LANES:(j + 1) * LANES], cos_f, sin_s, first_half)
            out_ref[0, :, j * LANES:(j + 1) * LANES] = blk.astype(out_ref.dtype)
            if n in norm_of:
                squares.append((blk * blk).astype(BF16))
        if n in norm_of:
            n_ref = norm_refs[norm_of.index(n)]
            head_of_col = lax.broadcasted_iota(jnp.int32, (ncols, LANES), 0) // HEAD_DIM
            sel = jnp.where(head_of_col == lax.broadcasted_iota(jnp.int32, (ncols, LANES), 1),
                            1.0, 0.0).astype(BF16)
            norms = _dot(jnp.concatenate(squares, axis=1), sel)
            if n in norm_tile_max:
                n_ref[0, 0] = jnp.broadcast_to(jnp.max(norms, axis=0, keepdims=True),
                                               (SUBLANES, LANES))
            else:
                n_ref[0] = norms
        col += ncols
    r = 0
    for out_ref, t_scale in zip(out_refs[n_row_outs:len(out_refs) - len(norm_of)], t_scales):
        nrows = out_ref.shape[-2]
        yt = _dot_nt(wt_ref[0, r:r + nrows, :], h)
        if t_scale is not None:
            yt = yt * t_scale
        out_ref[0, 0] = yt.astype(out_ref.dtype)
        r += nrows


def _proj_call(x, mods, pre_norm, cos_f, sin_s, w, wt, layer, wl, row_cols, t_rows, t_dtypes,
               t_scales, norm_of, norm_tile_max):
    b, s, d = x.shape
    n = w.shape[-1]
    nt = wt.shape[-2]
    tm = ROW_TILE if s % ROW_TILE == 0 else s
    assert all(row_cols[m] // HEAD_DIM <= LANES for m in norm_of)
    kern = functools.partial(_proj_kernel, n_row_outs=len(row_cols), norm_of=norm_of,
                             norm_tile_max=norm_tile_max, t_scales=t_scales)
    tile4 = lambda i, j: (i, j, 0, 0)
    row = lambda i, j: (i, j, 0)
    once = pl.Buffered(1)
    return pl.pallas_call(
        kern,
        grid=(b, s // tm),
        in_specs=[
            pl.BlockSpec((1, tm, d), row),
            pl.BlockSpec((1, 1, N_MOD, d), lambda i, j: (layer, i, 0, 0)),
            pl.BlockSpec((1, 3, d), lambda i, j: (layer, 0, 0)),
            pl.BlockSpec((1, tm, LANES), row),
            pl.BlockSpec((1, tm, LANES), row),
            pl.BlockSpec((1, d, n), lambda i, j: (wl, 0, 0), pipeline_mode=once),
            pl.BlockSpec((1, nt, d), lambda i, j: (wl, 0, 0), pipeline_mode=once),
        ],
        out_specs=([pl.BlockSpec((1, tm, nc), row) for nc in row_cols]
                   + [pl.BlockSpec((1, 1, nr, tm), tile4) for nr in t_rows]
                   + [pl.BlockSpec((1, 1, SUBLANES, LANES), tile4) if m in norm_tile_max
                      else pl.BlockSpec((1, tm, LANES), row) for m in norm_of]),
        out_shape=([jax.ShapeDtypeStruct((b, s, nc), BF16) for nc in row_cols]
                   + [jax.ShapeDtypeStruct((b, s // tm, nr, tm), dt)
                      for nr, dt in zip(t_rows, t_dtypes)]
                   + [jax.ShapeDtypeStruct((b, s // tm, SUBLANES, LANES) if m in norm_tile_max
                                           else (b, s, LANES), F32) for m in norm_of]),
        compiler_params=pltpu.CompilerParams(
            dimension_semantics=("arbitrary", "arbitrary"),
            vmem_limit_bytes=_vmem_limit(d * (n + nt) * 2 + 2 * tm * d * 4 + 6 * tm * (n + nt) * 2
                                         + 6 * tm * 1024 * 4 + 8 * 1024 * 1024)),
        name="mixer_in_proj",
    )(x, mods, pre_norm, cos_f, sin_s, w, wt)


def _outproj_kernel(o_ref, x_ref, mod_ref, post_ref, w_ref, out_ref):
    y = _dot(o_ref[0], w_ref[0])
    gate = mod_ref[0, 0, 5:6, :]
    out_ref[0] = x_ref[0] + gate * (_rms(y) * post_ref[0, 1:2, :])


def _outproj_call(o, x, mods, post_norm, w, layer, wl):
    b, s, d = x.shape
    k = o.shape[-1]
    tm = ROW_TILE if s % ROW_TILE == 0 else s
    row = lambda i, j: (i, j, 0)
    return pl.pallas_call(
        _outproj_kernel,
        grid=(b, s // tm),
        in_specs=[
            pl.BlockSpec((1, tm, k), row),
            pl.BlockSpec((1, tm, d), row),
            pl.BlockSpec((1, 1, N_MOD, d), lambda i, j: (layer, i, 0, 0)),
            pl.BlockSpec((1, 3, d), lambda i, j: (layer, 0, 0)),
            pl.BlockSpec((1, k, d), lambda i, j: (wl, 0, 0), pipeline_mode=pl.Buffered(1)),
        ],
        out_specs=pl.BlockSpec((1, tm, d), row),
        out_shape=jax.ShapeDtypeStruct((b, s, d), F32),
        compiler_params=pltpu.CompilerParams(
            dimension_semantics=("arbitrary", "arbitrary"),
            vmem_limit_bytes=_vmem_limit(k * d * 2 + 8 * tm * d * 4 + 8 * 1024 * 1024)),
        name="mixer_out_proj",
    )(o, x, mods, post_norm, w)


def _issue_pipelined(n_tiles, pieces_of, finish, n_bufs):
    for t in range(n_tiles):
        if t >= n_bufs:
            finish(t - n_bufs)
        for run in pieces_of(t):
            run()
    for t in range(max(0, n_tiles - n_bufs), n_tiles):
        finish(t)


def _online_softmax_step(s_ref, p_ref, tile_max, vt, m_ref, l_ref, acc_ref, slot, interleave=()):
    tk = s_ref.shape[0]
    strips = list(range(0, tk, SOFTMAX_STRIP))
    due = {}
    for j, run in enumerate(interleave):
        due.setdefault((j * len(strips)) // len(interleave), []).append(run)
    m8 = functools.reduce(jnp.maximum, tile_max)
    m_prev = m_ref[slot]
    m_new = jnp.maximum(m_prev, jnp.max(m8, axis=0, keepdims=True))
    alpha = jnp.exp2(m_prev - m_new)
    l8 = None
    for n, r in enumerate(strips):
        for run in due.get(n, ()):
            run()
        p = jnp.exp2(s_ref[r:r + SOFTMAX_STRIP, :] - m_new)
        p_ref[r:r + SOFTMAX_STRIP, :] = p.astype(BF16)
        if l_ref is not None:
            f = _fold_rows(p, jnp.sum)
            l8 = f if l8 is None else l8 + f
    if l_ref is not None:
        l_ref[slot] = alpha * l_ref[slot] + l8
    acc_ref[slot] = alpha * acc_ref[slot] + _dot(vt, p_ref[...])
    m_ref[slot] = m_new


def _diff_attn_kernel(lam_ref, sub_ref, q1_ref, q2_ref, qn_ref, k1_ref, k2_ref, kn_ref, vt_ref,
                      o_ref, m_ref, acc_ref, s0_ref, s1_ref, p0_ref, p1_ref, *,
                      tq, tk, lam_init, pairs):
    s_bufs = (s0_ref, s1_ref)
    p_bufs = (p0_ref, p1_ref)
    i = pl.program_id(2)
    lane = lax.broadcasted_iota(jnp.int32, (tq, LANES), 1)
    low = lane < HEAD_DIM
    v_dim = 2 * HEAD_DIM
    n_slots = 4 * pairs

    def slot_parts(slot):
        return slot // 4, (slot % 4) // 2, slot % 2

    heads = 2 * pairs * pl.num_programs(1)
    q2_t = qn_ref[0].T
    k2_lanes = jnp.max(jnp.max(kn_ref[0], axis=0), axis=0, keepdims=True)
    row_id = lax.broadcasted_iota(jnp.int32, q2_t.shape, 0)
    lane_id = lax.broadcasted_iota(jnp.int32, k2_lanes.shape, 1)
    qms = []
    bounds = []
    for slot in range(n_slots):
        pair, comp, half = slot_parts(slot)
        q = (q1_ref, q2_ref)[comp][0, :, pair * LANES:(pair + 1) * LANES]
        zero = jnp.zeros_like(q)
        qms.append(jnp.where(low, q, zero) if half == 0 else jnp.where(low, zero, q))
        head = comp * heads + 2 * pairs * pl.program_id(1) + 2 * pair + half
        k2_max = jnp.max(jnp.where(lane_id == head, k2_lanes, 0.0))
        q2 = jnp.sum(jnp.where(row_id == head, q2_t, 0.0), axis=0, keepdims=True)
        bounds.append(jnp.sqrt(q2 * k2_max) * BOUND_SAFETY)
    k_refs = (k1_ref, k2_ref)
    ones_rows = jnp.ones((DENOM_ROWS, tk), BF16)

    def chunk_operands(kc):
        off = pl.multiple_of(kc * tk, tk)
        ks = [k_r[0, pl.ds(off, tk), :] for k_r in k_refs]
        vt = vt_ref[0, kc]
        vts = [jnp.concatenate([vt[h * v_dim:(h + 1) * v_dim, :], ones_rows], axis=0)
               for h in range(2 * pairs)]
        return ks, vts

    def masked_scores(ks, slot, r, keep):
        pair, comp, _ = slot_parts(slot)
        k = ks[comp][r:r + SCORE_PIECE, pair * LANES:(pair + 1) * LANES]
        s = _dot_nt(k, qms[slot])
        if keep is not None:
            s = jnp.where(keep[r:r + SCORE_PIECE, :], s, MASKED)
        return s

    def causal_keep(kc):
        key_idx = kc * tk + lax.broadcasted_iota(jnp.int32, (tk, tq), 0)
        qry_idx = i * tq + lax.broadcasted_iota(jnp.int32, (tk, tq), 1)
        return key_idx <= qry_idx

    def sweep(step):
        n_full = (i * tq) // tk

        def full_body(kc, carry):
            step(kc, None)
            return carry

        lax.fori_loop(0, n_full, full_body, 0)
        for d in range(tq // tk):
            step(n_full + d, causal_keep(n_full + d))

    def fast_step(kc, keep):
        ks, vts = chunk_operands(kc)

        def prob_pieces(slot):
            buf = p_bufs[slot % len(p_bufs)]

            def piece(r):
                def run():
                    s = masked_scores(ks, slot, r, keep)
                    buf[r:r + SCORE_PIECE, :] = jnp.exp2(s - bounds[slot]).astype(BF16)
                return run

            return [piece(r) for r in range(0, tk, SCORE_PIECE)]

        def accumulate(slot):
            pair, _, half = slot_parts(slot)
            acc_ref[slot] = acc_ref[slot] + _dot(vts[2 * pair + half],
                                                 p_bufs[slot % len(p_bufs)][...])

        _issue_pipelined(n_slots, prob_pieces, accumulate, len(p_bufs))

    acc_ref[...] = jnp.zeros(acc_ref.shape, F32)
    sweep(fast_step)
    denom_min = functools.reduce(
        jnp.minimum, [acc_ref[slot, v_dim:v_dim + 1, :] for slot in range(n_slots)])

    @pl.when(jnp.logical_not(jnp.min(denom_min) > DENOM_FLOOR))
    def _():
        def exact_step(kc, keep):
            ks, vts = chunk_operands(kc)

            def score_pieces(slot):
                buf = s_bufs[slot % 2]
                maxes = []

                def piece(r):
                    def run():
                        s = masked_scores(ks, slot, r, keep)
                        buf[r:r + SCORE_PIECE, :] = s
                        maxes.append(_fold_rows(s, jnp.max))
                    return run

                return [piece(r) for r in range(0, tk, SCORE_PIECE)], maxes

            pieces, maxes = score_pieces(0)
            for run in pieces:
                run()
            for slot in range(n_slots):
                nxt, nxt_maxes = score_pieces(slot + 1) if slot + 1 < n_slots else ((), None)
                pair, _, half = slot_parts(slot)
                _online_softmax_step(s_bufs[slot % 2], p_bufs[slot % 2], maxes, vts[2 * pair + half],
                                     m_ref, None, acc_ref, slot, nxt)
                maxes = nxt_maxes

        m_ref[...] = jnp.full(m_ref.shape, MASKED, F32)
        acc_ref[...] = jnp.zeros(acc_ref.shape, F32)
        sweep(exact_step)

    lam = lam_ref[0]
    lam_val = (jnp.exp(jnp.sum(lam[0:1] * lam[1:2], axis=1, keepdims=True))
               - jnp.exp(jnp.sum(lam[2:3] * lam[3:4], axis=1, keepdims=True)) + lam_init)
    for pair in range(pairs):
        for half in range(2):
            s1 = 4 * pair + half
            s2 = s1 + 2
            o1 = acc_ref[s1, :v_dim, :] / acc_ref[s1, v_dim:v_dim + 1, :]
            o2 = acc_ref[s2, :v_dim, :] / acc_ref[s2, v_dim:v_dim + 1, :]
            o = (o1 - lam_val * o2).T
            o = _rms(o) * sub_ref[0] * (1.0 - lam_init)
            head = 2 * pair + half
            o_ref[0, :, head * v_dim:(head + 1) * v_dim] = o.astype(o_ref.dtype)


def _diff_attn_call(q, k, vt, qn2, kn2, lam, subln, wl, lam_init):
    b, s, qk = q.shape
    n_pairs = qk // (2 * LANES)
    pairs = 2 if n_pairs % 2 == 0 else 1
    groups = n_pairs // pairs
    nk, _, tk = vt.shape[1:]
    tq = tk
    v_dim = 2 * HEAD_DIM
    n_slots = 4 * pairs
    kern = functools.partial(_diff_attn_kernel, tq=tq, tk=tk, lam_init=lam_init, pairs=pairs)
    return pl.pallas_call(
        kern,
        grid=(b, groups, s // tq),
        in_specs=[
            pl.BlockSpec((1, 4, HEAD_DIM), lambda bi, g, i: (wl, 0, 0)),
            pl.BlockSpec((1, 1, v_dim), lambda bi, g, i: (wl, 0, 0)),
            pl.BlockSpec((1, tq, pairs * LANES), lambda bi, g, i: (bi, i, g)),
            pl.BlockSpec((1, tq, pairs * LANES), lambda bi, g, i: (bi, i, groups + g)),
            pl.BlockSpec((1, tq, LANES), lambda bi, g, i: (bi, i, 0)),
            pl.BlockSpec((1, s, pairs * LANES), lambda bi, g, i: (bi, 0, g)),
            pl.BlockSpec((1, s, pairs * LANES), lambda bi, g, i: (bi, 0, groups + g)),
            pl.BlockSpec((1,) + kn2.shape[1:], lambda bi, g, i: (bi, 0, 0, 0)),
            pl.BlockSpec((1, nk, 2 * pairs * v_dim, tk), lambda bi, g, i: (bi, 0, g, 0)),
        ],
        out_specs=pl.BlockSpec((1, tq, 2 * pairs * v_dim), lambda bi, g, i: (bi, i, g)),
        out_shape=jax.ShapeDtypeStruct((b, s, n_pairs * 2 * v_dim), BF16),
        scratch_shapes=[
            pltpu.VMEM((n_slots, 1, tq), F32),
            pltpu.VMEM((n_slots, v_dim + DENOM_ROWS, tq), F32),
            pltpu.VMEM((tk, tq), F32),
            pltpu.VMEM((tk, tq), F32),
            pltpu.VMEM((tk, tq), BF16),
            pltpu.VMEM((tk, tq), BF16),
        ],
        compiler_params=pltpu.CompilerParams(
            dimension_semantics=("arbitrary", "arbitrary", "arbitrary"),
            vmem_limit_bytes=_vmem_limit(40 * 1024 * 1024)),
        name="diff_attention",
    )(lam, subln.reshape(subln.shape[0], 1, v_dim), q, q, qn2, k, k, kn2, vt)


def _dsa_attn_kernel(q_ref, qi_ref, wi_ref, qn2_ref, kd_ref, kid_ref, kn2_ref, vt_ref, o_ref,
                     sc_ref, m_ref, acc_ref, bias_ref, s0_ref, s1_ref, p0_ref, p1_ref, *,
                     tq, tkc, top_k, n_heads, n_idx_heads):
    s_bufs = (s0_ref, s1_ref)
    p_bufs = (p0_ref, p1_ref)
    i = pl.program_id(1)
    nk = (i * tq + tq + tkc - 1) // tkc
    lane = lax.broadcasted_iota(jnp.int32, (tq, LANES), 1)
    low = lane < HEAD_DIM
    key0 = lax.broadcasted_iota(jnp.int32, (tkc, tq), 0)
    qry = i * tq + lax.broadcasted_iota(jnp.int32, (tkc, tq), 1)
    k_f = float(top_k)

    def head_operand(ref, h):
        pair = ref[0, :, (h // 2) * LANES:(h // 2 + 1) * LANES]
        zero = jnp.zeros_like(pair)
        return jnp.where(low, pair, zero) if h % 2 == 0 else jnp.where(low, zero, pair)

    def key_chunk(ref, kc):
        return ref[0, pl.ds(pl.multiple_of(kc * tkc, tkc), tkc), :]

    wt = wi_ref[0, 0]
    qims = [head_operand(qi_ref, h) for h in range(n_idx_heads)]

    def fold_count(hit):
        return jnp.sum(jnp.where(hit, 1.0, 0.0).reshape(tkc // COUNT_ROWS, COUNT_ROWS, tq), axis=0)

    def score_body(kc, cnts, on_diagonal):
        kik = key_chunk(kid_ref, kc)
        score = jnp.zeros((tkc, tq), F32)
        for h in range(n_idx_heads):
            score = score + wt[h:h + 1, :] * jnp.maximum(_dot_nt(kik, qims[h]), 0.0)
        if on_diagonal:
            score = jnp.where(kc * tkc + key0 <= qry, score, -jnp.inf)
        sc_ref[kc] = score
        return cnts[0] + fold_count(score >= 0.0), cnts[1] + fold_count(score > 0.0)

    n_below = (i * tq) // tkc
    zero_counts = lax.fori_loop(0, n_below, functools.partial(score_body, on_diagonal=False),
                                (jnp.zeros((COUNT_ROWS, tq), F32),) * 2)
    zero_counts = lax.fori_loop(n_below, nk, functools.partial(score_body, on_diagonal=True),
                                zero_counts)
    n_ge0, n_gt0 = [jnp.sum(c, axis=0, keepdims=True) for c in zero_counts]

    def key_to_float(u):
        key = u ^ jnp.int32(INT_MIN)
        bits = jnp.where(key >= 0, key, key ^ jnp.int32(0x7FFFFFFF))
        return lax.bitcast_convert_type(bits, F32)

    def counts(*preds):
        def body(kc, cnts):
            sc = sc_ref[kc]
            return tuple(c + fold_count(pred(sc)) for c, pred in zip(cnts, preds))
        cnts = lax.fori_loop(0, nk, body, (jnp.zeros((COUNT_ROWS, tq), F32),) * len(preds))
        return [jnp.sum(c, axis=0, keepdims=True) for c in cnts]

    few_keys = i * tq + lax.broadcasted_iota(jnp.int32, (1, tq), 1) < top_k - 1
    zero_thr = jnp.logical_and(n_gt0 < k_f, n_ge0 >= k_f)
    zero_tied = jnp.logical_and(zero_thr, n_ge0 > k_f)

    def bit_cond(state):
        step, _, settled = state
        return jnp.logical_and(step < 32, jnp.min(settled) < 0.5)

    def bit_body(state):
        step, prefix, settled = state
        for _ in range(SEARCH_STEPS_PER_TEST):
            bit = jnp.where(step < 32, lax.shift_left(jnp.int32(1), jnp.maximum(31 - step, 0)), 0)
            cand_u = prefix | bit
            cand = key_to_float(cand_u)
            (cnt,) = counts(lambda sc: sc >= cand)
            settled = jnp.maximum(settled, jnp.where(cnt == k_f, 1.0, 0.0))
            prefix = jnp.where(cnt >= k_f, cand_u, prefix)
            step = step + 1
        return step, prefix, settled

    _, prefix, settled = lax.while_loop(
        bit_cond, bit_body,
        (jnp.int32(1), jnp.where(n_ge0 >= k_f, jnp.int32(INT_MIN), jnp.int32(0)),
         jnp.where(jnp.logical_or(few_keys, zero_thr), 1.0, 0.0)))
    thr_key = jnp.maximum(prefix ^ jnp.int32(INT_MIN), jnp.int32(KEY_NEG_FLT_MAX))
    thr = lax.bitcast_convert_type(
        jnp.where(thr_key >= 0, thr_key, thr_key ^ jnp.int32(0x7FFFFFFF)), F32)

    any_tie = jnp.logical_or(jnp.max(jnp.where(zero_tied, 1.0, 0.0)) > 0.5,
                             jnp.min(settled) < 0.5)

    @pl.when(any_tie)
    def _():
        (n_gt,) = counts(lambda sc: sc > thr)
        room = k_f - n_gt
        r_i = lax.broadcasted_iota(jnp.int32, (tkc, tkc), 0)
        c_i = lax.broadcasted_iota(jnp.int32, (tkc, tkc), 1)
        prefix_op = jnp.where(r_i >= c_i, 1.0, 0.0).astype(BF16)

        def drop_body(kc, seen):
            sc = sc_ref[kc]
            tied = sc == thr
            rank = seen + _dot(prefix_op, jnp.where(tied, 1.0, 0.0).astype(BF16))
            sc_ref[kc] = jnp.where(jnp.logical_and(tied, rank > room), -jnp.inf, sc)
            return rank[tkc - 1:tkc, :]

        lax.fori_loop(0, nk, drop_body, jnp.zeros((1, tq), F32))

    ones_rows = jnp.ones((DENOM_ROWS, tkc), BF16)

    def chunk_operands(kc):
        kd = key_chunk(kd_ref, kc)
        vt = jnp.concatenate([vt_ref[0, kc], ones_rows], axis=0)
        bias_ref[...] = jnp.where(sc_ref[kc] >= thr, 0.0, MASKED)
        return kd, vt

    k2_max = jnp.max(kn2_ref[0])
    q2_t = qn2_ref[0].T
    bounds = [jnp.sqrt(q2_t[h:h + 1, :] * k2_max) * BOUND_SAFETY for h in range(n_heads)]
    acc_ref[...] = jnp.zeros(acc_ref.shape, F32)

    def fast_body(kc, carry):
        kd, vt = chunk_operands(kc)

        def prob_pieces(h):
            qm = head_operand(q_ref, h)
            buf = p_bufs[h % len(p_bufs)]

            def piece(r):
                def run():
                    s = bias_ref[r:r + SCORE_PIECE, :] + _dot_nt(kd[r:r + SCORE_PIECE, :], qm)
                    buf[r:r + SCORE_PIECE, :] = jnp.exp2(s - bounds[h]).astype(BF16)
                return run

            return [piece(r) for r in range(0, tkc, SCORE_PIECE)]

        def accumulate(h):
            acc_ref[h] = acc_ref[h] + _dot(vt, p_bufs[h % len(p_bufs)][...])

        _issue_pipelined(n_heads, prob_pieces, accumulate, len(p_bufs))
        return carry

    lax.fori_loop(0, nk, fast_body, 0)
    denom_min = functools.reduce(
        jnp.minimum, [acc_ref[h, HEAD_DIM:HEAD_DIM + 1, :] for h in range(n_heads)])

    @pl.when(jnp.logical_not(jnp.min(denom_min) > DENOM_FLOOR))
    def _():
        _dsa_exact_attention(q_ref, kd_ref, vt_ref, sc_ref, m_ref, acc_ref, bias_ref, s_bufs, p_bufs,
                             thr, nk, tkc, n_heads, head_operand, key_chunk, ones_rows)

    for j in range(n_heads // 2):
        halves = [acc_ref[h, :HEAD_DIM, :] / acc_ref[h, HEAD_DIM:HEAD_DIM + 1, :]
                  for h in (2 * j, 2 * j + 1)]
        o_ref[0, :, j * LANES:(j + 1) * LANES] = jnp.concatenate(halves, axis=0).T.astype(o_ref.dtype)


def _dsa_exact_attention(q_ref, kd_ref, vt_ref, sc_ref, m_ref, acc_ref, bias_ref, s_bufs, p_bufs,
                         thr, nk, tkc, n_heads, head_operand, key_chunk, ones_rows):
    m_ref[...] = jnp.full(m_ref.shape, MASKED, F32)
    acc_ref[...] = jnp.zeros(acc_ref.shape, F32)

    def attn_body(kc, carry):
        kd = key_chunk(kd_ref, kc)
        vt = jnp.concatenate([vt_ref[0, kc], ones_rows], axis=0)
        bias_ref[...] = jnp.where(sc_ref[kc] >= thr, 0.0, MASKED)

        def score_pieces(h):
            qm = head_operand(q_ref, h)
            buf = s_bufs[h % 2]
            maxes = []

            def piece(r):
                def run():
                    s = bias_ref[r:r + SCORE_PIECE, :] + _dot_nt(kd[r:r + SCORE_PIECE, :], qm)
                    buf[r:r + SCORE_PIECE, :] = s
                    maxes.append(_fold_rows(s, jnp.max))
                return run

            return [piece(r) for r in range(0, tkc, SCORE_PIECE)], maxes

        pieces, maxes = score_pieces(0)
        for run in pieces:
            run()
        for h in range(n_heads):
            nxt, nxt_maxes = score_pieces(h + 1) if h + 1 < n_heads else ((), None)
            _online_softmax_step(s_bufs[h % 2], p_bufs[h % 2], maxes, vt, m_ref, None, acc_ref, h,
                                 nxt)
            maxes = nxt_maxes
        return carry

    lax.fori_loop(0, nk, attn_body, 0)


def _dsa_attn_call(q, qi, wit, qn2, kd, kid, kn2, vt, n_idx_heads):
    b, s, a_q = q.shape
    n_heads = a_q // HEAD_DIM
    top_k = min(TOPK_MAX, s // 4)
    nkc, _, tkc = vt.shape[1:]
    tq = tkc
    per = tkc // tq
    kern = functools.partial(_dsa_attn_kernel, tq=tq, tkc=tkc, top_k=top_k, n_heads=n_heads,
                             n_idx_heads=n_idx_heads)
    rowq = lambda bi, i: (bi, i, 0)
    allk = lambda bi, i: (bi, 0, 0)
    return pl.pallas_call(
        kern,
        grid=(b, s // tq),
        in_specs=[
            pl.BlockSpec((1, tq, a_q), rowq),
            pl.BlockSpec((1, tq, qi.shape[-1]), rowq),
            pl.BlockSpec((1, 1, wit.shape[2], tq), lambda bi, i: (bi, i // per, 0, i % per)),
            pl.BlockSpec((1, tq, LANES), rowq),
            pl.BlockSpec((1, s, LANES), allk),
            pl.BlockSpec((1, s, LANES), allk),
            pl.BlockSpec((1,) + kn2.shape[1:], lambda bi, i: (bi, 0, 0, 0)),
            pl.BlockSpec((1, nkc, HEAD_DIM, tkc), lambda bi, i: (bi, 0, 0, 0)),
        ],
        out_specs=pl.BlockSpec((1, tq, a_q), rowq),
        out_shape=jax.ShapeDtypeStruct((b, s, a_q), BF16),
        scratch_shapes=[
            pltpu.VMEM((nkc, tkc, tq), F32),
            pltpu.VMEM((n_heads, 1, tq), F32),
            pltpu.VMEM((n_heads, HEAD_DIM + DENOM_ROWS, tq), F32),
            pltpu.VMEM((tkc, tq), F32),
            pltpu.VMEM((tkc, tq), F32),
            pltpu.VMEM((tkc, tq), F32),
            pltpu.VMEM((tkc, tq), BF16),
            pltpu.VMEM((tkc, tq), BF16),
        ],
        compiler_params=pltpu.CompilerParams(
            dimension_semantics=("arbitrary", "arbitrary"),
            vmem_limit_bytes=_vmem_limit(6 * s * LANES * 2 + tq * s * 4 + 24 * 1024 * 1024)),
        name="dsa_attention",
    )(q, qi, wit, qn2, kd, kid, kn2, vt)


def _pad_to(w, axis, mult):
    n = w.shape[axis]
    pad = (-n) % mult
    if pad == 0:
        return w
    widths = [(0, 0)] * w.ndim
    widths[axis] = (0, pad)
    return jnp.pad(w, widths)


def _prep_dsa_w_in(w, a_q, n_idx_heads):
    o1 = a_q
    o2 = o1 + HEAD_DIM
    o3 = o2 + HEAD_DIM
    o4 = o3 + n_idx_heads * HEAD_DIM
    o5 = o4 + HEAD_DIM
    wq = w[..., :o1] * ATTN_Q_SCALE
    wk, wv, wqi, wki, wwi = w[..., o1:o2], w[..., o2:o3], w[..., o3:o4], w[..., o4:o5], w[..., o5:]
    rows = jnp.concatenate([wq, wqi, wk, wk, wki, wki], axis=-1).astype(BF16)
    cols = jnp.concatenate([wv, _pad_to(wwi, 2, 2 * SUBLANES)], axis=-1)
    return rows, jnp.swapaxes(cols, 1, 2).astype(BF16)


def kernel(x, c, positions, ada_w, ada_b, pre_norm, post_norm, ffn_w_gate, ffn_w_up, ffn_w_down,
           dsa_w_in, dsa_w_out, diff_w_in, diff_w_out, diff_lambda, diff_subln):
    b, s, d = x.shape
    depth = ada_w.shape[0]

    mods = _mod_call(c, ada_w, ada_b).reshape(depth, b, N_MOD, d)
    cos_f, sin_s = _rope_tables(positions)

    wg = _pad_to(ffn_w_gate, 3, MXU_DIM).astype(BF16)
    wu = _pad_to(ffn_w_up, 3, MXU_DIM).astype(BF16)
    wd = _pad_to(ffn_w_down, 2, MXU_DIM).astype(BF16)

    a_q = dsa_w_out.shape[1]
    n_idx_heads = (dsa_w_in.shape[2] - a_q - 3 * HEAD_DIM) // (HEAD_DIM + 1)
    dsa_in, dsa_in_t = _prep_dsa_w_in(dsa_w_in, a_q, n_idx_heads)
    dsa_out = dsa_w_out.astype(BF16)
    idx_q = n_idx_heads * HEAD_DIM
    wi_scale = n_idx_heads ** -0.5 * HEAD_DIM ** -0.5

    b_out = diff_w_out.shape[1]
    b_qk = (diff_w_in.shape[2] - b_out) // 2
    diff_in = jnp.concatenate(
        [diff_w_in[..., :b_qk] * ATTN_Q_SCALE, diff_w_in[..., b_qk:2 * b_qk]],
        axis=-1).astype(BF16)
    diff_in_t = jnp.swapaxes(diff_w_in[..., 2 * b_qk:], 1, 2).astype(BF16)
    diff_out = diff_w_out.astype(BF16)

    for i in range(depth):
        x = _ffn_call(x, mods, pre_norm, post_norm, wg, wu, wd, i, 0, 0)
        j = i // 2
        if i % 2 == 0:
            q, qi, kd, kid, vt, wit, qn2, kn2 = _proj_call(
                x, mods, pre_norm, cos_f, sin_s, dsa_in, dsa_in_t, i, j,
                row_cols=(a_q, idx_q, LANES, LANES),
                t_rows=(HEAD_DIM, 2 * SUBLANES), t_dtypes=(BF16, F32), t_scales=(None, wi_scale),
                norm_of=(0, 2), norm_tile_max=(2,))
            o = _dsa_attn_call(q, qi, wit, qn2, kd, kid, kn2, vt, n_idx_heads)
            x = _outproj_call(o, x, mods, post_norm, dsa_out, i, j)
        else:
            lam_init = 0.8 - 0.6 * math.exp(-0.3 * i)
            q, k, vt, qn2, kn2 = _proj_call(
                x, mods, pre_norm, cos_f, sin_s, diff_in, diff_in_t, i, j,
                row_cols=(b_qk, b_qk), t_rows=(b_out,), t_dtypes=(BF16,), t_scales=(None,),
                norm_of=(0, 1), norm_tile_max=(1,))
            o = _diff_attn_call(q, k, vt, qn2, kn2, diff_lambda, diff_subln, j, lam_init)
            x = _outproj_call(o, x, mods, post_norm, diff_out, i, j)
        x = _ffn_call(x, mods, pre_norm, post_norm, wg, wu, wd, i, 2, 1)
    return x
```
